```python
import jax, jax.numpy as jnp
from jax import lax
import numpy as np

D_MODEL = 1024
BATCH = 2
SEQ = 8192
DEPTH = 2

CTX_LEN = 256
GRID_W = 64
D_MIX = D_MODEL
W_ATTN = D_MIX // 4
W_FOURIER = D_MIX // 4
W_CONV = D_MIX // 4
W_POOL = D_MIX // 4
HEAD_DIM = 64
N_Q_HEADS = W_ATTN // HEAD_DIM
N_KV_HEADS = 2
Q_PER_KV = N_Q_HEADS // N_KV_HEADS
N_FOURIER_HEADS = 4
FOURIER_HEAD_DIM = W_FOURIER // N_FOURIER_HEADS
CONV_WIDTH = 31
POOL_WINDOWS = (2, 4, 8, 16)
POOL_GROUP = W_POOL // len(POOL_WINDOWS)
D_FF = -(-(8 * D_MODEL) // (3 * 256)) * 256
ROPE_THETA = 10000.0
Q_BLOCK = 128
EPS = 1e-6
ATTN_SCALE = HEAD_DIM ** -0.5

OFF_Q = 0
OFF_K = OFF_Q + N_Q_HEADS * HEAD_DIM
OFF_V = OFF_K + N_KV_HEADS * HEAD_DIM
OFF_F = OFF_V + N_KV_HEADS * HEAD_DIM
OFF_C = OFF_F + W_FOURIER
OFF_P = OFF_C + 2 * W_CONV
D_IN = OFF_P + W_POOL

kernel_name = 'hymba_style_fourier_conv_pool_gqa_dit_block'


def rms_norm(x, g):
    xf = x.astype(jnp.float32)
    y = xf * lax.rsqrt(jnp.mean(xf * xf, axis=-1, keepdims=True) + EPS)
    return (y * g.astype(jnp.float32)).astype(x.dtype)


def layer_norm(x, g, b):
    xf = x.astype(jnp.float32)
    xc = xf - jnp.mean(xf, axis=-1, keepdims=True)
    y = xc * lax.rsqrt(jnp.mean(xc * xc, axis=-1, keepdims=True) + EPS)
    return (y * g.astype(jnp.float32) + b.astype(jnp.float32)).astype(x.dtype)


def axial_rope_tables(n, dtype):
    rows = n // GRID_W
    row = jnp.repeat(jnp.arange(rows, dtype=jnp.float32), GRID_W)
    col = jnp.tile(jnp.arange(GRID_W, dtype=jnp.float32), rows)
    n_freq = HEAD_DIM // 4
    inv_freq = ROPE_THETA ** (-jnp.arange(n_freq, dtype=jnp.float32) / n_freq)
    ang = jnp.concatenate([row[:, None] * inv_freq, col[:, None] * inv_freq], axis=-1)
    return jnp.cos(ang).astype(dtype), jnp.sin(ang).astype(dtype)


def apply_rope(x, cos, sin):
    xp = x.reshape(x.shape[:-1] + (HEAD_DIM // 2, 2))
    x0, x1 = xp[..., 0], xp[..., 1]
    cs = cos[None, :, None, :]
    sn = sin[None, :, None, :]
    return jnp.stack([x0 * cs - x1 * sn, x0 * sn + x1 * cs], axis=-1).reshape(x.shape)


def q_heads(pq, q_g):
    b, n, _ = pq.shape
    return rms_norm(pq.reshape(b, n, N_Q_HEADS, HEAD_DIM), q_g)


def kv_heads(pkv, k_g):
    b, n, _ = pkv.shape
    kv = pkv.reshape(b, n, 2, N_KV_HEADS, HEAD_DIM)
    return rms_norm(kv[:, :, 0], k_g), kv[:, :, 1]


def attend(q, k, v):
    b, nq = q.shape[:2]
    qg = q.reshape(b, nq, N_KV_HEADS, Q_PER_KV, HEAD_DIM)
    s = jnp.einsum('bqhgd,bkhd->bhgqk', qg, k).astype(jnp.float32) * ATTN_SCALE
    p = jax.nn.softmax(s, axis=-1).astype(v.dtype)
    o = jnp.einsum('bhgqk,bkhd->bqhgd', p, v)
    return o.reshape(b, nq, N_Q_HEADS * HEAD_DIM)


def attend_query_blocks(q, k, v):
    b, n = q.shape[:2]
    nb = n // Q_BLOCK
    qb = jnp.swapaxes(q.reshape(b, nb, Q_BLOCK, N_Q_HEADS, HEAD_DIM), 0, 1)
    ob = lax.map(lambda qi: attend(qi, k, v), qb)
    return jnp.swapaxes(ob, 0, 1).reshape(b, n, N_Q_HEADS * HEAD_DIM)


def fourier_mixer(u, w_f):
    b, n, _ = u.shape
    uf = u.astype(jnp.float32).reshape(b, n, N_FOURIER_HEADS, FOURIER_HEAD_DIM)
    y = jnp.fft.fft2(uf, axes=(1, 3), norm='ortho').real
    return y.reshape(b, n, W_FOURIER).astype(u.dtype) @ w_f


def conv_module(a, dw_w, dw_b, ln_g, ln_b, w_pw):
    glu = a[..., :W_CONV] * jax.nn.sigmoid(a[..., W_CONV:])
    y = lax.conv_general_dilated(
        glu, dw_w[:, None, :], window_strides=(1,),
        padding=[(CONV_WIDTH // 2, CONV_WIDTH // 2)],
        dimension_numbers=('NWC', 'WIO', 'NWC'),
        feature_group_count=W_CONV) + dw_b
    y = jax.nn.silu(layer_norm(y, ln_g, ln_b))
    return y @ w_pw


def pool_mixer(u, w_pool, scale):
    b, n, _ = u.shape
    uf = u.astype(jnp.float32)
    csum = jnp.concatenate([jnp.zeros((b, 1, W_POOL), jnp.float32), jnp.cumsum(uf, axis=1)], axis=1)
    t = jnp.arange(n)
    groups = []
    for gi, win in enumerate(POOL_WINDOWS):
        sl = slice(gi * POOL_GROUP, (gi + 1) * POOL_GROUP)
        lo = jnp.clip(t - win // 2, 0, n)
        hi = jnp.clip(t - win // 2 + win, 0, n)
        cs = csum[..., sl]
        cnt = (hi - lo).astype(jnp.float32)[None, :, None]
        mean = (jnp.take(cs, hi, axis=1) - jnp.take(cs, lo, axis=1)) / cnt
        groups.append(mean - uf[..., sl])
    y = jnp.stack(groups, axis=2).astype(u.dtype)
    y = jnp.einsum('blgc,gcd->blgd', y, w_pool).reshape(b, n, W_POOL)
    return y * scale


def swiglu(x, w_in, w_out):
    a, g = jnp.split(x @ w_in, 2, axis=-1)
    return (jax.nn.silu(a) * g) @ w_out


def mixers_and_ffn(x, p, attn, gate1, shift2, scale2, gate2, g2, w_f, dw_w, dw_b,
                   ln_g, ln_b, w_pw, w_pl, pl_scale, w_o, w_fi, w_fo):
    four = fourier_mixer(p[..., OFF_F:OFF_C], w_f)
    conv = conv_module(p[..., OFF_C:OFF_P], dw_w, dw_b, ln_g, ln_b, w_pw)
    pool = pool_mixer(p[..., OFF_P:], w_pl, pl_scale)
    mix = jnp.concatenate([attn, four, conv, pool], axis=-1) @ w_o
    x = x + gate1 * mix
    xn = rms_norm(x, g2) * (1 + scale2) + shift2
    return x + gate2 * swiglu(xn, w_fi, w_fo)


def setup_inputs(seed: int = 0) -> dict:
    key = jax.random.key(seed)
    ks = jax.random.split(key, 24)
    f32 = jnp.float32
    L = DEPTH
    D = D_MODEL

    def nrm(k, shape, scale):
        return jax.random.normal(k, shape, f32) * scale

    return {
        'x': nrm(ks[0], (BATCH, SEQ, D), 1.0),
        'c': nrm(ks[1], (BATCH, D), 1.0),
        'ctx': nrm(ks[2], (BATCH, CTX_LEN, D), 1.0),
        'c_ctx': nrm(ks[3], (D,), 1.0),
        'w_mod': nrm(ks[4], (L, D, 6 * D), D ** -0.5),
        'b_mod': nrm(ks[5], (L, 6 * D), 0.02),
        'g_norm1': 1.0 + nrm(ks[6], (L, D), 0.02),
        'g_norm2': 1.0 + nrm(ks[7], (L, D), 0.02),
        'w_in': nrm(ks[8], (L, D, D_IN), D ** -0.5),
        'q_norm_g': 1.0 + nrm(ks[9], (L, HEAD_DIM), 0.02),
        'k_norm_g': 1.0 + nrm(ks[10], (L, HEAD_DIM), 0.02),
        'w_fourier': nrm(ks[11], (L, W_FOURIER, W_FOURIER), W_FOURIER ** -0.5),
        'conv_dw_w': nrm(ks[12], (L, CONV_WIDTH, W_CONV), CONV_WIDTH ** -0.5),
        'conv_dw_b': nrm(ks[13], (L, W_CONV), 0.02),
        'conv_ln_g': 1.0 + nrm(ks[14], (L, W_CONV), 0.02),
        'conv_ln_b': nrm(ks[15], (L, W_CONV), 0.02),
        'w_conv_pw': nrm(ks[16], (L, W_CONV, W_CONV), W_CONV ** -0.5),
        'w_pool': nrm(ks[17], (L, len(POOL_WINDOWS), POOL_GROUP, POOL_GROUP), POOL_GROUP ** -0.5),
        'pool_scale': 1.0 + nrm(ks[18], (L, W_POOL), 0.02),
        'w_out': nrm(ks[19], (L, D_MIX, D), D_MIX ** -0.5),
        'w_ffn_in': nrm(ks[20], (L, D, 2 * D_FF), D ** -0.5),
        'w_ffn_out': nrm(ks[21], (L, D_FF, D), D_FF ** -0.5),
    }


def reference(x, c, ctx, c_ctx, w_mod, b_mod, g_norm1, g_norm2, w_in, q_norm_g, k_norm_g,
              w_fourier, conv_dw_w, conv_dw_b, conv_ln_g, conv_ln_b, w_conv_pw, w_pool,
              pool_scale, w_out, w_ffn_in, w_ffn_out):
    n = x.shape[1]
    cos, sin = axial_rope_tables(n, x.dtype)
    h = ctx
    sc = jax.nn.silu(c)
    scc = jax.nn.silu(c_ctx)
    for l in range(DEPTH):
        last = l == DEPTH - 1
        mod = jnp.split(sc @ w_mod[l] + b_mod[l], 6, axis=-1)
        shift1, scale1, gate1, shift2, scale2, gate2 = [m[:, None, :] for m in mod]
        n_ctx_mod = 2 if last else 6
        mod_c = jnp.split(scc @ w_mod[l][:, :n_ctx_mod * D_MODEL] + b_mod[l][:n_ctx_mod * D_MODEL], n_ctx_mod)

        hn = rms_norm(h, g_norm1[l]) * (1 + mod_c[1]) + mod_c[0]
        if last:
            kc, vc = kv_heads(hn @ w_in[l][:, OFF_K:OFF_F], k_norm_g[l])
        else:
            pc = hn @ w_in[l]
            qc = q_heads(pc[..., OFF_Q:OFF_K], q_norm_g[l])
            kc, vc = kv_heads(pc[..., OFF_K:OFF_F], k_norm_g[l])

        xn = rms_norm(x, g_norm1[l]) * (1 + scale1) + shift1
        p = xn @ w_in[l]
        q = apply_rope(q_heads(p[..., OFF_Q:OFF_K], q_norm_g[l]), cos, sin)
        k, v = kv_heads(p[..., OFF_K:OFF_F], k_norm_g[l])
        k = apply_rope(k, cos, sin)
        attn = attend_query_blocks(q, jnp.concatenate([kc, k], axis=1), jnp.concatenate([vc, v], axis=1))
        x_new = mixers_and_ffn(x, p, attn, gate1, shift2, scale2, gate2, g_norm2[l],
                               w_fourier[l], conv_dw_w[l], conv_dw_b[l], conv_ln_g[l], conv_ln_b[l],
                               w_conv_pw[l], w_pool[l], pool_scale[l], w_out[l], w_ffn_in[l], w_ffn_out[l])
        if not last:
            attn_c = attend(qc, kc, vc)
            h = mixers_and_ffn(h, pc, attn_c, mod_c[2], mod_c[3], mod_c[4], mod_c[5], g_norm2[l],
                               w_fourier[l], conv_dw_w[l], conv_dw_b[l], conv_ln_g[l], conv_ln_b[l],
                               w_conv_pw[l], w_pool[l], pool_scale[l], w_out[l], w_ffn_in[l], w_ffn_out[l])
        x = x_new
    return x
```

```python
import functools

import numpy as np
import jax
import jax.numpy as jnp
from jax import lax
from jax.experimental import pallas as pl
from jax.experimental.pallas import tpu as pltpu

F32 = jnp.float32
BF16 = jnp.bfloat16

D_MODEL = 1024
DEPTH = 2
GRID_W = 64
HEAD_DIM = 64
N_Q_HEADS = 4
N_KV_HEADS = 2
W_ATTN = 256
W_FOURIER = 256
W_CONV = 256
W_POOL = 256
CONV_WIDTH = 31
POOL_GROUP = 64
D_FF = 2816
ROPE_THETA = 10000.0
EPS = 1e-6
ATTN_SCALE = HEAD_DIM ** -0.5
W_QK = W_ATTN + N_KV_HEADS * HEAD_DIM
OFF_V = W_QK
OFF_F = OFF_V + N_KV_HEADS * HEAD_DIM
OFF_C = OFF_F + W_FOURIER
OFF_P = OFF_C + 2 * W_CONV
D_IN = OFF_P + W_POOL

V7X_LANES = 128
V7X_SUBLANES = 8
V7X_BF16_ROWS = 16
V7X_VMEM_BYTES = 64 * 1024 * 1024

HALO = V7X_BF16_ROWS
DFT_N1 = 64
NEG_BIG = -1e30


def _dot(a, b):
    return jnp.dot(a, b, preferred_element_type=F32)


def _sigmoid(x):
    return 1.0 / (1.0 + jnp.exp(-x))


def _split_bf16(x):
    hi = x.astype(BF16)
    lo = (x - hi.astype(F32)).astype(BF16)
    return hi, lo


def _resident(shape):
    nd = len(shape)
    return pl.BlockSpec(shape, lambda *_: (0,) * nd, pipeline_mode=pl.Buffered(1))


def _params(n_grid, vmem_mb):
    return pltpu.CompilerParams(
        dimension_semantics=("parallel",) * n_grid,
        vmem_limit_bytes=vmem_mb * 1024 * 1024)


@functools.lru_cache(maxsize=None)
def _rope_tables(n):
    rows = n // GRID_W
    row = np.repeat(np.arange(rows, dtype=np.float64), GRID_W)
    col = np.tile(np.arange(GRID_W, dtype=np.float64), rows)
    n_freq = HEAD_DIM // 4
    inv_freq = ROPE_THETA ** (-np.arange(n_freq, dtype=np.float64) / n_freq)
    ang = np.concatenate([row[:, None] * inv_freq, col[:, None] * inv_freq], axis=-1)
    cos = np.repeat(np.cos(ang), 2, axis=1)
    sin = np.repeat(np.sin(ang), 2, axis=1)
    sign = np.tile(np.array([-1.0, 1.0]), HEAD_DIM // 2)
    cos2 = np.tile(cos, (1, 2)).astype(np.float32)
    sin2 = np.tile(sin * sign, (1, 2)).astype(np.float32)
    return cos2, sin2


@functools.lru_cache(maxsize=None)
def _head_sum_matrix():
    idx = np.arange(W_QK) // HEAD_DIM
    return np.asarray((idx[:, None] == idx[None, :]).astype(np.float32), dtype=BF16)


@functools.lru_cache(maxsize=None)
def _channel_dft():
    c = np.arange(W_FOURIER)
    same = (c[:, None] // HEAD_DIM) == (c[None, :] // HEAD_DIM)
    ang = 2.0 * np.pi * ((c[:, None] % HEAD_DIM) * (c[None, :] % HEAD_DIM)) / HEAD_DIM
    cr = np.where(same, np.cos(ang), 0.0) / 8.0
    ci = np.where(same, -np.sin(ang), 0.0) / 8.0
    return np.concatenate([cr, ci], axis=1).astype(np.float32)


@functools.lru_cache(maxsize=None)
def _seq_dft_tables(n):
    n1, n2 = DFT_N1, n // DFT_N1
    t1 = np.arange(n1)[:, None, None]
    k2 = np.arange(n2)[None, :, None]
    t2 = np.arange(n2)[None, None, :]
    theta = 2.0 * np.pi * ((k2 * (t1 + n1 * t2)) % n) / n
    dr = np.cos(theta) / np.sqrt(n2)
    di = -np.sin(theta) / np.sqrt(n2)
    tab_r = np.concatenate([dr, di], axis=1)
    tab_i = np.concatenate([-di, dr], axis=1)
    k1 = np.arange(n1)[:, None]
    tt = np.arange(n1)[None, :]
    phi = 2.0 * np.pi * ((k1 * tt) % n1) / n1
    tab_c = np.concatenate([np.cos(phi), np.sin(phi)], axis=1) / np.sqrt(n1)
    return tab_r.astype(np.float32), tab_i.astype(np.float32), tab_c.astype(np.float32)


@functools.lru_cache(maxsize=None)
def _ctx_dft_tables(n):
    k = np.arange(n)[:, None]
    t = np.arange(n)[None, :]
    ang = 2.0 * np.pi * ((k * t) % n) / n
    s = 1.0 / np.sqrt(n)
    return (np.cos(ang) * s).astype(np.float32), (np.sin(ang) * s).astype(np.float32)


def _mod_kernel(c_ref, w_ref, b_ref, o_ref):
    c = c_ref[...]
    sc = c * _sigmoid(c)
    a_hi, a_lo = _split_bf16(sc)
    w_hi, w_lo = _split_bf16(w_ref[0])
    o_ref[0] = _dot(a_hi, w_hi) + _dot(a_lo, w_hi) + _dot(a_hi, w_lo) + b_ref[0]


def _modulation(cc, w_mod, b_mod):
    depth, d, d6 = w_mod.shape
    tn = 1536
    return pl.pallas_call(
        _mod_kernel,
        grid=(depth, d6 // tn),
        in_specs=[
            pl.BlockSpec((V7X_SUBLANES, d), lambda l, j: (0, 0)),
            pl.BlockSpec((1, d, tn), lambda l, j: (l, 0, j)),
            pl.BlockSpec((1, 1, tn), lambda l, j: (l, 0, j)),
        ],
        out_specs=pl.BlockSpec((1, V7X_SUBLANES, tn), lambda l, j: (l, 0, j)),
        out_shape=jax.ShapeDtypeStruct((depth, V7X_SUBLANES, d6), F32),
        compiler_params=_params(2, 40),
        name="modulation",
    )(cc, w_mod, b_mod.reshape(depth, 1, d6))


def _rope(x, cos, sin):
    lane = lax.broadcasted_iota(jnp.int32, x.shape, 1)
    swapped = jnp.where((lane & 1) == 0,
                        pltpu.roll(x, V7X_LANES - 1, axis=1),
                        pltpu.roll(x, 1, axis=1))
    return x * cos + swapped * sin


def _inproj_kernel(rope, x_ref, modv_ref, g1_ref, w_ref, qkg_ref, bd_ref, cd_ref, *rest):
    if rope:
        cos_ref, sin_ref = rest[:2]
        rest = rest[2:]
    q_ref, k_ref, v_ref, gf_ref, a_ref, u_ref = rest
    x = x_ref[0]
    ms = jnp.mean(x * x, axis=-1, keepdims=True)
    gain = g1_ref[...] * (1.0 + modv_ref[0, 1:2, :])
    xn = (x * lax.rsqrt(ms + EPS)) * gain + modv_ref[0, 0:1, :]
    p = _dot(xn.astype(BF16), w_ref[...])

    qk = p[:, :W_QK]
    sq_hi, sq_lo = _split_bf16(qk * qk)
    ss = _dot(sq_hi, bd_ref[...]) + _dot(sq_lo, bd_ref[...])
    qkn = qk * lax.rsqrt(ss * (1.0 / HEAD_DIM) + EPS) * qkg_ref[...]
    slabs = []
    for j in range(W_QK // V7X_LANES):
        slab = qkn[:, j * V7X_LANES:(j + 1) * V7X_LANES]
        if rope:
            slab = _rope(slab, cos_ref[...], sin_ref[...])
        slabs.append(slab)
    q_ref[0] = (jnp.concatenate(slabs[:2], axis=1) * ATTN_SCALE).astype(BF16)
    k_ref[0] = slabs[2].astype(BF16)
    v_ref[0] = p[:, OFF_V:OFF_F].astype(BF16)
    gf_ref[0] = _dot(p[:, OFF_F:OFF_C].astype(BF16), cd_ref[...]).astype(BF16)
    a_ref[0] = p[:, OFF_C:OFF_P].astype(BF16)
    u_ref[0] = p[:, OFF_P:].astype(BF16)


def _inproj(x, modv, g1, w_in, qk_gain, rope, tm):
    bsz, n, d = x.shape
    nt = n // tm
    tok = lambda width: pl.BlockSpec((1, tm, width), lambda b, i: (b, i, 0))
    in_specs = [
        tok(d),
        pl.BlockSpec((1, 2, d), lambda b, i: (b, 0, 0)),
        _resident((1, d)),
        _resident((d, D_IN)),
        _resident((1, W_QK)),
        _resident((W_QK, W_QK)),
        _resident((W_FOURIER, 2 * W_FOURIER)),
    ]
    args = [x, modv, g1, w_in, qk_gain, _head_sum_matrix(),
            jnp.asarray(_channel_dft()).astype(BF16)]
    if rope:
        cos2, sin2 = _rope_tables(n)
        in_specs += [pl.BlockSpec((tm, V7X_LANES), lambda b, i: (i, 0))] * 2
        args += [cos2, sin2]
    widths = (W_ATTN, N_KV_HEADS * HEAD_DIM, N_KV_HEADS * HEAD_DIM,
              2 * W_FOURIER, 2 * W_CONV, W_POOL)
    return pl.pallas_call(
        functools.partial(_inproj_kernel, rope),
        grid=(bsz, nt),
        in_specs=in_specs,
        out_specs=[tok(w) for w in widths],
        out_shape=[jax.ShapeDtypeStruct((bsz, n, w), BF16) for w in widths],
        compiler_params=_params(2, 40),
        name="inproj_rope" if rope else "inproj_ctx",
    )(*args)


def _attn_kernel(n_chunks, tk, q_ref, k_ref, v_ref, o_ref):
    tq = q_ref.shape[-1]
    q2 = jnp.concatenate([q_ref[0, 0], q_ref[0, 1]], axis=1)

    def body(i, carry):
        m, l, acc = carry
        off = pl.multiple_of(i * tk, tk)
        kc = k_ref[0, 0, pl.ds(off, tk), :]
        vc = v_ref[0, 0, :, pl.ds(off, tk)]
        s = _dot(kc, q2)
        m_new = jnp.maximum(m, jnp.max(s, axis=0, keepdims=True))
        alpha = jnp.exp(m - m_new)
        p = jnp.exp(s - m_new)
        l = alpha * l + jnp.sum(p, axis=0, keepdims=True)
        acc = alpha * acc + _dot(vc, p.astype(BF16))
        return m_new, l, acc

    init = (jnp.full((1, 2 * tq), NEG_BIG, F32), jnp.zeros((1, 2 * tq), F32),
            jnp.zeros((HEAD_DIM, 2 * tq), F32))
    _, l, acc = lax.fori_loop(0, n_chunks, body, init)
    o = acc * (1.0 / l)
    o_ref[0, 0] = o[:, :tq].astype(BF16)
    o_ref[0, 1] = o[:, tq:].astype(BF16)


def _attention(q_t, k, v_t, tq, tk):
    bsz, _, _, nq = q_t.shape
    nk = k.shape[2]
    q_per_kv = N_Q_HEADS // N_KV_HEADS
    return pl.pallas_call(
        functools.partial(_attn_kernel, nk // tk, tk),
        grid=(bsz, N_KV_HEADS, nq // tq),
        in_specs=[
            pl.BlockSpec((1, q_per_kv, HEAD_DIM, tq), lambda b, g, i: (b, g, 0, i)),
            pl.BlockSpec((1, 1, nk, HEAD_DIM), lambda b, g, i: (b, g, 0, 0)),
            pl.BlockSpec((1, 1, HEAD_DIM, nk), lambda b, g, i: (b, g, 0, 0)),
        ],
        out_specs=pl.BlockSpec((1, q_per_kv, HEAD_DIM, tq), lambda b, g, i: (b, g, 0, i)),
        out_shape=jax.ShapeDtypeStruct((bsz, N_Q_HEADS, HEAD_DIM, nq), BF16),
        compiler_params=_params(3, 40),
        name="attention_nk%d" % nk,
    )(q_t, k, v_t)


def _dft_a_kernel(bsz, g_ref, tr_ref, ti_ref, o_ref):
    n2 = g_ref.shape[2]
    tab_r = tr_ref[0].astype(BF16)
    tab_i = ti_ref[0].astype(BF16)
    for b in range(bsz):
        res = (_dot(tab_r, g_ref[0, b, :, :W_FOURIER])
               + _dot(tab_i, g_ref[0, b, :, W_FOURIER:]))
        o_ref[0, 0, :, b * W_FOURIER:(b + 1) * W_FOURIER] = res[:n2].astype(BF16)
        o_ref[1, 0, :, b * W_FOURIER:(b + 1) * W_FOURIER] = res[n2:].astype(BF16)


def _dft_c_kernel(a_ref, tc_ref, o_ref):
    o_ref[...] = _dot(tc_ref[...], a_ref[...]).astype(BF16)


def _fourier_latent(gf):
    bsz, n, _ = gf.shape
    n1, n2 = DFT_N1, n // DFT_N1
    tab_r, tab_i, tab_c = _seq_dft_tables(n)
    g_t = jnp.transpose(gf.reshape(bsz, n2, n1, 2 * W_FOURIER), (2, 0, 1, 3))
    lanes = bsz * W_FOURIER
    a = pl.pallas_call(
        functools.partial(_dft_a_kernel, bsz),
        grid=(n1,),
        in_specs=[
            pl.BlockSpec((1, bsz, n2, 2 * W_FOURIER), lambda t: (t, 0, 0, 0)),
            pl.BlockSpec((1, 2 * n2, n2), lambda t: (t, 0, 0)),
            pl.BlockSpec((1, 2 * n2, n2), lambda t: (t, 0, 0)),
        ],
        out_specs=pl.BlockSpec((2, 1, n2, lanes), lambda t: (0, t, 0, 0)),
        out_shape=jax.ShapeDtypeStruct((2, n1, n2, lanes), BF16),
        compiler_params=_params(1, 40),
        name="dft_stage_a",
    )(g_t, tab_r, tab_i)
    a2 = a.reshape(2 * n1, n2 * lanes)
    tl = 4096
    y = pl.pallas_call(
        _dft_c_kernel,
        grid=(n2 * lanes // tl,),
        in_specs=[pl.BlockSpec((2 * n1, tl), lambda j: (0, j)), _resident((n1, 2 * n1))],
        out_specs=pl.BlockSpec((n1, tl), lambda j: (0, j)),
        out_shape=jax.ShapeDtypeStruct((n1, n2 * lanes), BF16),
        compiler_params=_params(1, 40),
        name="dft_stage_c",
    )(a2, jnp.asarray(tab_c).astype(BF16))
    return y.reshape(n, lanes)


def _dft_ctx_kernel(g_ref, c_ref, s_ref, o_ref):
    o_ref[...] = (_dot(c_ref[...], g_ref[0, :, :W_FOURIER])
                  + _dot(s_ref[...], g_ref[0, :, W_FOURIER:])).astype(BF16)


def _fourier_ctx(gf):
    bsz, n, _ = gf.shape
    tab_cos, tab_sin = _ctx_dft_tables(n)
    return pl.pallas_call(
        _dft_ctx_kernel,
        grid=(bsz,),
        in_specs=[pl.BlockSpec((1, n, 2 * W_FOURIER), lambda b: (b, 0, 0)),
                  _resident((n, n)), _resident((n, n))],
        out_specs=pl.BlockSpec((n, W_FOURIER), lambda b: (0, b)),
        out_shape=jax.ShapeDtypeStruct((n, bsz * W_FOURIER), BF16),
        compiler_params=_params(1, 40),
        name="dft_ctx",
    )(gf, jnp.asarray(tab_cos).astype(BF16), jnp.asarray(tab_sin).astype(BF16))


CONV_ROWS = 64


def _mix_kernel(n_tiles, n_seq, x_ref, attn_ref, yf_ref, am_ref, ap_ref, an_ref,
                um_ref, up_ref, un_ref, gate_ref, wf_ref, dww_ref, cvec_ref, wpw_ref,
                wpl_ref, wo_ref, o_ref, g_ref, gsh_ref, pu_ref, psh_ref, cv_ref, pl_ref):
    tm = x_ref.shape[1]
    i = pl.program_id(1)
    first = i == 0
    last = i == n_tiles - 1

    def glu(a_blk):
        a = a_blk.astype(F32)
        return a[:, :W_CONV] * _sigmoid(a[:, W_CONV:])

    g_ref[0:HALO] = jnp.where(first, 0.0, glu(ap_ref[0]))
    g_ref[HALO:HALO + tm] = glu(am_ref[0])
    g_ref[HALO + tm:] = jnp.where(last, 0.0, glu(an_ref[0]))
    pu_ref[0:HALO] = jnp.where(first, 0.0, up_ref[0].astype(F32))
    pu_ref[HALO:HALO + tm] = um_ref[0].astype(F32)
    pu_ref[HALO + tm:] = jnp.where(last, 0.0, un_ref[0].astype(F32))
    rows_sh = gsh_ref.shape[1]
    for b in range(1, V7X_SUBLANES):
        gsh_ref[b - 1] = g_ref[pl.ds(b, rows_sh), :]
        psh_ref[b - 1] = pu_ref[pl.ds(b, rows_sh), :]

    lane = lax.broadcasted_iota(jnp.int32, (CONV_ROWS, W_POOL), 1)
    grp = lane // POOL_GROUP
    half = jnp.left_shift(1, grp)
    row = lax.broadcasted_iota(jnp.int32, (CONV_ROWS, W_POOL), 0)

    def shifted(base_ref, sh_ref, c0, j):
        src = base_ref if j % V7X_SUBLANES == 0 else sh_ref.at[j % V7X_SUBLANES - 1]
        return src[pl.ds(c0 + (j // V7X_SUBLANES) * V7X_SUBLANES, CONV_ROWS), :]

    def chunk(c, _):
        c0 = pl.multiple_of(c * CONV_ROWS, CONV_ROWS)
        acc = jnp.broadcast_to(cvec_ref[0:1, :], (CONV_ROWS, W_CONV))
        for k in range(CONV_WIDTH):
            acc = acc + dww_ref[k:k + 1, :] * shifted(g_ref, gsh_ref, c0, k + 1)
        mu = jnp.mean(acc, axis=-1, keepdims=True)
        yc = acc - mu
        var = jnp.mean(yc * yc, axis=-1, keepdims=True)
        yn = yc * lax.rsqrt(var + EPS) * cvec_ref[1:2, :] + cvec_ref[2:3, :]
        cv_ref[pl.ds(c0, CONV_ROWS), :] = (yn * _sigmoid(yn)).astype(BF16)

        ld = lambda d: shifted(pu_ref, psh_ref, c0, HALO + d)
        u0 = ld(0)
        s2 = ld(-1) + u0
        s4 = s2 + ld(-2) + ld(1)
        s8 = s4 + ld(-4) + ld(-3) + ld(2) + ld(3)
        s16 = s8 + ld(-8) + ld(-7) + ld(-6) + ld(-5) + ld(4) + ld(5) + ld(6) + ld(7)
        win = jnp.where(grp == 0, s2, jnp.where(grp == 1, s4, jnp.where(grp == 2, s8, s16)))
        t = i * tm + c0 + row
        cnt = jnp.minimum(t + half, n_seq) - jnp.maximum(t - half, 0)
        pl_ref[pl.ds(c0, CONV_ROWS), :] = (win / cnt.astype(F32) - u0).astype(BF16)
        return 0

    lax.fori_loop(0, tm // CONV_ROWS, chunk, 0)

    four = _dot(yf_ref[...], wf_ref[...]).astype(BF16)
    conv = _dot(cv_ref[...], wpw_ref[...]).astype(BF16)
    pool = (_dot(pl_ref[...], wpl_ref[...]) * cvec_ref[3:4, :]).astype(BF16)
    mix = (_dot(attn_ref[0], wo_ref[0:W_ATTN, :])
           + _dot(four, wo_ref[W_ATTN:2 * W_ATTN, :])
           + _dot(conv, wo_ref[2 * W_ATTN:3 * W_ATTN, :])
           + _dot(pool, wo_ref[3 * W_ATTN:, :]))
    o_ref[0] = x_ref[0] + gate_ref[0] * mix


def _mixers(x, attn, yf, a_conv, u_pool, gate1, w_f, dw_w, cvec, w_pw, w_pool_bd, w_o, tm):
    bsz, n, d = x.shape
    nt = n // tm
    hb = tm // HALO
    tok = lambda width: pl.BlockSpec((1, tm, width), lambda b, i: (b, i, 0))
    prev = lambda width: pl.BlockSpec(
        (1, HALO, width), lambda b, i: (b, jnp.maximum(i * hb - 1, 0), 0))
    nxt = lambda width: pl.BlockSpec(
        (1, HALO, width), lambda b, i: (b, jnp.minimum((i + 1) * hb, n // HALO - 1), 0))
    rows_sh = tm + 2 * HALO - V7X_SUBLANES
    return pl.pallas_call(
        functools.partial(_mix_kernel, nt, n),
        grid=(bsz, nt),
        in_specs=[
            tok(d), tok(W_ATTN),
            pl.BlockSpec((tm, W_FOURIER), lambda b, i: (i, b)),
            tok(2 * W_CONV), prev(2 * W_CONV), nxt(2 * W_CONV),
            tok(W_POOL), prev(W_POOL), nxt(W_POOL),
            pl.BlockSpec((1, 1, d), lambda b, i: (b, 0, 0)),
            _resident((W_FOURIER, W_FOURIER)),
            _resident((CONV_WIDTH, W_CONV)),
            _resident((V7X_SUBLANES, W_CONV)),
            _resident((W_CONV, W_CONV)),
            _resident((W_POOL, W_POOL)),
            _resident((4 * W_ATTN, d)),
        ],
        out_specs=tok(d),
        out_shape=jax.ShapeDtypeStruct((bsz, n, d), F32),
        scratch_shapes=[
            pltpu.VMEM((tm + 2 * HALO, W_CONV), F32),
            pltpu.VMEM((V7X_SUBLANES - 1, rows_sh, W_CONV), F32),
            pltpu.VMEM((tm + 2 * HALO, W_POOL), F32),
            pltpu.VMEM((V7X_SUBLANES - 1, rows_sh, W_POOL), F32),
            pltpu.VMEM((tm, W_CONV), BF16),
            pltpu.VMEM((tm, W_POOL), BF16),
        ],
        compiler_params=_params(2, 48),
        name="mixers_n%d" % n,
    )(x, attn, yf, a_conv, a_conv, a_conv, u_pool, u_pool, u_pool, gate1,
      w_f, dw_w, cvec, w_pw, w_pool_bd, w_o)


FFN_CHUNK = D_FF // 2


def _ffn_kernel(x_ref, modv_ref, g2_ref, wfi_ref, wfo_ref, o_ref):
    x = x_ref[0]
    ms = jnp.mean(x * x, axis=-1, keepdims=True)
    gain = g2_ref[...] * (1.0 + modv_ref[0, 1:2, :])
    xn = ((x * lax.rsqrt(ms + EPS)) * gain + modv_ref[0, 0:1, :]).astype(BF16)
    acc = None
    for c in range(D_FF // FFN_CHUNK):
        lo = c * FFN_CHUNK
        a = _dot(xn, wfi_ref[:, lo:lo + FFN_CHUNK])
        g = _dot(xn, wfi_ref[:, D_FF + lo:D_FF + lo + FFN_CHUNK])
        h = ((a * _sigmoid(a)) * g).astype(BF16)
        part = _dot(h, wfo_ref[lo:lo + FFN_CHUNK, :])
        acc = part if acc is None else acc + part
    o_ref[0] = x + modv_ref[0, 2:3, :] * acc


def _ffn(x, modv, g2, w_fi, w_fo, tm):
    bsz, n, d = x.shape
    tok = pl.BlockSpec((1, tm, d), lambda b, i: (b, i, 0))
    return pl.pallas_call(
        _ffn_kernel,
        grid=(bsz, n // tm),
        in_specs=[tok, pl.BlockSpec((1, 3, d), lambda b, i: (b, 0, 0)), _resident((1, d)),
                  _resident((d, 2 * D_FF)), _resident((D_FF, d))],
        out_specs=tok,
        out_shape=jax.ShapeDtypeStruct((bsz, n, d), F32),
        compiler_params=_params(2, 52),
        name="ffn_n%d" % n,
    )(x, modv, g2, w_fi, w_fo)


def _heads_t(a, n_heads):
    bsz, n, _ = a.shape
    return jnp.transpose(a.reshape(bsz, n, n_heads, HEAD_DIM), (0, 2, 3, 1))


def _heads_rows(a, n_heads):
    bsz, n, _ = a.shape
    return jnp.transpose(a.reshape(bsz, n, n_heads, HEAD_DIM), (0, 2, 1, 3))


def _from_heads_t(o_t):
    bsz, h, dh, n = o_t.shape
    return jnp.transpose(o_t, (0, 3, 1, 2)).reshape(bsz, n, h * dh)


def _pool_block_diag(w_pool_l):
    groups = w_pool_l.shape[0]
    out = jnp.zeros((W_POOL, W_POOL), w_pool_l.dtype)
    for g in range(groups):
        sl = slice(g * POOL_GROUP, (g + 1) * POOL_GROUP)
        out = out.at[sl, sl].set(w_pool_l[g])
    return out


def kernel(x, c, ctx, c_ctx, w_mod, b_mod, g_norm1, g_norm2, w_in, q_norm_g, k_norm_g,
           w_fourier, conv_dw_w, conv_dw_b, conv_ln_g, conv_ln_b, w_conv_pw, w_pool,
           pool_scale, w_out, w_ffn_in, w_ffn_out):
    bsz, n, d = x.shape
    n_ctx = ctx.shape[1]
    tm_lat, tm_ctx = 512, n_ctx
    tq_lat, tk_lat = 256, 768

    cc = jnp.zeros((V7X_SUBLANES, d), F32).at[:bsz].set(c).at[bsz].set(c_ctx)
    mod_all = _modulation(cc, w_mod, b_mod)

    h = ctx
    for l in range(DEPTH):
        last = l == DEPTH - 1
        m6 = mod_all[l].reshape(V7X_SUBLANES, 6, d)
        lat = m6[:bsz]
        cx = jnp.broadcast_to(m6[bsz:bsz + 1], (bsz, 6, d))
        w_in_l = w_in[l].astype(BF16)
        qk_gain = jnp.concatenate([jnp.tile(q_norm_g[l], N_Q_HEADS),
                                   jnp.tile(k_norm_g[l], N_KV_HEADS)])[None, :]
        g1 = g_norm1[l][None, :]
        g2 = g_norm2[l][None, :]
        w_f = w_fourier[l].astype(BF16)
        w_pw = w_conv_pw[l].astype(BF16)
        w_pl = _pool_block_diag(w_pool[l]).astype(BF16)
        w_o = w_out[l].astype(BF16)
        w_fi = w_ffn_in[l].astype(BF16)
        w_fo = w_ffn_out[l].astype(BF16)
        cvec = jnp.zeros((V7X_SUBLANES, W_CONV), F32)
        cvec = cvec.at[0].set(conv_dw_b[l]).at[1].set(conv_ln_g[l])
        cvec = cvec.at[2].set(conv_ln_b[l]).at[3].set(pool_scale[l])

        qc, kc, vc, gfc, ac, uc = _inproj(h, cx[:, 0:2], g1, w_in_l, qk_gain, False, tm_ctx)
        q, k, v, gf, a, u = _inproj(x, lat[:, 0:2], g1, w_in_l, qk_gain, True, tm_lat)

        kc_r, vc_t = _heads_rows(kc, N_KV_HEADS), _heads_t(vc, N_KV_HEADS)
        k_all = jnp.concatenate([kc_r, _heads_rows(k, N_KV_HEADS)], axis=2)
        v_all = jnp.concatenate([vc_t, _heads_t(v, N_KV_HEADS)], axis=3)
        attn = _from_heads_t(_attention(_heads_t(q, N_Q_HEADS), k_all, v_all, tq_lat, tk_lat))
        yf = _fourier_latent(gf)
        x1 = _mixers(x, attn, yf, a, u, lat[:, 2:3], w_f, conv_dw_w[l], cvec, w_pw, w_pl,
                     w_o, tm_lat)
        x_new = _ffn(x1, lat[:, 3:6], g2, w_fi, w_fo, tm_lat)

        if not last:
            attn_c = _from_heads_t(_attention(_heads_t(qc, N_Q_HEADS), kc_r, vc_t, n_ctx, n_ctx))
            yfc = _fourier_ctx(gfc)
            h1 = _mixers(h, attn_c, yfc, ac, uc, cx[:, 2:3], w_f, conv_dw_w[l], cvec, w_pw,
                         w_pl, w_o, tm_ctx)
            h = _ffn(h1, cx[:, 3:6], g2, w_fi, w_fo, tm_ctx)
        x = x_new
    return x
```

```python
import functools

import numpy as np
import jax
import jax.numpy as jnp
from jax import lax
from jax.experimental import pallas as pl
from jax.experimental.pallas import tpu as pltpu

F32 = jnp.float32
BF16 = jnp.bfloat16

D_MODEL = 1024
DEPTH = 2
GRID_W = 64
HEAD_DIM = 64
N_Q_HEADS = 4
N_KV_HEADS = 2
W_ATTN = 256
W_FOURIER = 256
W_CONV = 256
W_POOL = 256
CONV_WIDTH = 31
POOL_GROUP = 64
D_FF = 2816
ROPE_THETA = 10000.0
EPS = 1e-6
ATTN_SCALE = HEAD_DIM ** -0.5
Q_SCALE_LOG2 = ATTN_SCALE * float(np.log2(np.e))
W_QK = W_ATTN + N_KV_HEADS * HEAD_DIM
OFF_V = W_QK
OFF_F = OFF_V + N_KV_HEADS * HEAD_DIM
OFF_C = OFF_F + W_FOURIER
OFF_P = OFF_C + 2 * W_CONV
D_IN = OFF_P + W_POOL

V7X_LANES = 128
V7X_SUBLANES = 8
V7X_BF16_ROWS = 16
V7X_VMEM_BYTES = 64 * 1024 * 1024

HALO = V7X_BF16_ROWS
DFT_N1 = 64
NEG_BIG = -1e30


def _dot(a, b):
    return jnp.dot(a, b, preferred_element_type=F32)


def _sigmoid(x):
    return 1.0 / (1.0 + jnp.exp(-x))


def _split_bf16(x):
    hi = x.astype(BF16)
    lo = (x - hi.astype(F32)).astype(BF16)
    return hi, lo


def _resident(shape):
    nd = len(shape)
    return pl.BlockSpec(shape, lambda *_: (0,) * nd, pipeline_mode=pl.Buffered(1))


def _params(n_grid, vmem_mb):
    return pltpu.CompilerParams(
        dimension_semantics=("parallel",) * n_grid,
        vmem_limit_bytes=vmem_mb * 1024 * 1024)


@functools.lru_cache(maxsize=None)
def _rope_tables(n):
    rows = n // GRID_W
    row = np.repeat(np.arange(rows, dtype=np.float64), GRID_W)
    col = np.tile(np.arange(GRID_W, dtype=np.float64), rows)
    n_freq = HEAD_DIM // 4
    inv_freq = ROPE_THETA ** (-np.arange(n_freq, dtype=np.float64) / n_freq)
    ang = np.concatenate([row[:, None] * inv_freq, col[:, None] * inv_freq], axis=-1)
    cos = np.repeat(np.cos(ang), 2, axis=1)
    sin = np.repeat(np.sin(ang), 2, axis=1)
    sign = np.tile(np.array([-1.0, 1.0]), HEAD_DIM // 2)
    cos2 = np.tile(cos, (1, 2)).astype(np.float32)
    sin2 = np.tile(sin * sign, (1, 2)).astype(np.float32)
    return cos2, sin2


@functools.lru_cache(maxsize=None)
def _head_sum_matrix():
    idx = np.arange(W_QK) // HEAD_DIM
    return np.asarray((idx[:, None] == idx[None, :]).astype(np.float32), dtype=BF16)


@functools.lru_cache(maxsize=None)
def _channel_dft():
    c = np.arange(W_FOURIER)
    same = (c[:, None] // HEAD_DIM) == (c[None, :] // HEAD_DIM)
    ang = 2.0 * np.pi * ((c[:, None] % HEAD_DIM) * (c[None, :] % HEAD_DIM)) / HEAD_DIM
    cr = np.where(same, np.cos(ang), 0.0) / 8.0
    ci = np.where(same, -np.sin(ang), 0.0) / 8.0
    return np.concatenate([cr, ci], axis=1).astype(np.float32)


@functools.lru_cache(maxsize=None)
def _seq_dft_tables(n):
    n1, n2 = DFT_N1, n // DFT_N1
    t1 = np.arange(n1)[:, None, None]
    k2 = np.arange(n2)[None, :, None]
    t2 = np.arange(n2)[None, None, :]
    theta = 2.0 * np.pi * ((k2 * (t1 + n1 * t2)) % n) / n
    dr = np.cos(theta) / np.sqrt(n2)
    di = -np.sin(theta) / np.sqrt(n2)
    tab_r = np.concatenate([dr, di], axis=1)
    tab_i = np.concatenate([-di, dr], axis=1)
    k1 = np.arange(n1)[:, None]
    tt = np.arange(n1)[None, :]
    phi = 2.0 * np.pi * ((k1 * tt) % n1) / n1
    tab_c = np.concatenate([np.cos(phi), np.sin(phi)], axis=1) / np.sqrt(n1)
    return tab_r.astype(np.float32), tab_i.astype(np.float32), tab_c.astype(np.float32)


@functools.lru_cache(maxsize=None)
def _ctx_dft_tables(n):
    k = np.arange(n)[:, None]
    t = np.arange(n)[None, :]
    ang = 2.0 * np.pi * ((k * t) % n) / n
    s = 1.0 / np.sqrt(n)
    return (np.cos(ang) * s).astype(np.float32), (np.sin(ang) * s).astype(np.float32)


def _mod_kernel(c_ref, w_ref, b_ref, o_ref):
    c = c_ref[...]
    sc = c * _sigmoid(c)
    a_hi, a_lo = _split_bf16(sc)
    w_hi, w_lo = _split_bf16(w_ref[0])
    o_ref[0] = _dot(a_hi, w_hi) + _dot(a_lo, w_hi) + _dot(a_hi, w_lo) + b_ref[0]


def _modulation(cc, w_mod, b_mod):
    depth, d, d6 = w_mod.shape
    tn = 1536
    return pl.pallas_call(
        _mod_kernel,
        grid=(depth, d6 // tn),
        in_specs=[
            pl.BlockSpec((V7X_SUBLANES, d), lambda l, j: (0, 0)),
            pl.BlockSpec((1, d, tn), lambda l, j: (l, 0, j)),
            pl.BlockSpec((1, 1, tn), lambda l, j: (l, 0, j)),
        ],
        out_specs=pl.BlockSpec((1, V7X_SUBLANES, tn), lambda l, j: (l, 0, j)),
        out_shape=jax.ShapeDtypeStruct((depth, V7X_SUBLANES, d6), F32),
        compiler_params=_params(2, 40),
        name="modulation",
    )(cc, w_mod, b_mod.reshape(depth, 1, d6))


def _rope(x, cos, sin):
    lane = lax.broadcasted_iota(jnp.int32, x.shape, 1)
    swapped = jnp.where((lane & 1) == 0,
                        pltpu.roll(x, V7X_LANES - 1, axis=1),
                        pltpu.roll(x, 1, axis=1))
    return x * cos + swapped * sin


def _inproj_kernel(rope, x_ref, modv_ref, g1_ref, w_ref, qkg_ref, bd_ref, cd_ref, *rest):
    if rope:
        cos_ref, sin_ref = rest[:2]
        rest = rest[2:]
    q_ref, k_ref, v_ref, gf_ref, a_ref, u_ref = rest
    x = x_ref[0]
    ms = jnp.mean(x * x, axis=-1, keepdims=True)
    gain = g1_ref[...] * (1.0 + modv_ref[0, 1:2, :])
    xn = (x * lax.rsqrt(ms + EPS)) * gain + modv_ref[0, 0:1, :]
    p = _dot(xn.astype(BF16), w_ref[...])

    qk = p[:, :W_QK]
    sq_hi, sq_lo = _split_bf16(qk * qk)
    ss = _dot(sq_hi, bd_ref[...]) + _dot(sq_lo, bd_ref[...])
    qkn = qk * lax.rsqrt(ss * (1.0 / HEAD_DIM) + EPS) * qkg_ref[...]
    slabs = []
    for j in range(W_QK // V7X_LANES):
        slab = qkn[:, j * V7X_LANES:(j + 1) * V7X_LANES]
        if rope:
            slab = _rope(slab, cos_ref[...], sin_ref[...])
        slabs.append(slab)
    q_ref[0] = (jnp.concatenate(slabs[:2], axis=1) * Q_SCALE_LOG2).astype(BF16)
    k_ref[0] = slabs[2].astype(BF16)
    v_ref[0] = p[:, OFF_V:OFF_F].astype(BF16)
    gf_ref[0] = _dot(p[:, OFF_F:OFF_C].astype(BF16), cd_ref[...]).astype(BF16)
    a_ref[0] = p[:, OFF_C:OFF_P].astype(BF16)
    u_ref[0] = p[:, OFF_P:].astype(BF16)


def _inproj(x, modv, g1, w_in, qk_gain, rope, tm):
    bsz, n, d = x.shape
    nt = n // tm
    tok = lambda width: pl.BlockSpec((1, tm, width), lambda b, i: (b, i, 0))
    in_specs = [
        tok(d),
        pl.BlockSpec((1, 2, d), lambda b, i: (b, 0, 0)),
        _resident((1, d)),
        _resident((d, D_IN)),
        _resident((1, W_QK)),
        _resident((W_QK, W_QK)),
        _resident((W_FOURIER, 2 * W_FOURIER)),
    ]
    args = [x, modv, g1, w_in, qk_gain, _head_sum_matrix(),
            jnp.asarray(_channel_dft()).astype(BF16)]
    if rope:
        cos2, sin2 = _rope_tables(n)
        in_specs += [pl.BlockSpec((tm, V7X_LANES), lambda b, i: (i, 0))] * 2
        args += [cos2, sin2]
    widths = (W_ATTN, N_KV_HEADS * HEAD_DIM, N_KV_HEADS * HEAD_DIM,
              2 * W_FOURIER, 2 * W_CONV, W_POOL)
    return pl.pallas_call(
        functools.partial(_inproj_kernel, rope),
        grid=(bsz, nt),
        in_specs=in_specs,
        out_specs=[tok(w) for w in widths],
        out_shape=[jax.ShapeDtypeStruct((bsz, n, w), BF16) for w in widths],
        compiler_params=_params(2, 40),
        name="inproj_rope" if rope else "inproj_ctx",
    )(*args)


K_AUG = V7X_LANES
V_AUG = HEAD_DIM + V7X_BF16_ROWS
SAFE_SHIFT_LOG2 = 50.0


def _attn_kernel(n_chunks, tk, q_ref, k_ref, v_ref, o_ref, kmax_ref, acc_ref):
    tq = q_ref.shape[-1]
    i = pl.program_id(2)

    @pl.when(i == 0)
    def _():
        ones = jnp.ones((K_AUG, K_AUG), BF16)
        mx = jnp.zeros((1, K_AUG), F32)
        for c in range(n_chunks):
            kf = k_ref[0, 0, c * tk:(c + 1) * tk, :].astype(F32)
            norms = _dot((kf * kf).astype(BF16), ones)
            mx = jnp.maximum(mx, jnp.max(norms, axis=0, keepdims=True))
        kmax_ref[...] = jnp.sqrt(mx)

    q2 = jnp.concatenate([q_ref[0, 0], q_ref[0, 1]], axis=1)
    qf = q2.astype(F32)
    kmax = jnp.tile(kmax_ref[...], (1, 2 * tq // K_AUG))
    bound = jnp.sqrt(jnp.sum(qf * qf, axis=0, keepdims=True)) * kmax
    safe = jnp.max(bound) <= SAFE_SHIFT_LOG2
    row = lax.broadcasted_iota(jnp.int32, (K_AUG - HEAD_DIM, 2 * tq), 0)
    shift_rows = jnp.where(row == 0, -bound, 0.0).astype(BF16)
    q_aug = jnp.concatenate([q2, shift_rows], axis=0)

    def logits(c):
        return _dot(k_ref[0, 0, c * tk:(c + 1) * tk, :], q_aug)

    @pl.when(safe)
    def _():
        acc = None
        for c in range(n_chunks):
            p = jnp.exp2(logits(c)).astype(BF16)
            part = _dot(v_ref[0, 0, :, c * tk:(c + 1) * tk], p)
            acc = part if acc is None else acc + part
        acc_ref[...] = acc

    @pl.when(jnp.logical_not(safe))
    def _():
        m = jnp.full((1, 2 * tq), NEG_BIG, F32)
        acc = jnp.zeros((V_AUG, 2 * tq), F32)
        for c in range(n_chunks):
            s = logits(c)
            m_new = jnp.maximum(m, jnp.max(s, axis=0, keepdims=True))
            p = jnp.exp2(s - m_new).astype(BF16)
            acc = jnp.exp2(m - m_new) * acc + _dot(v_ref[0, 0, :, c * tk:(c + 1) * tk], p)
            m = m_new
        acc_ref[...] = acc

    o = acc_ref[0:HEAD_DIM, :] * (1.0 / acc_ref[HEAD_DIM:HEAD_DIM + 1, :])
    o_ref[0, 0] = o[:, :tq].astype(BF16)
    o_ref[0, 1] = o[:, tq:].astype(BF16)


def _attention(q_t, k_aug, v_aug, tq, tk):
    bsz, _, _, nq = q_t.shape
    nk = k_aug.shape[2]
    q_per_kv = N_Q_HEADS // N_KV_HEADS
    return pl.pallas_call(
        functools.partial(_attn_kernel, nk // tk, tk),
        grid=(bsz, N_KV_HEADS, nq // tq),
        in_specs=[
            pl.BlockSpec((1, q_per_kv, HEAD_DIM, tq), lambda b, g, i: (b, g, 0, i)),
            pl.BlockSpec((1, 1, nk, K_AUG), lambda b, g, i: (b, g, 0, 0)),
            pl.BlockSpec((1, 1, V_AUG, nk), lambda b, g, i: (b, g, 0, 0)),
        ],
        out_specs=pl.BlockSpec((1, q_per_kv, HEAD_DIM, tq), lambda b, g, i: (b, g, 0, i)),
        out_shape=jax.ShapeDtypeStruct((bsz, N_Q_HEADS, HEAD_DIM, nq), BF16),
        scratch_shapes=[pltpu.VMEM((1, K_AUG), F32),
                        pltpu.VMEM((V_AUG, q_per_kv * tq), F32)],
        compiler_params=pltpu.CompilerParams(
            dimension_semantics=("parallel", "parallel", "arbitrary"),
            vmem_limit_bytes=40 * 1024 * 1024),
        name="attention_nk%d" % nk,
    )(q_t, k_aug, v_aug)


def _augment_k(k_rows):
    shp = k_rows.shape[:-1]
    return jnp.concatenate(
        [k_rows, jnp.ones(shp + (1,), k_rows.dtype),
         jnp.zeros(shp + (K_AUG - HEAD_DIM - 1,), k_rows.dtype)], axis=-1)


def _augment_v(v_t):
    bsz, h, _, n = v_t.shape
    return jnp.concatenate(
        [v_t, jnp.ones((bsz, h, 1, n), v_t.dtype),
         jnp.zeros((bsz, h, V_AUG - HEAD_DIM - 1, n), v_t.dtype)], axis=2)


def _dft_a_kernel(bsz, g_ref, tr_ref, ti_ref, o_ref):
    n2 = g_ref.shape[2]
    tab_r = tr_ref[0].astype(BF16)
    tab_i = ti_ref[0].astype(BF16)
    for b in range(bsz):
        res = (_dot(tab_r, g_ref[0, b, :, :W_FOURIER])
               + _dot(tab_i, g_ref[0, b, :, W_FOURIER:]))
        o_ref[0, 0, :, b * W_FOURIER:(b + 1) * W_FOURIER] = res[:n2].astype(BF16)
        o_ref[1, 0, :, b * W_FOURIER:(b + 1) * W_FOURIER] = res[n2:].astype(BF16)


def _dft_c_kernel(a_ref, tc_ref, o_ref):
    o_ref[...] = _dot(tc_ref[...], a_ref[...]).astype(BF16)


def _fourier_latent(gf):
    bsz, n, _ = gf.shape
    n1, n2 = DFT_N1, n // DFT_N1
    tab_r, tab_i, tab_c = _seq_dft_tables(n)
    g_t = jnp.transpose(gf.reshape(bsz, n2, n1, 2 * W_FOURIER), (2, 0, 1, 3))
    lanes = bsz * W_FOURIER
    a = pl.pallas_call(
        functools.partial(_dft_a_kernel, bsz),
        grid=(n1,),
        in_specs=[
            pl.BlockSpec((1, bsz, n2, 2 * W_FOURIER), lambda t: (t, 0, 0, 0)),
            pl.BlockSpec((1, 2 * n2, n2), lambda t: (t, 0, 0)),
            pl.BlockSpec((1, 2 * n2, n2), lambda t: (t, 0, 0)),
        ],
        out_specs=pl.BlockSpec((2, 1, n2, lanes), lambda t: (0, t, 0, 0)),
        out_shape=jax.ShapeDtypeStruct((2, n1, n2, lanes), BF16),
        compiler_params=_params(1, 40),
        name="dft_stage_a",
    )(g_t, tab_r, tab_i)
    a2 = a.reshape(2 * n1, n2 * lanes)
    tl = 4096
    y = pl.pallas_call(
        _dft_c_kernel,
        grid=(n2 * lanes // tl,),
        in_specs=[pl.BlockSpec((2 * n1, tl), lambda j: (0, j)), _resident((n1, 2 * n1))],
        out_specs=pl.BlockSpec((n1, tl), lambda j: (0, j)),
        out_shape=jax.ShapeDtypeStruct((n1, n2 * lanes), BF16),
        compiler_params=_params(1, 40),
        name="dft_stage_c",
    )(a2, jnp.asarray(tab_c).astype(BF16))
    return y.reshape(n, lanes)


def _dft_ctx_kernel(g_ref, c_ref, s_ref, o_ref):
    o_ref[...] = (_dot(c_ref[...], g_ref[0, :, :W_FOURIER])
                  + _dot(s_ref[...], g_ref[0, :, W_FOURIER:])).astype(BF16)


def _fourier_ctx(gf):
    bsz, n, _ = gf.shape
    tab_cos, tab_sin = _ctx_dft_tables(n)
    return pl.pallas_call(
        _dft_ctx_kernel,
        grid=(bsz,),
        in_specs=[pl.BlockSpec((1, n, 2 * W_FOURIER), lambda b: (b, 0, 0)),
                  _resident((n, n)), _resident((n, n))],
        out_specs=pl.BlockSpec((n, W_FOURIER), lambda b: (0, b)),
        out_shape=jax.ShapeDtypeStruct((n, bsz * W_FOURIER), BF16),
        compiler_params=_params(1, 40),
        name="dft_ctx",
    )(gf, jnp.asarray(tab_cos).astype(BF16), jnp.asarray(tab_sin).astype(BF16))


CONV_ROWS = 64


def _mix_kernel(n_tiles, n_seq, x_ref, attn_ref, yf_ref, am_ref, ap_ref, an_ref,
                um_ref, up_ref, un_ref, gate_ref, wf_ref, dww_ref, cvec_ref, wpw_ref,
                wpl_ref, wo_ref, o_ref, g_ref, gsh_ref, pu_ref, psh_ref, cv_ref, pl_ref):
    tm = x_ref.shape[1]
    i = pl.program_id(1)
    first = i == 0
    last = i == n_tiles - 1

    def glu(a_blk):
        a = a_blk.astype(F32)
        return a[:, :W_CONV] * _sigmoid(a[:, W_CONV:])

    g_ref[0:HALO] = jnp.where(first, 0.0, glu(ap_ref[0]))
    g_ref[HALO:HALO + tm] = glu(am_ref[0])
    g_ref[HALO + tm:] = jnp.where(last, 0.0, glu(an_ref[0]))
    pu_ref[0:HALO] = jnp.where(first, 0.0, up_ref[0].astype(F32))
    pu_ref[HALO:HALO + tm] = um_ref[0].astype(F32)
    pu_ref[HALO + tm:] = jnp.where(last, 0.0, un_ref[0].astype(F32))
    rows_sh = gsh_ref.shape[1]
    for b in range(1, V7X_SUBLANES):
        gsh_ref[b - 1] = g_ref[pl.ds(b, rows_sh), :]
        psh_ref[b - 1] = pu_ref[pl.ds(b, rows_sh), :]

    lane = lax.broadcasted_iota(jnp.int32, (CONV_ROWS, W_POOL), 1)
    grp = lane // POOL_GROUP
    half = jnp.left_shift(1, grp)
    row = lax.broadcasted_iota(jnp.int32, (CONV_ROWS, W_POOL), 0)

    def shifted(base_ref, sh_ref, c0, j):
        src = base_ref if j % V7X_SUBLANES == 0 else sh_ref.at[j % V7X_SUBLANES - 1]
        return src[pl.ds(c0 + (j // V7X_SUBLANES) * V7X_SUBLANES, CONV_ROWS), :]

    def chunk(c, _):
        c0 = pl.multiple_of(c * CONV_ROWS, CONV_ROWS)
        acc = jnp.broadcast_to(cvec_ref[0:1, :], (CONV_ROWS, W_CONV))
        for k in range(CONV_WIDTH):
            acc = acc + dww_ref[k:k + 1, :] * shifted(g_ref, gsh_ref, c0, k + 1)
        mu = jnp.mean(acc, axis=-1, keepdims=True)
        yc = acc - mu
        var = jnp.mean(yc * yc, axis=-1, keepdims=True)
        yn = yc * lax.rsqrt(var + EPS) * cvec_ref[1:2, :] + cvec_ref[2:3, :]
        cv_ref[pl.ds(c0, CONV_ROWS), :] = (yn * _sigmoid(yn)).astype(BF16)

        ld = lambda d: shifted(pu_ref, psh_ref, c0, HALO + d)
        u0 = ld(0)
        s2 = ld(-1) + u0
        s4 = s2 + ld(-2) + ld(1)
        s8 = s4 + ld(-4) + ld(-3) + ld(2) + ld(3)
        s16 = s8 + ld(-8) + ld(-7) + ld(-6) + ld(-5) + ld(4) + ld(5) + ld(6) + ld(7)
        win = jnp.where(grp == 0, s2, jnp.where(grp == 1, s4, jnp.where(grp == 2, s8, s16)))
        t = i * tm + c0 + row
        cnt = jnp.minimum(t + half, n_seq) - jnp.maximum(t - half, 0)
        pl_ref[pl.ds(c0, CONV_ROWS), :] = (win / cnt.astype(F32) - u0).astype(BF16)
        return 0

    lax.fori_loop(0, tm // CONV_ROWS, chunk, 0)

    four = _dot(yf_ref[...], wf_ref[...]).astype(BF16)
    conv = _dot(cv_ref[...], wpw_ref[...]).astype(BF16)
    pool = (_dot(pl_ref[...], wpl_ref[...]) * cvec_ref[3:4, :]).astype(BF16)
    mix = (_dot(attn_ref[0], wo_ref[0:W_ATTN, :])
           + _dot(four, wo_ref[W_ATTN:2 * W_ATTN, :])
           + _dot(conv, wo_ref[2 * W_ATTN:3 * W_ATTN, :])
           + _dot(pool, wo_ref[3 * W_ATTN:, :]))
    o_ref[0] = x_ref[0] + gate_ref[0] * mix


def _mixers(x, attn, yf, a_conv, u_pool, gate1, w_f, dw_w, cvec, w_pw, w_pool_bd, w_o, tm):
    bsz, n, d = x.shape
    nt = n // tm
    hb = tm // HALO
    tok = lambda width: pl.BlockSpec((1, tm, width), lambda b, i: (b, i, 0))
    prev = lambda width: pl.BlockSpec(
        (1, HALO, width), lambda b, i: (b, jnp.maximum(i * hb - 1, 0), 0))
    nxt = lambda width: pl.BlockSpec(
        (1, HALO, width), lambda b, i: (b, jnp.minimum((i + 1) * hb, n // HALO - 1), 0))
    rows_sh = tm + 2 * HALO - V7X_SUBLANES
    return pl.pallas_call(
        functools.partial(_mix_kernel, nt, n),
        grid=(bsz, nt),
        in_specs=[
            tok(d), tok(W_ATTN),
            pl.BlockSpec((tm, W_FOURIER), lambda b, i: (i, b)),
            tok(2 * W_CONV), prev(2 * W_CONV), nxt(2 * W_CONV),
            tok(W_POOL), prev(W_POOL), nxt(W_POOL),
            pl.BlockSpec((1, 1, d), lambda b, i: (b, 0, 0)),
            _resident((W_FOURIER, W_FOURIER)),
            _resident((CONV_WIDTH, W_CONV)),
            _resident((V7X_SUBLANES, W_CONV)),
            _resident((W_CONV, W_CONV)),
            _resident((W_POOL, W_POOL)),
            _resident((4 * W_ATTN, d)),
        ],
        out_specs=tok(d),
        out_shape=jax.ShapeDtypeStruct((bsz, n, d), F32),
        scratch_shapes=[
            pltpu.VMEM((tm + 2 * HALO, W_CONV), F32),
            pltpu.VMEM((V7X_SUBLANES - 1, rows_sh, W_CONV), F32),
            pltpu.VMEM((tm + 2 * HALO, W_POOL), F32),
            pltpu.VMEM((V7X_SUBLANES - 1, rows_sh, W_POOL), F32),
            pltpu.VMEM((tm, W_CONV), BF16),
            pltpu.VMEM((tm, W_POOL), BF16),
        ],
        compiler_params=_params(2, 48),
        name="mixers_n%d" % n,
    )(x, attn, yf, a_conv, a_conv, a_conv, u_pool, u_pool, u_pool, gate1,
      w_f, dw_w, cvec, w_pw, w_pool_bd, w_o)


FFN_CHUNK = D_FF // 2


def _ffn_kernel(x_ref, modv_ref, g2_ref, wfi_ref, wfo_ref, o_ref):
    x = x_ref[0]
    ms = jnp.mean(x * x, axis=-1, keepdims=True)
    gain = g2_ref[...] * (1.0 + modv_ref[0, 1:2, :])
    xn = ((x * lax.rsqrt(ms + EPS)) * gain + modv_ref[0, 0:1, :]).astype(BF16)
    acc = None
    for c in range(D_FF // FFN_CHUNK):
        lo = c * FFN_CHUNK
        a = _dot(xn, wfi_ref[:, lo:lo + FFN_CHUNK])
        g = _dot(xn, wfi_ref[:, D_FF + lo:D_FF + lo + FFN_CHUNK])
        h = ((a * _sigmoid(a)) * g).astype(BF16)
        part = _dot(h, wfo_ref[lo:lo + FFN_CHUNK, :])
        acc = part if acc is None else acc + part
    o_ref[0] = x + modv_ref[0, 2:3, :] * acc


def _ffn(x, modv, g2, w_fi, w_fo, tm):
    bsz, n, d = x.shape
    tok = pl.BlockSpec((1, tm, d), lambda b, i: (b, i, 0))
    return pl.pallas_call(
        _ffn_kernel,
        grid=(bsz, n // tm),
        in_specs=[tok, pl.BlockSpec((1, 3, d), lambda b, i: (b, 0, 0)), _resident((1, d)),
                  _resident((d, 2 * D_FF)), _resident((D_FF, d))],
        out_specs=tok,
        out_shape=jax.ShapeDtypeStruct((bsz, n, d), F32),
        compiler_params=_params(2, 52),
        name="ffn_n%d" % n,
    )(x, modv, g2, w_fi, w_fo)


def _heads_t(a, n_heads):
    bsz, n, _ = a.shape
    return jnp.transpose(a.reshape(bsz, n, n_heads, HEAD_DIM), (0, 2, 3, 1))


def _heads_rows(a, n_heads):
    bsz, n, _ = a.shape
    return jnp.transpose(a.reshape(bsz, n, n_heads, HEAD_DIM), (0, 2, 1, 3))


def _from_heads_t(o_t):
    bsz, h, dh, n = o_t.shape
    return jnp.transpose(o_t, (0, 3, 1, 2)).reshape(bsz, n, h * dh)


def _pool_block_diag(w_pool_l):
    groups = w_pool_l.shape[0]
    out = jnp.zeros((W_POOL, W_POOL), w_pool_l.dtype)
    for g in range(groups):
        sl = slice(g * POOL_GROUP, (g + 1) * POOL_GROUP)
        out = out.at[sl, sl].set(w_pool_l[g])
    return out


def kernel(x, c, ctx, c_ctx, w_mod, b_mod, g_norm1, g_norm2, w_in, q_norm_g, k_norm_g,
           w_fourier, conv_dw_w, conv_dw_b, conv_ln_g, conv_ln_b, w_conv_pw, w_pool,
           pool_scale, w_out, w_ffn_in, w_ffn_out):
    bsz, n, d = x.shape
    n_ctx = ctx.shape[1]
    tm_lat, tm_ctx = 512, n_ctx
    tq_lat, tk_lat = 256, 768

    cc = jnp.zeros((V7X_SUBLANES, d), F32).at[:bsz].set(c).at[bsz].set(c_ctx)
    mod_all = _modulation(cc, w_mod, b_mod)

    h = ctx
    for l in range(DEPTH):
        last = l == DEPTH - 1
        m6 = mod_all[l].reshape(V7X_SUBLANES, 6, d)
        lat = m6[:bsz]
        cx = jnp.broadcast_to(m6[bsz:bsz + 1], (bsz, 6, d))
        w_in_l = w_in[l].astype(BF16)
        qk_gain = jnp.concatenate([jnp.tile(q_norm_g[l], N_Q_HEADS),
                                   jnp.tile(k_norm_g[l], N_KV_HEADS)])[None, :]
        g1 = g_norm1[l][None, :]
        g2 = g_norm2[l][None, :]
        w_f = w_fourier[l].astype(BF16)
        w_pw = w_conv_pw[l].astype(BF16)
        w_pl = _pool_block_diag(w_pool[l]).astype(BF16)
        w_o = w_out[l].astype(BF16)
        w_fi = w_ffn_in[l].astype(BF16)
        w_fo = w_ffn_out[l].astype(BF16)
        cvec = jnp.zeros((V7X_SUBLANES, W_CONV), F32)
        cvec = cvec.at[0].set(conv_dw_b[l]).at[1].set(conv_ln_g[l])
        cvec = cvec.at[2].set(conv_ln_b[l]).at[3].set(pool_scale[l])

        qc, kc, vc, gfc, ac, uc = _inproj(h, cx[:, 0:2], g1, w_in_l, qk_gain, False, tm_ctx)
        q, k, v, gf, a, u = _inproj(x, lat[:, 0:2], g1, w_in_l, qk_gain, True, tm_lat)

        kc_r = _augment_k(_heads_rows(kc, N_KV_HEADS))
        vc_t = _augment_v(_heads_t(vc, N_KV_HEADS))
        k_all = jnp.concatenate([kc_r, _augment_k(_heads_rows(k, N_KV_HEADS))], axis=2)
        v_all = jnp.concatenate([vc_t, _augment_v(_heads_t(v, N_KV_HEADS))], axis=3)
        attn = _from_heads_t(_attention(_heads_t(q, N_Q_HEADS), k_all, v_all, tq_lat, tk_lat))
        yf = _fourier_latent(gf)
        x1 = _mixers(x, attn, yf, a, u, lat[:, 2:3], w_f, conv_dw_w[l], cvec, w_pw, w_pl,
                     w_o, tm_lat)
        x_new = _ffn(x1, lat[:, 3:6], g2, w_fi, w_fo, tm_lat)

        if not last:
            attn_c = _from_heads_t(_attention(_heads_t(qc, N_Q_HEADS), kc_r, vc_t, n_ctx, n_ctx))
            yfc = _fourier_ctx(gfc)
            h1 = _mixers(h, attn_c, yfc, ac, uc, cx[:, 2:3], w_f, conv_dw_w[l], cvec, w_pw,
                         w_pl, w_o, tm_ctx)
            h = _ffn(h1, cx[:, 3:6], g2, w_fi, w_fo, tm_ctx)
        x = x_new
    return x
```

```python
import functools

import numpy as np
import jax
import jax.numpy as jnp
from jax import lax
from jax.experimental import pallas as pl
from jax.experimental.pallas import tpu as pltpu

F32 = jnp.float32
BF16 = jnp.bfloat16

D_MODEL = 1024
DEPTH = 2
GRID_W = 64
HEAD_DIM = 64
N_Q_HEADS = 4
N_KV_HEADS = 2
W_ATTN = 256
W_FOURIER = 256
W_CONV = 256
W_POOL = 256
CONV_WIDTH = 31
POOL_GROUP = 64
D_FF = 2816
ROPE_THETA = 10000.0
EPS = 1e-6
ATTN_SCALE = HEAD_DIM ** -0.5
Q_SCALE_LOG2 = ATTN_SCALE * float(np.log2(np.e))
W_QK = W_ATTN + N_KV_HEADS * HEAD_DIM
OFF_V = W_QK
OFF_F = OFF_V + N_KV_HEADS * HEAD_DIM
OFF_C = OFF_F + W_FOURIER
OFF_P = OFF_C + 2 * W_CONV
D_IN = OFF_P + W_POOL

V7X_LANES = 128
V7X_SUBLANES = 8
V7X_BF16_ROWS = 16
V7X_VMEM_BYTES = 64 * 1024 * 1024

HALO = V7X_BF16_ROWS
DFT_N1 = 64
NEG_BIG = -1e30


def _dot(a, b):
    return jnp.dot(a, b, preferred_element_type=F32)


def _sigmoid(x):
    return 1.0 / (1.0 + jnp.exp(-x))


def _split_bf16(x):
    hi = x.astype(BF16)
    lo = (x - hi.astype(F32)).astype(BF16)
    return hi, lo


def _resident(shape):
    nd = len(shape)
    return pl.BlockSpec(shape, lambda *_: (0,) * nd, pipeline_mode=pl.Buffered(1))


def _params(n_grid, vmem_mb):
    return pltpu.CompilerParams(
        dimension_semantics=("parallel",) * n_grid,
        vmem_limit_bytes=vmem_mb * 1024 * 1024)


@functools.lru_cache(maxsize=None)
def _rope_tables(n):
    rows = n // GRID_W
    row = np.repeat(np.arange(rows, dtype=np.float64), GRID_W)
    col = np.tile(np.arange(GRID_W, dtype=np.float64), rows)
    n_freq = HEAD_DIM // 4
    inv_freq = ROPE_THETA ** (-np.arange(n_freq, dtype=np.float64) / n_freq)
    ang = np.concatenate([row[:, None] * inv_freq, col[:, None] * inv_freq], axis=-1)
    cos = np.repeat(np.cos(ang), 2, axis=1)
    sin = np.repeat(np.sin(ang), 2, axis=1)
    sign = np.tile(np.array([-1.0, 1.0]), HEAD_DIM // 2)
    cos2 = np.tile(cos, (1, 2)).astype(np.float32)
    sin2 = np.tile(sin * sign, (1, 2)).astype(np.float32)
    return cos2, sin2


@functools.lru_cache(maxsize=None)
def _head_sum_matrix():
    idx = np.arange(W_QK) // HEAD_DIM
    return np.asarray((idx[:, None] == idx[None, :]).astype(np.float32), dtype=BF16)


@functools.lru_cache(maxsize=None)
def _channel_dft():
    c = np.arange(W_FOURIER)
    same = (c[:, None] // HEAD_DIM) == (c[None, :] // HEAD_DIM)
    ang = 2.0 * np.pi * ((c[:, None] % HEAD_DIM) * (c[None, :] % HEAD_DIM)) / HEAD_DIM
    cr = np.where(same, np.cos(ang), 0.0) / 8.0
    ci = np.where(same, -np.sin(ang), 0.0) / 8.0
    return np.concatenate([cr, ci], axis=1).astype(np.float32)


@functools.lru_cache(maxsize=None)
def _seq_dft_tables(n):
    n1, n2 = DFT_N1, n // DFT_N1
    t1 = np.arange(n1)[:, None, None]
    k2 = np.arange(n2)[None, :, None]
    t2 = np.arange(n2)[None, None, :]
    theta = 2.0 * np.pi * ((k2 * (t1 + n1 * t2)) % n) / n
    dr = np.cos(theta) / np.sqrt(n2)
    di = -np.sin(theta) / np.sqrt(n2)
    tab_r = np.concatenate([dr, di], axis=1)
    tab_i = np.concatenate([-di, dr], axis=1)
    k1 = np.arange(n1)[:, None]
    tt = np.arange(n1)[None, :]
    phi = 2.0 * np.pi * ((k1 * tt) % n1) / n1
    tab_c = np.concatenate([np.cos(phi), np.sin(phi)], axis=1) / np.sqrt(n1)
    return tab_r.astype(np.float32), tab_i.astype(np.float32), tab_c.astype(np.float32)


@functools.lru_cache(maxsize=None)
def _ctx_dft_tables(n):
    k = np.arange(n)[:, None]
    t = np.arange(n)[None, :]
    ang = 2.0 * np.pi * ((k * t) % n) / n
    s = 1.0 / np.sqrt(n)
    return (np.cos(ang) * s).astype(np.float32), (np.sin(ang) * s).astype(np.float32)


def _mod_kernel(c_ref, w_ref, b_ref, o_ref):
    c = c_ref[...]
    sc = c * _sigmoid(c)
    a_hi, a_lo = _split_bf16(sc)
    w_hi, w_lo = _split_bf16(w_ref[0])
    o_ref[0] = _dot(a_hi, w_hi) + _dot(a_lo, w_hi) + _dot(a_hi, w_lo) + b_ref[0]


def _modulation(cc, w_mod, b_mod):
    depth, d, d6 = w_mod.shape
    tn = 1536
    return pl.pallas_call(
        _mod_kernel,
        grid=(depth, d6 // tn),
        in_specs=[
            pl.BlockSpec((V7X_SUBLANES, d), lambda l, j: (0, 0)),
            pl.BlockSpec((1, d, tn), lambda l, j: (l, 0, j)),
            pl.BlockSpec((1, 1, tn), lambda l, j: (l, 0, j)),
        ],
        out_specs=pl.BlockSpec((1, V7X_SUBLANES, tn), lambda l, j: (l, 0, j)),
        out_shape=jax.ShapeDtypeStruct((depth, V7X_SUBLANES, d6), F32),
        compiler_params=_params(2, 40),
        name="modulation",
    )(cc, w_mod, b_mod.reshape(depth, 1, d6))


def _rope(x, cos, sin):
    lane = lax.broadcasted_iota(jnp.int32, x.shape, 1)
    swapped = jnp.where((lane & 1) == 0,
                        pltpu.roll(x, V7X_LANES - 1, axis=1),
                        pltpu.roll(x, 1, axis=1))
    return x * cos + swapped * sin


def _inproj_kernel(rope, x_ref, modv_ref, g1_ref, w_ref, qkg_ref, bd_ref, cd_ref, *rest):
    if rope:
        cos_ref, sin_ref = rest[:2]
        rest = rest[2:]
    q_ref, k_ref, v_ref, gf_ref, a_ref, u_ref = rest
    x = x_ref[0]
    ms = jnp.mean(x * x, axis=-1, keepdims=True)
    gain = g1_ref[...] * (1.0 + modv_ref[0, 1:2, :])
    xn = (x * lax.rsqrt(ms + EPS)) * gain + modv_ref[0, 0:1, :]
    p = _dot(xn.astype(BF16), w_ref[...])

    qk = p[:, :W_QK]
    sq_hi, sq_lo = _split_bf16(qk * qk)
    ss = _dot(sq_hi, bd_ref[...]) + _dot(sq_lo, bd_ref[...])
    qkn = qk * lax.rsqrt(ss * (1.0 / HEAD_DIM) + EPS) * qkg_ref[...]
    slabs = []
    for j in range(W_QK // V7X_LANES):
        slab = qkn[:, j * V7X_LANES:(j + 1) * V7X_LANES]
        if rope:
            slab = _rope(slab, cos_ref[...], sin_ref[...])
        slabs.append(slab)
    for j in range(W_ATTN // V7X_LANES):
        qt = (slabs[j] * Q_SCALE_LOG2).T.astype(BF16)
        q_ref[0, 2 * j] = qt[:HEAD_DIM]
        q_ref[0, 2 * j + 1] = qt[HEAD_DIM:]
    lane = lax.broadcasted_iota(jnp.int32, slabs[2].shape, 1)
    one_col = jnp.where(lane == HEAD_DIM, 1.0, 0.0)
    k_ref[0, 0] = jnp.where(lane < HEAD_DIM, slabs[2], one_col).astype(BF16)
    k_ref[0, 1] = jnp.where(lane < HEAD_DIM, pltpu.roll(slabs[2], HEAD_DIM, axis=1),
                            one_col).astype(BF16)
    vt = p[:, OFF_V:OFF_F].T.astype(BF16)
    tm = vt.shape[1]
    row = lax.broadcasted_iota(jnp.int32, (V_AUG - HEAD_DIM, tm), 0)
    ones_rows = jnp.where(row == 0, 1.0, 0.0).astype(BF16)
    for g in range(N_KV_HEADS):
        v_ref[0, g, 0:HEAD_DIM, :] = vt[g * HEAD_DIM:(g + 1) * HEAD_DIM]
        v_ref[0, g, HEAD_DIM:, :] = ones_rows
    gf_ref[0] =_dot(p[:, OFF_F:OFF_C].astype(BF16), cd_ref[...]).astype(BF16)
    a_ref[0] = p[:, OFF_C:OFF_P].astype(BF16)
    u_ref[0] = p[:, OFF_P:].astype(BF16)


def _inproj(x, modv, g1, w_in, qk_gain, rope, tm):
    bsz, n, d = x.shape
    nt = n // tm
    tok = lambda width: pl.BlockSpec((1, tm, width), lambda b, i: (b, i, 0))
    in_specs = [
        tok(d),
        pl.BlockSpec((1, 2, d), lambda b, i: (b, 0, 0)),
        _resident((1, d)),
        _resident((d, D_IN)),
        _resident((1, W_QK)),
        _resident((W_QK, W_QK)),
        _resident((W_FOURIER, 2 * W_FOURIER)),
    ]
    args = [x, modv, g1, w_in, qk_gain, _head_sum_matrix(),
            jnp.asarray(_channel_dft()).astype(BF16)]
    if rope:
        cos2, sin2 = _rope_tables(n)
        in_specs += [pl.BlockSpec((tm, V7X_LANES), lambda b, i: (i, 0))] * 2
        args += [cos2, sin2]
    widths = (2 * W_FOURIER, 2 * W_CONV, W_POOL)
    out_specs = [
        pl.BlockSpec((1, N_Q_HEADS, HEAD_DIM, tm), lambda b, i: (b, 0, 0, i)),
        pl.BlockSpec((1, N_KV_HEADS, tm, K_AUG), lambda b, i: (b, 0, i, 0)),
        pl.BlockSpec((1, N_KV_HEADS, V_AUG, tm), lambda b, i: (b, 0, 0, i)),
    ] + [tok(w) for w in widths]
    out_shape = [
        jax.ShapeDtypeStruct((bsz, N_Q_HEADS, HEAD_DIM, n), BF16),
        jax.ShapeDtypeStruct((bsz, N_KV_HEADS, n, K_AUG), BF16),
        jax.ShapeDtypeStruct((bsz, N_KV_HEADS, V_AUG, n), BF16),
    ] + [jax.ShapeDtypeStruct((bsz, n, w), BF16) for w in widths]
    return pl.pallas_call(
        functools.partial(_inproj_kernel, rope),
        grid=(bsz, nt),
        in_specs=in_specs,
        out_specs=out_specs,
        out_shape=out_shape,
        compiler_params=_params(2, 40),
        name="inproj_rope" if rope else "inproj_ctx",
    )(*args)


K_AUG = V7X_LANES
V_AUG = HEAD_DIM + V7X_BF16_ROWS
SAFE_SHIFT_LOG2 = 50.0


def _attn_kernel(chunks, q_ref, *refs):
    n_src = (len(refs) - 3) // 2
    k_refs, v_refs = refs[:n_src], refs[n_src:2 * n_src]
    o_ref, kmax_ref, acc_ref = refs[2 * n_src:]
    n_chunks = len(chunks)
    tq = q_ref.shape[-1]
    i = pl.program_id(2)

    def keys(c):
        src, start, size = chunks[c]
        return k_refs[src][0, 0, start:start + size, :]

    def values(c):
        src, start, size = chunks[c]
        return v_refs[src][0, 0, :, start:start + size]

    @pl.when(i == 0)
    def _():
        ones = jnp.ones((K_AUG, K_AUG), BF16)
        mx = jnp.zeros((1, K_AUG), F32)
        for c in range(n_chunks):
            kf = keys(c).astype(F32)
            norms = _dot((kf * kf).astype(BF16), ones)
            mx = jnp.maximum(mx, jnp.max(norms, axis=0, keepdims=True))
        kmax_ref[...] = jnp.sqrt(mx)

    q2 = jnp.concatenate([q_ref[0, 0], q_ref[0, 1]], axis=1)
    qf = q2.astype(F32)
    kmax = jnp.tile(kmax_ref[...], (1, 2 * tq // K_AUG))
    bound = jnp.sqrt(jnp.sum(qf * qf, axis=0, keepdims=True)) * kmax
    safe = jnp.max(bound) <= SAFE_SHIFT_LOG2
    row = lax.broadcasted_iota(jnp.int32, (K_AUG - HEAD_DIM, 2 * tq), 0)
    shift_rows = jnp.where(row == 0, -bound, 0.0).astype(BF16)
    q_aug = jnp.concatenate([q2, shift_rows], axis=0)

    def logits(c):
        return _dot(keys(c), q_aug)

    @pl.when(safe)
    def _():
        acc = None
        for c in range(n_chunks):
            p = jnp.exp2(logits(c)).astype(BF16)
            part = _dot(values(c), p)
            acc = part if acc is None else acc + part
        acc_ref[...] = acc

    @pl.when(jnp.logical_not(safe))
    def _():
        m = jnp.full((1, 2 * tq), NEG_BIG, F32)
        acc = jnp.zeros((V_AUG, 2 * tq), F32)
        for c in range(n_chunks):
            s = logits(c)
            m_new = jnp.maximum(m, jnp.max(s, axis=0, keepdims=True))
            p = jnp.exp2(s - m_new).astype(BF16)
            acc = jnp.exp2(m - m_new) * acc + _dot(values(c), p)
            m = m_new
        acc_ref[...] = acc

    o = acc_ref[0:HEAD_DIM, :] * (1.0 / acc_ref[HEAD_DIM:HEAD_DIM + 1, :])
    o2 = jnp.concatenate([o[:, :tq], o[:, tq:]], axis=0)
    o_ref[0] = o2.T.astype(BF16)


def _attention(q_t, kv_sources, tq, tk):
    bsz, _, _, nq = q_t.shape
    q_per_kv = N_Q_HEADS // N_KV_HEADS
    chunks = []
    for src, (k_aug, _) in enumerate(kv_sources):
        nk = k_aug.shape[2]
        step = min(tk, nk)
        chunks += [(src, start, step) for start in range(0, nk, step)]
    k_specs = [pl.BlockSpec((1, 1, k.shape[2], K_AUG), lambda b, g, i: (b, g, 0, 0))
               for k, _ in kv_sources]
    v_specs = [pl.BlockSpec((1, 1, V_AUG, v.shape[3]), lambda b, g, i: (b, g, 0, 0))
               for _, v in kv_sources]
    return pl.pallas_call(
        functools.partial(_attn_kernel, tuple(chunks)),
        grid=(bsz, N_KV_HEADS, nq // tq),
        in_specs=[pl.BlockSpec((1, q_per_kv, HEAD_DIM, tq), lambda b, g, i: (b, g, 0, i))]
        + k_specs + v_specs,
        out_specs=pl.BlockSpec((1, tq, q_per_kv * HEAD_DIM), lambda b, g, i: (b, i, g)),
        out_shape=jax.ShapeDtypeStruct((bsz, nq, W_ATTN), BF16),
        scratch_shapes=[pltpu.VMEM((1, K_AUG), F32),
                        pltpu.VMEM((V_AUG, q_per_kv * tq), F32)],
        compiler_params=pltpu.CompilerParams(
            dimension_semantics=("parallel", "parallel", "arbitrary"),
            vmem_limit_bytes=40 * 1024 * 1024),
        name="attention_%dsrc" % len(kv_sources),
    )(q_t, *[k for k, _ in kv_sources], *[v for _, v in kv_sources])


def _dft_a_kernel(bsz, g_ref, tr_ref, ti_ref, o_ref):
    n2 = g_ref.shape[2]
    tab_r = tr_ref[0].astype(BF16)
    tab_i = ti_ref[0].astype(BF16)
    for b in range(bsz):
        res = (_dot(tab_r, g_ref[0, b, :, :W_FOURIER])
               + _dot(tab_i, g_ref[0, b, :, W_FOURIER:]))
        o_ref[0, 0, :, b * W_FOURIER:(b + 1) * W_FOURIER] = res[:n2].astype(BF16)
        o_ref[1, 0, :, b * W_FOURIER:(b + 1) * W_FOURIER] = res[n2:].astype(BF16)


def _dft_c_kernel(a_ref, tc_ref, o_ref):
    o_ref[...] = _dot(tc_ref[...], a_ref[...]).astype(BF16)


def _fourier_latent(gf):
    bsz, n, _ = gf.shape
    n1, n2 = DFT_N1, n // DFT_N1
    tab_r, tab_i, tab_c = _seq_dft_tables(n)
    g_t = jnp.transpose(gf.reshape(bsz, n2, n1, 2 * W_FOURIER), (2, 0, 1, 3))
    lanes = bsz * W_FOURIER
    a = pl.pallas_call(
        functools.partial(_dft_a_kernel, bsz),
        grid=(n1,),
        in_specs=[
            pl.BlockSpec((1, bsz, n2, 2 * W_FOURIER), lambda t: (t, 0, 0, 0)),
            pl.BlockSpec((1, 2 * n2, n2), lambda t: (t, 0, 0)),
            pl.BlockSpec((1, 2 * n2, n2), lambda t: (t, 0, 0)),
        ],
        out_specs=pl.BlockSpec((2, 1, n2, lanes), lambda t: (0, t, 0, 0)),
        out_shape=jax.ShapeDtypeStruct((2, n1, n2, lanes), BF16),
        compiler_params=_params(1, 40),
        name="dft_stage_a",
    )(g_t, tab_r, tab_i)
    a2 = a.reshape(2 * n1, n2 * lanes)
    tl = 4096
    y = pl.pallas_call(
        _dft_c_kernel,
        grid=(n2 * lanes // tl,),
        in_specs=[pl.BlockSpec((2 * n1, tl), lambda j: (0, j)), _resident((n1, 2 * n1))],
        out_specs=pl.BlockSpec((n1, tl), lambda j: (0, j)),
        out_shape=jax.ShapeDtypeStruct((n1, n2 * lanes), BF16),
        compiler_params=_params(1, 40),
        name="dft_stage_c",
    )(a2, jnp.asarray(tab_c).astype(BF16))
    return y.reshape(n, lanes)


def _dft_ctx_kernel(g_ref, c_ref, s_ref, o_ref):
    o_ref[...] = (_dot(c_ref[...], g_ref[0, :, :W_FOURIER])
                  + _dot(s_ref[...], g_ref[0, :, W_FOURIER:])).astype(BF16)


def _fourier_ctx(gf):
    bsz, n, _ = gf.shape
    tab_cos, tab_sin = _ctx_dft_tables(n)
    return pl.pallas_call(
        _dft_ctx_kernel,
        grid=(bsz,),
        in_specs=[pl.BlockSpec((1, n, 2 * W_FOURIER), lambda b: (b, 0, 0)),
                  _resident((n, n)), _resident((n, n))],
        out_specs=pl.BlockSpec((n, W_FOURIER), lambda b: (0, b)),
        out_shape=jax.ShapeDtypeStruct((n, bsz * W_FOURIER), BF16),
        compiler_params=_params(1, 40),
        name="dft_ctx",
    )(gf, jnp.asarray(tab_cos).astype(BF16), jnp.asarray(tab_sin).astype(BF16))


CONV_ROWS = 64


def _mix_kernel(n_tiles, n_seq, x_ref, attn_ref, yf_ref, am_ref, ap_ref, an_ref,
                um_ref, up_ref, un_ref, gate_ref, wf_ref, dww_ref, cvec_ref, wpw_ref,
                wpl_ref, wo_ref, o_ref, g_ref, gsh_ref, pu_ref, psh_ref, cv_ref, pl_ref):
    tm = x_ref.shape[1]
    i = pl.program_id(1)
    first = i == 0
    last = i == n_tiles - 1

    def glu(a_blk):
        a = a_blk.astype(F32)
        return a[:, :W_CONV] * _sigmoid(a[:, W_CONV:])

    g_ref[0:HALO] = jnp.where(first, 0.0, glu(ap_ref[0]))
    g_ref[HALO:HALO + tm] = glu(am_ref[0])
    g_ref[HALO + tm:] = jnp.where(last, 0.0, glu(an_ref[0]))
    pu_ref[0:HALO] = jnp.where(first, 0.0, up_ref[0].astype(F32))
    pu_ref[HALO:HALO + tm] = um_ref[0].astype(F32)
    pu_ref[HALO + tm:] = jnp.where(last, 0.0, un_ref[0].astype(F32))
    rows_sh = gsh_ref.shape[1]
    for b in range(1, V7X_SUBLANES):
        gsh_ref[b - 1] = g_ref[pl.ds(b, rows_sh), :]
        psh_ref[b - 1] = pu_ref[pl.ds(b, rows_sh), :]

    lane = lax.broadcasted_iota(jnp.int32, (CONV_ROWS, W_POOL), 1)
    grp = lane // POOL_GROUP
    half = jnp.left_shift(1, grp)
    row = lax.broadcasted_iota(jnp.int32, (CONV_ROWS, W_POOL), 0)

    def shifted(base_ref, sh_ref, c0, j):
        src = base_ref if j % V7X_SUBLANES == 0 else sh_ref.at[j % V7X_SUBLANES - 1]
        return src[pl.ds(c0 + (j // V7X_SUBLANES) * V7X_SUBLANES, CONV_ROWS), :]

    def chunk(c, _):
        c0 = pl.multiple_of(c * CONV_ROWS, CONV_ROWS)
        acc = jnp.broadcast_to(cvec_ref[0:1, :], (CONV_ROWS, W_CONV))
        for k in range(CONV_WIDTH):
            acc = acc + dww_ref[k:k + 1, :] * shifted(g_ref, gsh_ref, c0, k + 1)
        mu = jnp.mean(acc, axis=-1, keepdims=True)
        yc = acc - mu
        var = jnp.mean(yc * yc, axis=-1, keepdims=True)
        yn = yc * lax.rsqrt(var + EPS) * cvec_ref[1:2, :] + cvec_ref[2:3, :]
        cv_ref[pl.ds(c0, CONV_ROWS), :] = (yn * _sigmoid(yn)).astype(BF16)

        ld = lambda d: shifted(pu_ref, psh_ref, c0, HALO + d)
        u0 = ld(0)
        s2 = ld(-1) + u0
        s4 = s2 + ld(-2) + ld(1)
        s8 = s4 + ld(-4) + ld(-3) + ld(2) + ld(3)
        s16 = s8 + ld(-8) + ld(-7) + ld(-6) + ld(-5) + ld(4) + ld(5) + ld(6) + ld(7)
        win = jnp.where(grp == 0, s2, jnp.where(grp == 1, s4, jnp.where(grp == 2, s8, s16)))
        t = i * tm + c0 + row
        cnt = jnp.minimum(t + half, n_seq) - jnp.maximum(t - half, 0)
        pl_ref[pl.ds(c0, CONV_ROWS), :] = (win / cnt.astype(F32) - u0).astype(BF16)
        return 0

    lax.fori_loop(0, tm // CONV_ROWS, chunk, 0)

    four = _dot(yf_ref[...], wf_ref[...]).astype(BF16)
    conv = _dot(cv_ref[...], wpw_ref[...]).astype(BF16)
    pool = (_dot(pl_ref[...], wpl_ref[...]) * cvec_ref[3:4, :]).astype(BF16)
    mix = (_dot(attn_ref[0], wo_ref[0:W_ATTN, :])
           + _dot(four, wo_ref[W_ATTN:2 * W_ATTN, :])
           + _dot(conv, wo_ref[2 * W_ATTN:3 * W_ATTN, :])
           + _dot(pool, wo_ref[3 * W_ATTN:, :]))
    o_ref[0] = x_ref[0] + gate_ref[0] * mix


def _mixers(x, attn, yf, a_conv, u_pool, gate1, w_f, dw_w, cvec, w_pw, w_pool_bd, w_o, tm):
    bsz, n, d = x.shape
    nt = n // tm
    hb = tm // HALO
    tok = lambda width: pl.BlockSpec((1, tm, width), lambda b, i: (b, i, 0))
    prev = lambda width: pl.BlockSpec(
        (1, HALO, width), lambda b, i: (b, jnp.maximum(i * hb - 1, 0), 0))
    nxt = lambda width: pl.BlockSpec(
        (1, HALO, width), lambda b, i: (b, jnp.minimum((i + 1) * hb, n // HALO - 1), 0))
    rows_sh = tm + 2 * HALO - V7X_SUBLANES
    return pl.pallas_call(
        functools.partial(_mix_kernel, nt, n),
        grid=(bsz, nt),
        in_specs=[
            tok(d), tok(W_ATTN),
            pl.BlockSpec((tm, W_FOURIER), lambda b, i: (i, b)),
            tok(2 * W_CONV), prev(2 * W_CONV), nxt(2 * W_CONV),
            tok(W_POOL), prev(W_POOL), nxt(W_POOL),
            pl.BlockSpec((1, 1, d), lambda b, i: (b, 0, 0)),
            _resident((W_FOURIER, W_FOURIER)),
            _resident((CONV_WIDTH, W_CONV)),
            _resident((V7X_SUBLANES, W_CONV)),
            _resident((W_CONV, W_CONV)),
            _resident((W_POOL, W_POOL)),
            _resident((4 * W_ATTN, d)),
        ],
        out_specs=tok(d),
        out_shape=jax.ShapeDtypeStruct((bsz, n, d), F32),
        scratch_shapes=[
            pltpu.VMEM((tm + 2 * HALO, W_CONV), F32),
            pltpu.VMEM((V7X_SUBLANES - 1, rows_sh, W_CONV), F32),
            pltpu.VMEM((tm + 2 * HALO, W_POOL), F32),
            pltpu.VMEM((V7X_SUBLANES - 1, rows_sh, W_POOL), F32),
            pltpu.VMEM((tm, W_CONV), BF16),
            pltpu.VMEM((tm, W_POOL), BF16),
        ],
        compiler_params=_params(2, 48),
        name="mixers_n%d" % n,
    )(x, attn, yf, a_conv, a_conv, a_conv, u_pool, u_pool, u_pool, gate1,
      w_f, dw_w, cvec, w_pw, w_pool_bd, w_o)


FFN_CHUNK = D_FF // 2


def _ffn_kernel(x_ref, modv_ref, g2_ref, wfi_ref, wfo_ref, o_ref):
    x = x_ref[0]
    ms = jnp.mean(x * x, axis=-1, keepdims=True)
    gain = g2_ref[...] * (1.0 + modv_ref[0, 1:2, :])
    xn = ((x * lax.rsqrt(ms + EPS)) * gain + modv_ref[0, 0:1, :]).astype(BF16)
    acc = None
    for c in range(D_FF // FFN_CHUNK):
        lo = c * FFN_CHUNK
        a = _dot(xn, wfi_ref[:, lo:lo + FFN_CHUNK])
        g = _dot(xn, wfi_ref[:, D_FF + lo:D_FF + lo + FFN_CHUNK])
        h = ((a * _sigmoid(a)) * g).astype(BF16)
        part = _dot(h, wfo_ref[lo:lo + FFN_CHUNK, :])
        acc = part if acc is None else acc + part
    o_ref[0] = x + modv_ref[0, 2:3, :] * acc


def _ffn(x, modv, g2, w_fi, w_fo, tm):
    bsz, n, d = x.shape
    tok = pl.BlockSpec((1, tm, d), lambda b, i: (b, i, 0))
    return pl.pallas_call(
        _ffn_kernel,
        grid=(bsz, n // tm),
        in_specs=[tok, pl.BlockSpec((1, 3, d), lambda b, i: (b, 0, 0)), _resident((1, d)),
                  _resident((d, 2 * D_FF)), _resident((D_FF, d))],
        out_specs=tok,
        out_shape=jax.ShapeDtypeStruct((bsz, n, d), F32),
        compiler_params=_params(2, 52),
        name="ffn_n%d" % n,
    )(x, modv, g2, w_fi, w_fo)


def _pool_block_diag(w_pool_l):
    groups = w_pool_l.shape[0]
    out = jnp.zeros((W_POOL, W_POOL), w_pool_l.dtype)
    for g in range(groups):
        sl = slice(g * POOL_GROUP, (g + 1) * POOL_GROUP)
        out = out.at[sl, sl].set(w_pool_l[g])
    return out


def kernel(x, c, ctx, c_ctx, w_mod, b_mod, g_norm1, g_norm2, w_in, q_norm_g, k_norm_g,
           w_fourier, conv_dw_w, conv_dw_b, conv_ln_g, conv_ln_b, w_conv_pw, w_pool,
           pool_scale, w_out, w_ffn_in, w_ffn_out):
    bsz, n, d = x.shape
    n_ctx = ctx.shape[1]
    tm_lat, tm_ctx = 512, n_ctx
    tq_lat, tk_lat = 256, 1024

    cc = jnp.zeros((V7X_SUBLANES, d), F32).at[:bsz].set(c).at[bsz].set(c_ctx)
    mod_all = _modulation(cc, w_mod, b_mod)

    h = ctx
    for l in range(DEPTH):
        last = l == DEPTH - 1
        m6 = mod_all[l].reshape(V7X_SUBLANES, 6, d)
        lat = m6[:bsz]
        cx = jnp.broadcast_to(m6[bsz:bsz + 1], (bsz, 6, d))
        w_in_l = w_in[l].astype(BF16)
        qk_gain = jnp.concatenate([jnp.tile(q_norm_g[l], N_Q_HEADS),
                                   jnp.tile(k_norm_g[l], N_KV_HEADS)])[None, :]
        g1 = g_norm1[l][None, :]
        g2 = g_norm2[l][None, :]
        w_f = w_fourier[l].astype(BF16)
        w_pw = w_conv_pw[l].astype(BF16)
        w_pl = _pool_block_diag(w_pool[l]).astype(BF16)
        w_o = w_out[l].astype(BF16)
        w_fi = w_ffn_in[l].astype(BF16)
        w_fo = w_ffn_out[l].astype(BF16)
        cvec = jnp.zeros((V7X_SUBLANES, W_CONV), F32)
        cvec = cvec.at[0].set(conv_dw_b[l]).at[1].set(conv_ln_g[l])
        cvec = cvec.at[2].set(conv_ln_b[l]).at[3].set(pool_scale[l])

        qc, kc, vc, gfc, ac, uc = _inproj(h, cx[:, 0:2], g1, w_in_l, qk_gain, False, tm_ctx)
        q, k, v, gf, a, u = _inproj(x, lat[:, 0:2], g1, w_in_l, qk_gain, True, tm_lat)

        attn = _attention(q, [(kc, vc), (k, v)], tq_lat, tk_lat)
        yf = _fourier_latent(gf)
        x1 = _mixers(x, attn, yf, a, u, lat[:, 2:3], w_f, conv_dw_w[l], cvec, w_pw, w_pl,
                     w_o, tm_lat)
        x_new = _ffn(x1, lat[:, 3:6], g2, w_fi, w_fo, tm_lat)

        if not last:
            attn_c = _attention(qc, [(kc, vc)], n_ctx, n_ctx)
            yfc = _fourier_ctx(gfc)
            h1 = _mixers(h, attn_c, yfc, ac, uc, cx[:, 2:3], w_f, conv_dw_w[l], cvec, w_pw,
                         w_pl, w_o, tm_ctx)
            h = _ffn(h1, cx[:, 3:6], g2, w_fi, w_fo, tm_ctx)
        x = x_new
    return x
```

```python
import functools

import numpy as np
import jax
import jax.numpy as jnp
from jax import lax
from jax.experimental import pallas as pl
from jax.experimental.pallas import tpu as pltpu

F32 = jnp.float32
BF16 = jnp.bfloat16

D_MODEL = 1024
DEPTH = 2
GRID_W = 64
HEAD_DIM = 64
N_Q_HEADS = 4
N_KV_HEADS = 2
W_ATTN = 256
W_FOURIER = 256
W_CONV = 256
W_POOL = 256
CONV_WIDTH = 31
POOL_GROUP = 64
D_FF = 2816
ROPE_THETA = 10000.0
EPS = 1e-6
ATTN_SCALE = HEAD_DIM ** -0.5
Q_SCALE_LOG2 = ATTN_SCALE * float(np.log2(np.e))
W_QK = W_ATTN + N_KV_HEADS * HEAD_DIM
OFF_V = W_QK
OFF_F = OFF_V + N_KV_HEADS * HEAD_DIM
OFF_C = OFF_F + W_FOURIER
OFF_P = OFF_C + 2 * W_CONV
D_IN = OFF_P + W_POOL

V7X_LANES = 128
V7X_SUBLANES = 8
V7X_BF16_ROWS = 16
V7X_VMEM_BYTES = 64 * 1024 * 1024

HALO = V7X_BF16_ROWS
DFT_N1 = 64
NEG_BIG = -1e30


def _dot(a, b):
    return jnp.dot(a, b, preferred_element_type=F32)


def _sigmoid(x):
    return 1.0 / (1.0 + jnp.exp(-x))


def _split_bf16(x):
    hi = x.astype(BF16)
    lo = (x - hi.astype(F32)).astype(BF16)
    return hi, lo


def _resident(shape):
    nd = len(shape)
    return pl.BlockSpec(shape, lambda *_: (0,) * nd, pipeline_mode=pl.Buffered(1))


def _params(n_grid, vmem_mb):
    return pltpu.CompilerParams(
        dimension_semantics=("parallel",) * n_grid,
        vmem_limit_bytes=vmem_mb * 1024 * 1024)


@functools.lru_cache(maxsize=None)
def _rope_tables(n):
    rows = n // GRID_W
    row = np.repeat(np.arange(rows, dtype=np.float64), GRID_W)
    col = np.tile(np.arange(GRID_W, dtype=np.float64), rows)
    n_freq = HEAD_DIM // 4
    inv_freq = ROPE_THETA ** (-np.arange(n_freq, dtype=np.float64) / n_freq)
    ang = np.concatenate([row[:, None] * inv_freq, col[:, None] * inv_freq], axis=-1)
    cos = np.repeat(np.cos(ang), 2, axis=1)
    sin = np.repeat(np.sin(ang), 2, axis=1)
    sign = np.tile(np.array([-1.0, 1.0]), HEAD_DIM // 2)
    cos2 = np.tile(cos, (1, 2)).astype(np.float32)
    sin2 = np.tile(sin * sign, (1, 2)).astype(np.float32)
    return cos2, sin2


@functools.lru_cache(maxsize=None)
def _head_sum_matrix():
    idx = np.arange(W_QK) // HEAD_DIM
    return np.asarray((idx[:, None] == idx[None, :]).astype(np.float32), dtype=BF16)


@functools.lru_cache(maxsize=None)
def _channel_dft():
    c = np.arange(W_FOURIER)
    same = (c[:, None] // HEAD_DIM) == (c[None, :] // HEAD_DIM)
    ang = 2.0 * np.pi * ((c[:, None] % HEAD_DIM) * (c[None, :] % HEAD_DIM)) / HEAD_DIM
    cr = np.where(same, np.cos(ang), 0.0) / 8.0
    ci = np.where(same, -np.sin(ang), 0.0) / 8.0
    return np.concatenate([cr, ci], axis=1).astype(np.float32)


@functools.lru_cache(maxsize=None)
def _seq_dft_tables(n):
    n1, n2 = DFT_N1, n // DFT_N1
    t1 = np.arange(n1)[:, None, None]
    k2 = np.arange(n2)[None, :, None]
    t2 = np.arange(n2)[None, None, :]
    theta = 2.0 * np.pi * ((k2 * (t1 + n1 * t2)) % n) / n
    dr = np.cos(theta) / np.sqrt(n2)
    di = -np.sin(theta) / np.sqrt(n2)
    tab_r = np.concatenate([dr, di], axis=1)
    tab_i = np.concatenate([-di, dr], axis=1)
    k1 = np.arange(n1)[:, None]
    tt = np.arange(n1)[None, :]
    phi = 2.0 * np.pi * ((k1 * tt) % n1) / n1
    tab_c = np.concatenate([np.cos(phi), np.sin(phi)], axis=1) / np.sqrt(n1)
    return tab_r.astype(np.float32), tab_i.astype(np.float32), tab_c.astype(np.float32)


@functools.lru_cache(maxsize=None)
def _ctx_dft_tables(n):
    k = np.arange(n)[:, None]
    t = np.arange(n)[None, :]
    ang = 2.0 * np.pi * ((k * t) % n) / n
    s = 1.0 / np.sqrt(n)
    return (np.cos(ang) * s).astype(np.float32), (np.sin(ang) * s).astype(np.float32)


def _mod_kernel(c_ref, w_ref, b_ref, o_ref):
    c = c_ref[...]
    sc = c * _sigmoid(c)
    a_hi, a_lo = _split_bf16(sc)
    w_hi, w_lo = _split_bf16(w_ref[0])
    o_ref[0] = _dot(a_hi, w_hi) + _dot(a_lo, w_hi) + _dot(a_hi, w_lo) + b_ref[0]


def _modulation(cc, w_mod, b_mod):
    depth, d, d6 = w_mod.shape
    tn = 1536
    return pl.pallas_call(
        _mod_kernel,
        grid=(depth, d6 // tn),
        in_specs=[
            pl.BlockSpec((V7X_SUBLANES, d), lambda l, j: (0, 0)),
            pl.BlockSpec((1, d, tn), lambda l, j: (l, 0, j)),
            pl.BlockSpec((1, 1, tn), lambda l, j: (l, 0, j)),
        ],
        out_specs=pl.BlockSpec((1, V7X_SUBLANES, tn), lambda l, j: (l, 0, j)),
        out_shape=jax.ShapeDtypeStruct((depth, V7X_SUBLANES, d6), F32),
        compiler_params=_params(2, 40),
        name="modulation",
    )(cc, w_mod, b_mod.reshape(depth, 1, d6))


def _rope(x, cos, sin):
    lane = lax.broadcasted_iota(jnp.int32, x.shape, 1)
    swapped = jnp.where((lane & 1) == 0,
                        pltpu.roll(x, V7X_LANES - 1, axis=1),
                        pltpu.roll(x, 1, axis=1))
    return x * cos + swapped * sin


def _inproj_kernel(rope, x_ref, modv_ref, g1_ref, w_ref, qkg_ref, bd_ref, cd_ref, *rest):
    if rope:
        cos_ref, sin_ref = rest[:2]
        rest = rest[2:]
    q_ref, k_ref, v_ref, gf_ref, a_ref, u_ref = rest
    x = x_ref[0]
    ms = jnp.mean(x * x, axis=-1, keepdims=True)
    gain = g1_ref[...] * (1.0 + modv_ref[0, 1:2, :])
    xn = (x * lax.rsqrt(ms + EPS)) * gain + modv_ref[0, 0:1, :]
    p = _dot(xn.astype(BF16), w_ref[...])

    qk = p[:, :W_QK]
    sq_hi, sq_lo = _split_bf16(qk * qk)
    ss = _dot(sq_hi, bd_ref[...]) + _dot(sq_lo, bd_ref[...])
    qkn = qk * lax.rsqrt(ss * (1.0 / HEAD_DIM) + EPS) * qkg_ref[...]
    slabs = []
    for j in range(W_QK // V7X_LANES):
        slab = qkn[:, j * V7X_LANES:(j + 1) * V7X_LANES]
        if rope:
            slab = _rope(slab, cos_ref[...], sin_ref[...])
        slabs.append(slab)
    for j in range(W_ATTN // V7X_LANES):
        qt = (slabs[j] * Q_SCALE_LOG2).T.astype(BF16)
        q_ref[0, 2 * j] = qt[:HEAD_DIM]
        q_ref[0, 2 * j + 1] = qt[HEAD_DIM:]
    lane = lax.broadcasted_iota(jnp.int32, slabs[2].shape, 1)
    one_col = jnp.where(lane == HEAD_DIM, 1.0, 0.0)
    k_ref[0, 0] = jnp.where(lane < HEAD_DIM, slabs[2], one_col).astype(BF16)
    k_ref[0, 1] = jnp.where(lane < HEAD_DIM, pltpu.roll(slabs[2], HEAD_DIM, axis=1),
                            one_col).astype(BF16)
    vt = p[:, OFF_V:OFF_F].T.astype(BF16)
    tm = vt.shape[1]
    row = lax.broadcasted_iota(jnp.int32, (V_AUG - HEAD_DIM, tm), 0)
    ones_rows = jnp.where(row == 0, 1.0, 0.0).astype(BF16)
    for g in range(N_KV_HEADS):
        v_ref[0, g, 0:HEAD_DIM, :] = vt[g * HEAD_DIM:(g + 1) * HEAD_DIM]
        v_ref[0, g, HEAD_DIM:, :] = ones_rows
    gf_ref[0] =_dot(p[:, OFF_F:OFF_C].astype(BF16), cd_ref[...]).astype(BF16)
    a_ref[0] = p[:, OFF_C:OFF_P].astype(BF16)
    u_ref[0] = p[:, OFF_P:].astype(BF16)


def _inproj(x, modv, g1, w_in, qk_gain, rope, tm):
    bsz, n, d = x.shape
    nt = n // tm
    tok = lambda width: pl.BlockSpec((1, tm, width), lambda b, i: (b, i, 0))
    in_specs = [
        tok(d),
        pl.BlockSpec((1, 2, d), lambda b, i: (b, 0, 0)),
        _resident((1, d)),
        _resident((d, D_IN)),
        _resident((1, W_QK)),
        _resident((W_QK, W_QK)),
        _resident((W_FOURIER, 2 * W_FOURIER)),
    ]
    args = [x, modv, g1, w_in, qk_gain, _head_sum_matrix(),
            jnp.asarray(_channel_dft()).astype(BF16)]
    if rope:
        cos2, sin2 = _rope_tables(n)
        in_specs += [pl.BlockSpec((tm, V7X_LANES), lambda b, i: (i, 0))] * 2
        args += [cos2, sin2]
    widths = (2 * W_FOURIER, 2 * W_CONV, W_POOL)
    out_specs = [
        pl.BlockSpec((1, N_Q_HEADS, HEAD_DIM, tm), lambda b, i: (b, 0, 0, i)),
        pl.BlockSpec((1, N_KV_HEADS, tm, K_AUG), lambda b, i: (b, 0, i, 0)),
        pl.BlockSpec((1, N_KV_HEADS, V_AUG, tm), lambda b, i: (b, 0, 0, i)),
    ] + [tok(w) for w in widths]
    out_shape = [
        jax.ShapeDtypeStruct((bsz, N_Q_HEADS, HEAD_DIM, n), BF16),
        jax.ShapeDtypeStruct((bsz, N_KV_HEADS, n, K_AUG), BF16),
        jax.ShapeDtypeStruct((bsz, N_KV_HEADS, V_AUG, n), BF16),
    ] + [jax.ShapeDtypeStruct((bsz, n, w), BF16) for w in widths]
    return pl.pallas_call(
        functools.partial(_inproj_kernel, rope),
        grid=(bsz, nt),
        in_specs=in_specs,
        out_specs=out_specs,
        out_shape=out_shape,
        compiler_params=_params(2, 40),
        name="inproj_rope" if rope else "inproj_ctx",
    )(*args)


K_AUG = V7X_LANES
V_AUG = HEAD_DIM + V7X_BF16_ROWS
SAFE_SHIFT_LOG2 = 50.0


def _attn_kernel(chunks, q_ref, *refs):
    n_src = (len(refs) - 3) // 2
    k_refs, v_refs = refs[:n_src], refs[n_src:2 * n_src]
    o_ref, kmax_ref, acc_ref = refs[2 * n_src:]
    n_chunks = len(chunks)
    tq = q_ref.shape[-1]
    i = pl.program_id(2)

    def keys(c):
        src, start, size = chunks[c]
        return k_refs[src][0, 0, start:start + size, :]

    def values(c):
        src, start, size = chunks[c]
        return v_refs[src][0, 0, :, start:start + size]

    @pl.when(i == 0)
    def _():
        ones = jnp.ones((K_AUG, K_AUG), BF16)
        mx = jnp.zeros((1, K_AUG), F32)
        for c in range(n_chunks):
            kf = keys(c).astype(F32)
            norms = _dot((kf * kf).astype(BF16), ones)
            mx = jnp.maximum(mx, jnp.max(norms, axis=0, keepdims=True))
        kmax_ref[...] = jnp.sqrt(mx)

    q2 = jnp.concatenate([q_ref[0, 0], q_ref[0, 1]], axis=1)
    qf = q2.astype(F32)
    kmax = jnp.tile(kmax_ref[...], (1, 2 * tq // K_AUG))
    bound = jnp.sqrt(jnp.sum(qf * qf, axis=0, keepdims=True)) * kmax
    safe = jnp.max(bound) <= SAFE_SHIFT_LOG2
    row = lax.broadcasted_iota(jnp.int32, (K_AUG - HEAD_DIM, 2 * tq), 0)
    shift_rows = jnp.where(row == 0, -bound, 0.0).astype(BF16)
    q_aug = jnp.concatenate([q2, shift_rows], axis=0)

    def logits(c):
        return _dot(keys(c), q_aug)

    @pl.when(safe)
    def _():
        acc = None
        for c in range(n_chunks):
            p = jnp.exp2(logits(c)).astype(BF16)
            part = _dot(values(c), p)
            acc = part if acc is None else acc + part
        acc_ref[...] = acc

    @pl.when(jnp.logical_not(safe))
    def _():
        m = jnp.full((1, 2 * tq), NEG_BIG, F32)
        acc = jnp.zeros((V_AUG, 2 * tq), F32)
        for c in range(n_chunks):
            s = logits(c)
            m_new = jnp.maximum(m, jnp.max(s, axis=0, keepdims=True))
            p = jnp.exp2(s - m_new).astype(BF16)
            acc = jnp.exp2(m - m_new) * acc + _dot(values(c), p)
            m = m_new
        acc_ref[...] = acc

    o = acc_ref[0:HEAD_DIM, :] * (1.0 / acc_ref[HEAD_DIM:HEAD_DIM + 1, :])
    o2 = jnp.concatenate([o[:, :tq], o[:, tq:]], axis=0)
    o_ref[0] = o2.T.astype(BF16)


def _attention(q_t, kv_sources, tq, tk):
    bsz, _, _, nq = q_t.shape
    q_per_kv = N_Q_HEADS // N_KV_HEADS
    chunks = []
    for src, (k_aug, _) in enumerate(kv_sources):
        nk = k_aug.shape[2]
        step = min(tk, nk)
        chunks += [(src, start, step) for start in range(0, nk, step)]
    k_specs = [pl.BlockSpec((1, 1, k.shape[2], K_AUG), lambda b, g, i: (b, g, 0, 0))
               for k, _ in kv_sources]
    v_specs = [pl.BlockSpec((1, 1, V_AUG, v.shape[3]), lambda b, g, i: (b, g, 0, 0))
               for _, v in kv_sources]
    return pl.pallas_call(
        functools.partial(_attn_kernel, tuple(chunks)),
        grid=(bsz, N_KV_HEADS, nq // tq),
        in_specs=[pl.BlockSpec((1, q_per_kv, HEAD_DIM, tq), lambda b, g, i: (b, g, 0, i))]
        + k_specs + v_specs,
        out_specs=pl.BlockSpec((1, tq, q_per_kv * HEAD_DIM), lambda b, g, i: (b, i, g)),
        out_shape=jax.ShapeDtypeStruct((bsz, nq, W_ATTN), BF16),
        scratch_shapes=[pltpu.VMEM((1, K_AUG), F32),
                        pltpu.VMEM((V_AUG, q_per_kv * tq), F32)],
        compiler_params=pltpu.CompilerParams(
            dimension_semantics=("parallel", "parallel", "arbitrary"),
            vmem_limit_bytes=40 * 1024 * 1024),
        name="attention_%dsrc" % len(kv_sources),
    )(q_t, *[k for k, _ in kv_sources], *[v for _, v in kv_sources])


def _dft_a_kernel(bsz, g_ref, tr_ref, ti_ref, o_ref):
    n2 = g_ref.shape[2]
    tab_r = tr_ref[0].astype(BF16)
    tab_i = ti_ref[0].astype(BF16)
    for b in range(bsz):
        res = (_dot(tab_r, g_ref[0, b, :, :W_FOURIER])
               + _dot(tab_i, g_ref[0, b, :, W_FOURIER:]))
        o_ref[0, 0, :, b * W_FOURIER:(b + 1) * W_FOURIER] = res[:n2].astype(BF16)
        o_ref[1, 0, :, b * W_FOURIER:(b + 1) * W_FOURIER] = res[n2:].astype(BF16)


def _dft_c_kernel(a_ref, tc_ref, o_ref):
    o_ref[...] = _dot(tc_ref[...], a_ref[...]).astype(BF16)


def _fourier_latent(gf):
    bsz, n, _ = gf.shape
    n1, n2 = DFT_N1, n // DFT_N1
    tab_r, tab_i, tab_c = _seq_dft_tables(n)
    g_t = jnp.transpose(gf.reshape(bsz, n2, n1, 2 * W_FOURIER), (2, 0, 1, 3))
    lanes = bsz * W_FOURIER
    a = pl.pallas_call(
        functools.partial(_dft_a_kernel, bsz),
        grid=(n1,),
        in_specs=[
            pl.BlockSpec((1, bsz, n2, 2 * W_FOURIER), lambda t: (t, 0, 0, 0)),
            pl.BlockSpec((1, 2 * n2, n2), lambda t: (t, 0, 0)),
            pl.BlockSpec((1, 2 * n2, n2), lambda t: (t, 0, 0)),
        ],
        out_specs=pl.BlockSpec((2, 1, n2, lanes), lambda t: (0, t, 0, 0)),
        out_shape=jax.ShapeDtypeStruct((2, n1, n2, lanes), BF16),
        compiler_params=_params(1, 40),
        name="dft_stage_a",
    )(g_t, tab_r, tab_i)
    a2 = a.reshape(2 * n1, n2 * lanes)
    tl = 4096
    y = pl.pallas_call(
        _dft_c_kernel,
        grid=(n2 * lanes // tl,),
        in_specs=[pl.BlockSpec((2 * n1, tl), lambda j: (0, j)), _resident((n1, 2 * n1))],
        out_specs=pl.BlockSpec((n1, tl), lambda j: (0, j)),
        out_shape=jax.ShapeDtypeStruct((n1, n2 * lanes), BF16),
        compiler_params=_params(1, 40),
        name="dft_stage_c",
    )(a2, jnp.asarray(tab_c).astype(BF16))
    return y.reshape(n, lanes)


def _dft_ctx_kernel(g_ref, c_ref, s_ref, o_ref):
    o_ref[...] = (_dot(c_ref[...], g_ref[0, :, :W_FOURIER])
                  + _dot(s_ref[...], g_ref[0, :, W_FOURIER:])).astype(BF16)


def _fourier_ctx(gf):
    bsz, n, _ = gf.shape
    tab_cos, tab_sin = _ctx_dft_tables(n)
    return pl.pallas_call(
        _dft_ctx_kernel,
        grid=(bsz,),
        in_specs=[pl.BlockSpec((1, n, 2 * W_FOURIER), lambda b: (b, 0, 0)),
                  _resident((n, n)), _resident((n, n))],
        out_specs=pl.BlockSpec((n, W_FOURIER), lambda b: (0, b)),
        out_shape=jax.ShapeDtypeStruct((n, bsz * W_FOURIER), BF16),
        compiler_params=_params(1, 40),
        name="dft_ctx",
    )(gf, jnp.asarray(tab_cos).astype(BF16), jnp.asarray(tab_sin).astype(BF16))


CONV_ROWS = 64
FFN_CHUNKS = ((0, 768), (768, 768), (1536, 768), (2304, 512))


def _conv_pool(tile, n_tiles, n_seq, tm, am_ref, ap_ref, an_ref, um_ref, up_ref, un_ref,
               dww_ref, cvec_ref, g_ref, gsh_ref, pu_ref, psh_ref, cv_ref, pl_ref):
    first = tile == 0
    last = tile == n_tiles - 1

    def glu(a_blk):
        a = a_blk.astype(F32)
        return a[:, :W_CONV] * _sigmoid(a[:, W_CONV:])

    g_ref[0:HALO] = jnp.where(first, 0.0, glu(ap_ref[0]))
    g_ref[HALO:HALO + tm] = glu(am_ref[0])
    g_ref[HALO + tm:] = jnp.where(last, 0.0, glu(an_ref[0]))
    pu_ref[0:HALO] = jnp.where(first, 0.0, up_ref[0].astype(F32))
    pu_ref[HALO:HALO + tm] = um_ref[0].astype(F32)
    pu_ref[HALO + tm:] = jnp.where(last, 0.0, un_ref[0].astype(F32))
    rows_sh = gsh_ref.shape[1]
    for b in range(1, V7X_SUBLANES):
        gsh_ref[b - 1] = g_ref[pl.ds(b, rows_sh), :]
        psh_ref[b - 1] = pu_ref[pl.ds(b, rows_sh), :]

    lane = lax.broadcasted_iota(jnp.int32, (CONV_ROWS, W_POOL), 1)
    grp = lane // POOL_GROUP
    half = jnp.left_shift(1, grp)
    row = lax.broadcasted_iota(jnp.int32, (CONV_ROWS, W_POOL), 0)

    def shifted(base_ref, sh_ref, c0, j):
        src = base_ref if j % V7X_SUBLANES == 0 else sh_ref.at[j % V7X_SUBLANES - 1]
        return src[pl.ds(c0 + (j // V7X_SUBLANES) * V7X_SUBLANES, CONV_ROWS), :]

    for c in range(tm // CONV_ROWS):
        c0 = c * CONV_ROWS
        acc = jnp.broadcast_to(cvec_ref[0:1, :], (CONV_ROWS, W_CONV))
        for k in range(CONV_WIDTH):
            acc = acc + dww_ref[k:k + 1, :] * shifted(g_ref, gsh_ref, c0, k + 1)
        mu = jnp.mean(acc, axis=-1, keepdims=True)
        yc = acc - mu
        var = jnp.mean(yc * yc, axis=-1, keepdims=True)
        yn = yc * lax.rsqrt(var + EPS) * cvec_ref[1:2, :] + cvec_ref[2:3, :]
        act = yn * _sigmoid(yn)
        cv_ref[c0:c0 + CONV_ROWS, :] = act.astype(BF16)
        yield act[0:V7X_SUBLANES, 0:V7X_LANES]

        ld = lambda d: shifted(pu_ref, psh_ref, c0, HALO + d)
        u0 = ld(0)
        s2 = ld(-1) + u0
        s4 = s2 + ld(-2) + ld(1)
        s8 = s4 + ld(-4) + ld(-3) + ld(2) + ld(3)
        s16 = s8 + ld(-8) + ld(-7) + ld(-6) + ld(-5) + ld(4) + ld(5) + ld(6) + ld(7)
        win = jnp.where(grp == 0, s2, jnp.where(grp == 1, s4, jnp.where(grp == 2, s8, s16)))
        t = tile * tm + c0 + row
        cnt = jnp.minimum(t + half, n_seq) - jnp.maximum(t - half, 0)
        centred = win / cnt.astype(F32) - u0
        pl_ref[c0:c0 + CONV_ROWS, :] = centred.astype(BF16)
        yield centred[0:V7X_SUBLANES, 0:V7X_LANES]


def _mixffn_kernel(n_tiles, n_seq, x_ref, attn_ref, yf_ref,
                   amc_ref, apc_ref, anc_ref, umc_ref, upc_ref, unc_ref,
                   amn_ref, apn_ref, ann_ref, umn_ref, upn_ref, unn_ref,
                   mod_ref, g2_ref, wf_ref, dww_ref, cvec_ref, wpw_ref, wpl_ref, wo_ref,
                   wfi_ref, wfo_ref, o_ref, g_ref, gsh_ref, pu_ref, psh_ref, cv_ref, pl_ref):
    tm = x_ref.shape[1]
    s = pl.program_id(0)
    n_steps = pl.num_programs(0)
    scratch = (dww_ref, cvec_ref, g_ref, gsh_ref, pu_ref, psh_ref, cv_ref, pl_ref)

    @pl.when(s == 0)
    def _():
        for _ in _conv_pool(0, n_tiles, n_seq, tm, amc_ref, apc_ref, anc_ref, umc_ref, upc_ref,
                            unc_ref, *scratch):
            pass

    cv = cv_ref[...]
    pooled = pl_ref[...]
    anchors = []

    def tie(value, count):
        mine = [anchors.pop(0) for _ in range(min(count, len(anchors)))]
        if not mine:
            return value
        zero = sum(jnp.minimum(jnp.abs(v), 0.0) for v in mine)
        rows, cols = value.shape
        return value + jnp.tile(zero, (rows // V7X_SUBLANES, cols // V7X_LANES))

    four = _dot(yf_ref[...], wf_ref[...]).astype(BF16)
    conv = _dot(cv, wpw_ref[...]).astype(BF16)
    pool = (_dot(pooled, wpl_ref[...]) * cvec_ref[3:4, :]).astype(BF16)
    mix = (_dot(attn_ref[0], wo_ref[0:W_ATTN, :])
           + _dot(four, wo_ref[W_ATTN:2 * W_ATTN, :])
           + _dot(conv, wo_ref[2 * W_ATTN:3 * W_ATTN, :])
           + _dot(pool, wo_ref[3 * W_ATTN:, :]))
    x1 = x_ref[0] + mod_ref[0, 0:1, :] * mix

    ms = jnp.mean(x1 * x1, axis=-1, keepdims=True)
    gain = g2_ref[...] * (1.0 + mod_ref[0, 2:3, :])
    xn_f32 = (x1 * lax.rsqrt(ms + EPS)) * gain + mod_ref[0, 1:2, :]
    xn = xn_f32.astype(BF16)

    nxt_tile = jnp.minimum(s + 1, n_steps - 1) % n_tiles
    anchors += list(_conv_pool(nxt_tile, n_tiles, n_seq, tm, amn_ref, apn_ref, ann_ref, umn_ref,
                               upn_ref, unn_ref, *scratch))
    acc = None
    for c, (lo, size) in enumerate(FFN_CHUNKS):
        xn_c = xn if c == 0 else tie(xn_f32, 2).astype(BF16)
        a = _dot(xn_c, wfi_ref[:, lo:lo + size])
        g = _dot(xn, wfi_ref[:, D_FF + lo:D_FF + lo + size])
        h = tie((a * _sigmoid(a)) * g, 2 if c == 0 else 4)
        part = _dot(h.astype(BF16), wfo_ref[lo:lo + size, :])
        acc = part if acc is None else acc + part
    assert not anchors
    o_ref[0] = x1 + mod_ref[0, 3:4, :] * acc


def _mix_ffn(x, attn, yf, a_conv, u_pool, mod4, g2, w_f, dw_w, cvec, w_pw, w_pool_bd, w_o,
             w_fi, w_fo, tm):
    bsz, n, d = x.shape
    nt = n // tm
    n_steps = bsz * nt
    hb = tm // HALO
    last_halo = n // HALO - 1

    def nxt(s):
        s1 = jnp.minimum(s + 1, n_steps - 1)
        return s1 // nt, s1 % nt

    tok = lambda width: pl.BlockSpec((1, tm, width), lambda s: (s // nt, s % nt, 0))
    cur_main = lambda w: pl.BlockSpec((1, tm, w), lambda s: (0, 0, 0))
    cur_prev = lambda w: pl.BlockSpec((1, HALO, w), lambda s: (0, 0, 0))
    cur_next = lambda w: pl.BlockSpec((1, HALO, w), lambda s: (0, min(hb, last_halo), 0))
    nxt_main = lambda w: pl.BlockSpec((1, tm, w), lambda s: (nxt(s)[0], nxt(s)[1], 0))
    nxt_prev = lambda w: pl.BlockSpec(
        (1, HALO, w), lambda s: (nxt(s)[0], jnp.maximum(nxt(s)[1] * hb - 1, 0), 0))
    nxt_next = lambda w: pl.BlockSpec(
        (1, HALO, w), lambda s: (nxt(s)[0], jnp.minimum((nxt(s)[1] + 1) * hb, last_halo), 0))
    rows_sh = tm + 2 * HALO - V7X_SUBLANES
    return pl.pallas_call(
        functools.partial(_mixffn_kernel, nt, n),
        grid=(n_steps,),
        in_specs=[
            tok(d), tok(W_ATTN),
            pl.BlockSpec((tm, W_FOURIER), lambda s: (s % nt, s // nt)),
            cur_main(2 * W_CONV), cur_prev(2 * W_CONV), cur_next(2 * W_CONV),
            cur_main(W_POOL), cur_prev(W_POOL), cur_next(W_POOL),
            nxt_main(2 * W_CONV), nxt_prev(2 * W_CONV), nxt_next(2 * W_CONV),
            nxt_main(W_POOL), nxt_prev(W_POOL), nxt_next(W_POOL),
            pl.BlockSpec((1, 4, d), lambda s: (s // nt, 0, 0)),
            _resident((1, d)),
            _resident((W_FOURIER, W_FOURIER)),
            _resident((CONV_WIDTH, W_CONV)),
            _resident((V7X_SUBLANES, W_CONV)),
            _resident((W_CONV, W_CONV)),
            _resident((W_POOL, W_POOL)),
            _resident((4 * W_ATTN, d)),
            _resident((d, 2 * D_FF)),
            _resident((D_FF, d)),
        ],
        out_specs=tok(d),
        out_shape=jax.ShapeDtypeStruct((bsz, n, d), F32),
        scratch_shapes=[
            pltpu.VMEM((tm + 2 * HALO, W_CONV), F32),
            pltpu.VMEM((V7X_SUBLANES - 1, rows_sh, W_CONV), F32),
            pltpu.VMEM((tm + 2 * HALO, W_POOL), F32),
            pltpu.VMEM((V7X_SUBLANES - 1, rows_sh, W_POOL), F32),
            pltpu.VMEM((tm, W_CONV), BF16),
            pltpu.VMEM((tm, W_POOL), BF16),
        ],
        compiler_params=pltpu.CompilerParams(
            dimension_semantics=("arbitrary",), vmem_limit_bytes=56 * 1024 * 1024),
        name="mix_ffn_n%d" % n,
    )(x, attn, yf, a_conv, a_conv, a_conv, u_pool, u_pool, u_pool,
      a_conv, a_conv, a_conv, u_pool, u_pool, u_pool, mod4, g2,
      w_f, dw_w, cvec, w_pw, w_pool_bd, w_o, w_fi, w_fo)


def _pool_block_diag(w_pool_l):
    groups = w_pool_l.shape[0]
    out = jnp.zeros((W_POOL, W_POOL), w_pool_l.dtype)
    for g in range(groups):
        sl = slice(g * POOL_GROUP, (g + 1) * POOL_GROUP)
        out = out.at[sl, sl].set(w_pool_l[g])
    return out


def kernel(x, c, ctx, c_ctx, w_mod, b_mod, g_norm1, g_norm2, w_in, q_norm_g, k_norm_g,
           w_fourier, conv_dw_w, conv_dw_b, conv_ln_g, conv_ln_b, w_conv_pw, w_pool,
           pool_scale, w_out, w_ffn_in, w_ffn_out):
    bsz, n, d = x.shape
    n_ctx = ctx.shape[1]
    tm_lat, tm_ctx = 512, n_ctx
    tq_lat, tk_lat = 256, 1024

    cc = jnp.zeros((V7X_SUBLANES, d), F32).at[:bsz].set(c).at[bsz].set(c_ctx)
    mod_all = _modulation(cc, w_mod, b_mod)

    h = ctx
    for l in range(DEPTH):
        last = l == DEPTH - 1
        m6 = mod_all[l].reshape(V7X_SUBLANES, 6, d)
        lat = m6[:bsz]
        cx = jnp.broadcast_to(m6[bsz:bsz + 1], (bsz, 6, d))
        w_in_l = w_in[l].astype(BF16)
        qk_gain = jnp.concatenate([jnp.tile(q_norm_g[l], N_Q_HEADS),
                                   jnp.tile(k_norm_g[l], N_KV_HEADS)])[None, :]
        g1 = g_norm1[l][None, :]
        g2 = g_norm2[l][None, :]
        w_f = w_fourier[l].astype(BF16)
        w_pw = w_conv_pw[l].astype(BF16)
        w_pl = _pool_block_diag(w_pool[l]).astype(BF16)
        w_o = w_out[l].astype(BF16)
        w_fi = w_ffn_in[l].astype(BF16)
        w_fo = w_ffn_out[l].astype(BF16)
        cvec = jnp.zeros((V7X_SUBLANES, W_CONV), F32)
        cvec = cvec.at[0].set(conv_dw_b[l]).at[1].set(conv_ln_g[l])
        cvec = cvec.at[2].set(conv_ln_b[l]).at[3].set(pool_scale[l])

        qc, kc, vc, gfc, ac, uc = _inproj(h, cx[:, 0:2], g1, w_in_l, qk_gain, False, tm_ctx)
        q, k, v, gf, a, u = _inproj(x, lat[:, 0:2], g1, w_in_l, qk_gain, True, tm_lat)

        attn = _attention(q, [(kc, vc), (k, v)], tq_lat, tk_lat)
        yf = _fourier_latent(gf)
        x_new = _mix_ffn(x, attn, yf, a, u, lat[:, 2:6], g2, w_f, conv_dw_w[l], cvec, w_pw, w_pl,
                         w_o, w_fi, w_fo, tm_lat)

        if not last:
            attn_c = _attention(qc, [(kc, vc)], n_ctx, n_ctx)
            yfc = _fourier_ctx(gfc)
            h = _mix_ffn(h, attn_c, yfc, ac, uc, cx[:, 2:6], g2, w_f, conv_dw_w[l], cvec, w_pw,
                         w_pl, w_o, w_fi, w_fo, tm_ctx)
        x = x_new
    return x
```

```python
import functools

import numpy as np
import jax
import jax.numpy as jnp
from jax import lax
from jax.experimental import pallas as pl
from jax.experimental.pallas import tpu as pltpu

F32 = jnp.float32
BF16 = jnp.bfloat16

D_MODEL = 1024
DEPTH = 2
GRID_W = 64
HEAD_DIM = 64
N_Q_HEADS = 4
N_KV_HEADS = 2
W_ATTN = 256
W_FOURIER = 256
W_CONV = 256
W_POOL = 256
CONV_WIDTH = 31
POOL_GROUP = 64
D_FF = 2816
ROPE_THETA = 10000.0
EPS = 1e-6
ATTN_SCALE = HEAD_DIM ** -0.5
Q_SCALE_LOG2 = ATTN_SCALE * float(np.log2(np.e))
W_QK = W_ATTN + N_KV_HEADS * HEAD_DIM
OFF_V = W_QK
OFF_F = OFF_V + N_KV_HEADS * HEAD_DIM
OFF_C = OFF_F + W_FOURIER
OFF_P = OFF_C + 2 * W_CONV
D_IN = OFF_P + W_POOL

V7X_LANES = 128
V7X_SUBLANES = 8
V7X_BF16_ROWS = 16
V7X_VMEM_BYTES = 64 * 1024 * 1024

HALO = V7X_BF16_ROWS
DFT_N1 = 64
NEG_BIG = -1e30


def _dot(a, b):
    return jnp.dot(a, b, preferred_element_type=F32)


def _sigmoid(x):
    return 1.0 / (1.0 + jnp.exp(-x))


def _split_bf16(x):
    hi = x.astype(BF16)
    lo = (x - hi.astype(F32)).astype(BF16)
    return hi, lo


def _resident(shape):
    nd = len(shape)
    return pl.BlockSpec(shape, lambda *_: (0,) * nd, pipeline_mode=pl.Buffered(1))


def _params(n_grid, vmem_mb):
    return pltpu.CompilerParams(
        dimension_semantics=("parallel",) * n_grid,
        vmem_limit_bytes=vmem_mb * 1024 * 1024)


@functools.lru_cache(maxsize=None)
def _rope_tables(n):
    rows = n // GRID_W
    row = np.repeat(np.arange(rows, dtype=np.float64), GRID_W)
    col = np.tile(np.arange(GRID_W, dtype=np.float64), rows)
    n_freq = HEAD_DIM // 4
    inv_freq = ROPE_THETA ** (-np.arange(n_freq, dtype=np.float64) / n_freq)
    ang = np.concatenate([row[:, None] * inv_freq, col[:, None] * inv_freq], axis=-1)
    cos = np.repeat(np.cos(ang), 2, axis=1)
    sin = np.repeat(np.sin(ang), 2, axis=1)
    sign = np.tile(np.array([-1.0, 1.0]), HEAD_DIM // 2)
    cos2 = np.tile(cos, (1, 2)).astype(np.float32)
    sin2 = np.tile(sin * sign, (1, 2)).astype(np.float32)
    return cos2, sin2


@functools.lru_cache(maxsize=None)
def _head_sum_matrix():
    idx = np.arange(W_QK) // HEAD_DIM
    return np.asarray((idx[:, None] == idx[None, :]).astype(np.float32), dtype=BF16)


@functools.lru_cache(maxsize=None)
def _channel_dft():
    c = np.arange(W_FOURIER)
    same = (c[:, None] // HEAD_DIM) == (c[None, :] // HEAD_DIM)
    ang = 2.0 * np.pi * ((c[:, None] % HEAD_DIM) * (c[None, :] % HEAD_DIM)) / HEAD_DIM
    cr = np.where(same, np.cos(ang), 0.0) / 8.0
    ci = np.where(same, -np.sin(ang), 0.0) / 8.0
    return np.concatenate([cr, ci], axis=1).astype(np.float32)


@functools.lru_cache(maxsize=None)
def _seq_dft_tables(n):
    n1, n2 = DFT_N1, n // DFT_N1
    t1 = np.arange(n1)[:, None, None]
    k2 = np.arange(n2)[None, :, None]
    t2 = np.arange(n2)[None, None, :]
    theta = 2.0 * np.pi * ((k2 * (t1 + n1 * t2)) % n) / n
    dr = np.cos(theta) / np.sqrt(n2)
    di = -np.sin(theta) / np.sqrt(n2)
    tab_r = np.concatenate([dr, di], axis=1)
    tab_i = np.concatenate([-di, dr], axis=1)
    k1 = np.arange(n1)[:, None]
    tt = np.arange(n1)[None, :]
    phi = 2.0 * np.pi * ((k1 * tt) % n1) / n1
    tab_c = np.concatenate([np.cos(phi), np.sin(phi)], axis=1) / np.sqrt(n1)
    return tab_r.astype(np.float32), tab_i.astype(np.float32), tab_c.astype(np.float32)


@functools.lru_cache(maxsize=None)
def _ctx_dft_tables(n):
    k = np.arange(n)[:, None]
    t = np.arange(n)[None, :]
    ang = 2.0 * np.pi * ((k * t) % n) / n
    s = 1.0 / np.sqrt(n)
    return (np.cos(ang) * s).astype(np.float32), (np.sin(ang) * s).astype(np.float32)


def _mod_kernel(c_ref, w_ref, b_ref, o_ref):
    c = c_ref[...]
    sc = c * _sigmoid(c)
    a_hi, a_lo = _split_bf16(sc)
    w_hi, w_lo = _split_bf16(w_ref[0])
    o_ref[0] = _dot(a_hi, w_hi) + _dot(a_lo, w_hi) + _dot(a_hi, w_lo) + b_ref[0]


def _modulation(cc, w_mod, b_mod):
    depth, d, d6 = w_mod.shape
    tn = 1536
    return pl.pallas_call(
        _mod_kernel,
        grid=(depth, d6 // tn),
        in_specs=[
            pl.BlockSpec((V7X_SUBLANES, d), lambda l, j: (0, 0)),
            pl.BlockSpec((1, d, tn), lambda l, j: (l, 0, j)),
            pl.BlockSpec((1, 1, tn), lambda l, j: (l, 0, j)),
        ],
        out_specs=pl.BlockSpec((1, V7X_SUBLANES, tn), lambda l, j: (l, 0, j)),
        out_shape=jax.ShapeDtypeStruct((depth, V7X_SUBLANES, d6), F32),
        compiler_params=_params(2, 40),
        name="modulation",
    )(cc, w_mod, b_mod.reshape(depth, 1, d6))


def _rope(x, cos, sin):
    lane = lax.broadcasted_iota(jnp.int32, x.shape, 1)
    swapped = jnp.where((lane & 1) == 0,
                        pltpu.roll(x, V7X_LANES - 1, axis=1),
                        pltpu.roll(x, 1, axis=1))
    return x * cos + swapped * sin


def _inproj_kernel(rope, x_ref, modv_ref, g1_ref, w_ref, qkg_ref, bd_ref, cd_ref, *rest):
    if rope:
        cos_ref, sin_ref = rest[:2]
        rest = rest[2:]
    q_ref, k_ref, v_ref, gf_ref, a_ref, u_ref = rest
    x = x_ref[0]
    ms = jnp.mean(x * x, axis=-1, keepdims=True)
    gain = g1_ref[...] * (1.0 + modv_ref[0, 1:2, :])
    xn = (x * lax.rsqrt(ms + EPS)) * gain + modv_ref[0, 0:1, :]
    p = _dot(xn.astype(BF16), w_ref[...])

    qk = p[:, :W_QK]
    sq_hi, sq_lo = _split_bf16(qk * qk)
    ss = _dot(sq_hi, bd_ref[...]) + _dot(sq_lo, bd_ref[...])
    qkn = qk * lax.rsqrt(ss * (1.0 / HEAD_DIM) + EPS) * qkg_ref[...]
    slabs = []
    for j in range(W_QK // V7X_LANES):
        slab = qkn[:, j * V7X_LANES:(j + 1) * V7X_LANES]
        if rope:
            slab = _rope(slab, cos_ref[...], sin_ref[...])
        slabs.append(slab)
    for j in range(W_ATTN // V7X_LANES):
        qt = (slabs[j] * Q_SCALE_LOG2).T.astype(BF16)
        q_ref[0, 2 * j] = qt[:HEAD_DIM]
        q_ref[0, 2 * j + 1] = qt[HEAD_DIM:]
    lane = lax.broadcasted_iota(jnp.int32, slabs[2].shape, 1)
    one_col = jnp.where(lane == HEAD_DIM, 1.0, 0.0)
    k_ref[0, 0] = jnp.where(lane < HEAD_DIM, slabs[2], one_col).astype(BF16)
    k_ref[0, 1] = jnp.where(lane < HEAD_DIM, pltpu.roll(slabs[2], HEAD_DIM, axis=1),
                            one_col).astype(BF16)
    vt = p[:, OFF_V:OFF_F].T.astype(BF16)
    tm = vt.shape[1]
    row = lax.broadcasted_iota(jnp.int32, (V_AUG - HEAD_DIM, tm), 0)
    ones_rows = jnp.where(row == 0, 1.0, 0.0).astype(BF16)
    for g in range(N_KV_HEADS):
        v_ref[0, g, 0:HEAD_DIM, :] = vt[g * HEAD_DIM:(g + 1) * HEAD_DIM]
        v_ref[0, g, HEAD_DIM:, :] = ones_rows
    gf_ref[0] =_dot(p[:, OFF_F:OFF_C].astype(BF16), cd_ref[...]).astype(BF16)
    a_ref[0] = p[:, OFF_C:OFF_P].astype(BF16)
    u_ref[0] = p[:, OFF_P:].astype(BF16)


def _inproj(x, modv, g1, w_in, qk_gain, rope, tm):
    bsz, n, d = x.shape
    nt = n // tm
    tok = lambda width: pl.BlockSpec((1, tm, width), lambda b, i: (b, i, 0))
    in_specs = [
        tok(d),
        pl.BlockSpec((1, 2, d), lambda b, i: (b, 0, 0)),
        _resident((1, d)),
        _resident((d, D_IN)),
        _resident((1, W_QK)),
        _resident((W_QK, W_QK)),
        _resident((W_FOURIER, 2 * W_FOURIER)),
    ]
    args = [x, modv, g1, w_in, qk_gain, _head_sum_matrix(),
            jnp.asarray(_channel_dft()).astype(BF16)]
    if rope:
        cos2, sin2 = _rope_tables(n)
        in_specs += [pl.BlockSpec((tm, V7X_LANES), lambda b, i: (i, 0))] * 2
        args += [cos2, sin2]
    widths = (2 * W_FOURIER, 2 * W_CONV, W_POOL)
    out_specs = [
        pl.BlockSpec((1, N_Q_HEADS, HEAD_DIM, tm), lambda b, i: (b, 0, 0, i)),
        pl.BlockSpec((1, N_KV_HEADS, tm, K_AUG), lambda b, i: (b, 0, i, 0)),
        pl.BlockSpec((1, N_KV_HEADS, V_AUG, tm), lambda b, i: (b, 0, 0, i)),
    ] + [tok(w) for w in widths]
    out_shape = [
        jax.ShapeDtypeStruct((bsz, N_Q_HEADS, HEAD_DIM, n), BF16),
        jax.ShapeDtypeStruct((bsz, N_KV_HEADS, n, K_AUG), BF16),
        jax.ShapeDtypeStruct((bsz, N_KV_HEADS, V_AUG, n), BF16),
    ] + [jax.ShapeDtypeStruct((bsz, n, w), BF16) for w in widths]
    return pl.pallas_call(
        functools.partial(_inproj_kernel, rope),
        grid=(bsz, nt),
        in_specs=in_specs,
        out_specs=out_specs,
        out_shape=out_shape,
        compiler_params=_params(2, 40),
        name="inproj_rope" if rope else "inproj_ctx",
    )(*args)


K_AUG = V7X_LANES
V_AUG = HEAD_DIM + V7X_BF16_ROWS
SAFE_SHIFT_LOG2 = 50.0


def _attn_kernel(chunks, q_ref, *refs):
    n_src = (len(refs) - 3) // 2
    k_refs, v_refs = refs[:n_src], refs[n_src:2 * n_src]
    o_ref, kmax_ref, acc_ref = refs[2 * n_src:]
    n_chunks = len(chunks)
    tq = q_ref.shape[-1]
    i = pl.program_id(2)

    def keys(c):
        src, start, size = chunks[c]
        return k_refs[src][0, 0, start:start + size, :]

    def values(c):
        src, start, size = chunks[c]
        return v_refs[src][0, 0, :, start:start + size]

    @pl.when(i == 0)
    def _():
        ones = jnp.ones((K_AUG, K_AUG), BF16)
        mx = jnp.zeros((1, K_AUG), F32)
        for c in range(n_chunks):
            kf = keys(c).astype(F32)
            norms = _dot((kf * kf).astype(BF16), ones)
            mx = jnp.maximum(mx, jnp.max(norms, axis=0, keepdims=True))
        kmax_ref[...] = jnp.sqrt(mx)

    q2 = jnp.concatenate([q_ref[0, 0], q_ref[0, 1]], axis=1)
    qf = q2.astype(F32)
    kmax = jnp.tile(kmax_ref[...], (1, 2 * tq // K_AUG))
    bound = jnp.sqrt(jnp.sum(qf * qf, axis=0, keepdims=True)) * kmax
    safe = jnp.max(bound) <= SAFE_SHIFT_LOG2
    row = lax.broadcasted_iota(jnp.int32, (K_AUG - HEAD_DIM, 2 * tq), 0)
    shift_rows = jnp.where(row == 0, -bound, 0.0).astype(BF16)
    q_aug = jnp.concatenate([q2, shift_rows], axis=0)

    def logits(c):
        return _dot(keys(c), q_aug)

    @pl.when(safe)
    def _():
        acc = None
        for c in range(n_chunks):
            p = jnp.exp2(logits(c)).astype(BF16)
            part = _dot(values(c), p)
            acc = part if acc is None else acc + part
        acc_ref[...] = acc

    @pl.when(jnp.logical_not(safe))
    def _():
        m = jnp.full((1, 2 * tq), NEG_BIG, F32)
        acc = jnp.zeros((V_AUG, 2 * tq), F32)
        for c in range(n_chunks):
            s = logits(c)
            m_new = jnp.maximum(m, jnp.max(s, axis=0, keepdims=True))
            p = jnp.exp2(s - m_new).astype(BF16)
            acc = jnp.exp2(m - m_new) * acc + _dot(values(c), p)
            m = m_new
        acc_ref[...] = acc

    o = acc_ref[0:HEAD_DIM, :] * (1.0 / acc_ref[HEAD_DIM:HEAD_DIM + 1, :])
    o2 = jnp.concatenate([o[:, :tq], o[:, tq:]], axis=0)
    o_ref[0] = o2.T.astype(BF16)


def _attention(q_t, kv_sources, tq, tk):
    bsz, _, _, nq = q_t.shape
    q_per_kv = N_Q_HEADS // N_KV_HEADS
    chunks = []
    for src, (k_aug, _) in enumerate(kv_sources):
        nk = k_aug.shape[2]
        step = min(tk, nk)
        chunks += [(src, start, step) for start in range(0, nk, step)]
    k_specs = [pl.BlockSpec((1, 1, k.shape[2], K_AUG), lambda b, g, i: (b, g, 0, 0))
               for k, _ in kv_sources]
    v_specs = [pl.BlockSpec((1, 1, V_AUG, v.shape[3]), lambda b, g, i: (b, g, 0, 0))
               for _, v in kv_sources]
    return pl.pallas_call(
        functools.partial(_attn_kernel, tuple(chunks)),
        grid=(bsz, N_KV_HEADS, nq // tq),
        in_specs=[pl.BlockSpec((1, q_per_kv, HEAD_DIM, tq), lambda b, g, i: (b, g, 0, i))]
        + k_specs + v_specs,
        out_specs=pl.BlockSpec((1, tq, q_per_kv * HEAD_DIM), lambda b, g, i: (b, i, g)),
        out_shape=jax.ShapeDtypeStruct((bsz, nq, W_ATTN), BF16),
        scratch_shapes=[pltpu.VMEM((1, K_AUG), F32),
                        pltpu.VMEM((V_AUG, q_per_kv * tq), F32)],
        compiler_params=pltpu.CompilerParams(
            dimension_semantics=("parallel", "parallel", "arbitrary"),
            vmem_limit_bytes=40 * 1024 * 1024),
        name="attention_%dsrc" % len(kv_sources),
    )(q_t, *[k for k, _ in kv_sources], *[v for _, v in kv_sources])


DFT_A_GROUP = 8


def _dft_a_kernel(bsz, g_ref, tr_ref, ti_ref, o_ref):
    n2 = g_ref.shape[2]
    for j in range(DFT_A_GROUP):
        tab_r = tr_ref[j].astype(BF16)
        tab_i = ti_ref[j].astype(BF16)
        for b in range(bsz):
            res = (_dot(tab_r, g_ref[j, b, :, :W_FOURIER])
                   + _dot(tab_i, g_ref[j, b, :, W_FOURIER:]))
            o_ref[0, j, :, b * W_FOURIER:(b + 1) * W_FOURIER] = res[:n2].astype(BF16)
            o_ref[1, j, :, b * W_FOURIER:(b + 1) * W_FOURIER] = res[n2:].astype(BF16)


def _dft_c_kernel(a_ref, tc_ref, o_ref):
    o_ref[...] = _dot(tc_ref[...], a_ref[...]).astype(BF16)


def _fourier_latent(gf):
    bsz, n, _ = gf.shape
    n1, n2 = DFT_N1, n // DFT_N1
    tab_r, tab_i, tab_c = _seq_dft_tables(n)
    g_t = jnp.transpose(gf.reshape(bsz, n2, n1, 2 * W_FOURIER), (2, 0, 1, 3))
    lanes = bsz * W_FOURIER
    a = pl.pallas_call(
        functools.partial(_dft_a_kernel, bsz),
        grid=(n1 // DFT_A_GROUP,),
        in_specs=[
            pl.BlockSpec((DFT_A_GROUP, bsz, n2, 2 * W_FOURIER), lambda t: (t, 0, 0, 0)),
            pl.BlockSpec((DFT_A_GROUP, 2 * n2, n2), lambda t: (t, 0, 0)),
            pl.BlockSpec((DFT_A_GROUP, 2 * n2, n2), lambda t: (t, 0, 0)),
        ],
        out_specs=pl.BlockSpec((2, DFT_A_GROUP, n2, lanes), lambda t: (0, t, 0, 0)),
        out_shape=jax.ShapeDtypeStruct((2, n1, n2, lanes), BF16),
        compiler_params=_params(1, 40),
        name="dft_stage_a",
    )(g_t, tab_r, tab_i)
    a2 = a.reshape(2 * n1, n2 * lanes)
    tl = 4096
    y = pl.pallas_call(
        _dft_c_kernel,
        grid=(n2 * lanes // tl,),
        in_specs=[pl.BlockSpec((2 * n1, tl), lambda j: (0, j)), _resident((n1, 2 * n1))],
        out_specs=pl.BlockSpec((n1, tl), lambda j: (0, j)),
        out_shape=jax.ShapeDtypeStruct((n1, n2 * lanes), BF16),
        compiler_params=_params(1, 40),
        name="dft_stage_c",
    )(a2, jnp.asarray(tab_c).astype(BF16))
    return y.reshape(n, lanes)


def _dft_ctx_kernel(g_ref, c_ref, s_ref, o_ref):
    o_ref[...] = (_dot(c_ref[...], g_ref[0, :, :W_FOURIER])
                  + _dot(s_ref[...], g_ref[0, :, W_FOURIER:])).astype(BF16)


def _fourier_ctx(gf):
    bsz, n, _ = gf.shape
    tab_cos, tab_sin = _ctx_dft_tables(n)
    return pl.pallas_call(
        _dft_ctx_kernel,
        grid=(bsz,),
        in_specs=[pl.BlockSpec((1, n, 2 * W_FOURIER), lambda b: (b, 0, 0)),
                  _resident((n, n)), _resident((n, n))],
        out_specs=pl.BlockSpec((n, W_FOURIER), lambda b: (0, b)),
        out_shape=jax.ShapeDtypeStruct((n, bsz * W_FOURIER), BF16),
        compiler_params=_params(1, 40),
        name="dft_ctx",
    )(gf, jnp.asarray(tab_cos).astype(BF16), jnp.asarray(tab_sin).astype(BF16))


CONV_ROWS = 64
CONV_TAP_ROWS = 32
FFN_CHUNKS = ((0, 768), (768, 768), (1536, 768), (2304, 512))


def _exact_zero(v):
    return jnp.minimum(jnp.abs(v), 0.0)


def _conv_pool_units(tile, n_tiles, n_seq, tm, am_ref, ap_ref, an_ref, um_ref, up_ref, un_ref,
                     dww_ref, cvec_ref, g_ref, gsh_ref, pu_ref, psh_ref, cv_ref, pl_ref):
    first = tile == 0
    last = tile == n_tiles - 1
    rows_sh = gsh_ref.shape[1]
    corner = lambda v: v[0:V7X_SUBLANES, 0:V7X_LANES]

    def gated(v, gate):
        if gate is None:
            return v
        rows, cols = v.shape
        return v + jnp.tile(gate, (rows // V7X_SUBLANES, cols // V7X_LANES))

    def glu(a_blk):
        a = a_blk.astype(F32)
        return a[:, :W_CONV] * _sigmoid(a[:, W_CONV:])

    def stage(gate):
        main = gated(glu(am_ref[0]), gate)
        g_ref[0:HALO] = jnp.where(first, 0.0, glu(ap_ref[0]))
        g_ref[HALO:HALO + tm] = main
        g_ref[HALO + tm:] = jnp.where(last, 0.0, glu(an_ref[0]))
        pu_ref[0:HALO] = jnp.where(first, 0.0, up_ref[0].astype(F32))
        pu_ref[HALO:HALO + tm] = um_ref[0].astype(F32)
        pu_ref[HALO + tm:] = jnp.where(last, 0.0, un_ref[0].astype(F32))
        return corner(main)

    def copy(b):
        def run(gate):
            gv = gated(g_ref[pl.ds(b, rows_sh), :], gate)
            gsh_ref[b - 1] = gv
            psh_ref[b - 1] = gated(pu_ref[pl.ds(b, rows_sh), :], gate)
            return corner(gv)
        return run

    lane = lax.broadcasted_iota(jnp.int32, (CONV_ROWS, W_POOL), 1)
    grp = lane // POOL_GROUP
    half = jnp.left_shift(1, grp)
    row = lax.broadcasted_iota(jnp.int32, (CONV_ROWS, W_POOL), 0)

    def shifted(base_ref, sh_ref, c0, j):
        src = base_ref if j % V7X_SUBLANES == 0 else sh_ref.at[j % V7X_SUBLANES - 1]
        return src[pl.ds(c0 + (j // V7X_SUBLANES) * V7X_SUBLANES, CONV_ROWS), :]

    def conv(c0):
        def run(gate):
            wts = gated(dww_ref[...], gate)
            acc = jnp.broadcast_to(cvec_ref[0:1, :], (CONV_ROWS, W_CONV))
            for k in range(CONV_WIDTH):
                acc = acc + wts[k:k + 1, :] * shifted(g_ref, gsh_ref, c0, k + 1)
            mu = jnp.mean(acc, axis=-1, keepdims=True)
            yc = acc - mu
            var = jnp.mean(yc * yc, axis=-1, keepdims=True)
            yn = yc * lax.rsqrt(var + EPS) * cvec_ref[1:2, :] + cvec_ref[2:3, :]
            act = yn * _sigmoid(yn)
            cv_ref[c0:c0 + CONV_ROWS, :] = act.astype(BF16)
            return corner(act)
        return run

    def pool(c0):
        def run(gate):
            ld = lambda d: shifted(pu_ref, psh_ref, c0, HALO + d)
            u0 = ld(0)
            s2 = gated(ld(-1) + u0, gate)
            s4 = s2 + ld(-2) + ld(1)
            s8 = s4 + ld(-4) + ld(-3) + ld(2) + ld(3)
            s16 = s8 + ld(-8) + ld(-7) + ld(-6) + ld(-5) + ld(4) + ld(5) + ld(6) + ld(7)
            win = jnp.where(grp == 0, s2, jnp.where(grp == 1, s4, jnp.where(grp == 2, s8, s16)))
            t = tile * tm + c0 + row
            cnt = jnp.minimum(t + half, n_seq) - jnp.maximum(t - half, 0)
            centred = win / cnt.astype(F32) - u0
            pl_ref[c0:c0 + CONV_ROWS, :] = centred.astype(BF16)
            return corner(centred)
        return run

    units = [stage] + [copy(b) for b in range(1, V7X_SUBLANES)]
    for c in range(tm // CONV_ROWS):
        units += [conv(c * CONV_ROWS), pool(c * CONV_ROWS)]
    return units


def _mixffn_kernel(n_tiles, n_seq, x_ref, attn_ref, yf_ref,
                   amc_ref, apc_ref, anc_ref, umc_ref, upc_ref, unc_ref,
                   amn_ref, apn_ref, ann_ref, umn_ref, upn_ref, unn_ref,
                   mod_ref, g2_ref, wf_ref, dww_ref, cvec_ref, wpw_ref, wpl_ref, wo_ref,
                   wfi_ref, wfo_ref, o_ref, g_ref, gsh_ref, pu_ref, psh_ref, cv_ref, pl_ref):
    tm = x_ref.shape[1]
    s = pl.program_id(0)
    n_steps = pl.num_programs(0)
    scratch = (dww_ref, cvec_ref, g_ref, gsh_ref, pu_ref, psh_ref, cv_ref, pl_ref)

    @pl.when(s == 0)
    def _():
        for unit in _conv_pool_units(0, n_tiles, n_seq, tm, amc_ref, apc_ref, anc_ref, umc_ref,
                                     upc_ref, unc_ref, *scratch):
            unit(None)

    cv = cv_ref[...]
    pooled = pl_ref[...]
    nxt_tile = jnp.minimum(s + 1, n_steps - 1) % n_tiles
    units = _conv_pool_units(nxt_tile, n_tiles, n_seq, tm, amn_ref, apn_ref, ann_ref, umn_ref,
                             upn_ref, unn_ref, *scratch)
    n_pre = V7X_SUBLANES
    slots = [units[:n_pre // 2], units[n_pre // 2:n_pre]]
    slots += [units[i:i + 2] for i in range(n_pre, len(units), 2)]

    def run_slot(gate_src):
        if not slots:
            return []
        gate = None if gate_src is None else _exact_zero(
            gate_src[-V7X_SUBLANES:, -V7X_LANES:])
        return [unit(gate) for unit in slots.pop(0)]

    def tied(lhs_f32, tokens):
        lhs = lhs_f32.astype(BF16)
        if not tokens:
            return lhs
        zero = sum(_exact_zero(t) for t in tokens)
        head = lhs_f32[0:V7X_BF16_ROWS] + jnp.tile(
            zero, (V7X_BF16_ROWS // V7X_SUBLANES, lhs_f32.shape[1] // V7X_LANES))
        return jnp.concatenate([head.astype(BF16), lhs[V7X_BF16_ROWS:]], axis=0)

    tokens = run_slot(None)
    four = _dot(yf_ref[...], wf_ref[...]).astype(BF16)
    conv = _dot(cv, wpw_ref[...]).astype(BF16)
    pool = (_dot(pooled, wpl_ref[...]) * cvec_ref[3:4, :]).astype(BF16)
    mix = (_dot(attn_ref[0], wo_ref[0:W_ATTN, :])
           + _dot(four, wo_ref[W_ATTN:2 * W_ATTN, :])
           + _dot(conv, wo_ref[2 * W_ATTN:3 * W_ATTN, :])
           + _dot(pool, wo_ref[3 * W_ATTN:, :]))
    x1 = x_ref[0] + mod_ref[0, 0:1, :] * mix

    ms = jnp.mean(x1 * x1, axis=-1, keepdims=True)
    gain = g2_ref[...] * (1.0 + mod_ref[0, 2:3, :])
    xn_f32 = (x1 * lax.rsqrt(ms + EPS)) * gain + mod_ref[0, 1:2, :]
    prev = mix
    acc = None
    for lo, size in FFN_CHUNKS:
        lhs = tied(xn_f32, tokens)
        tokens = run_slot(prev)
        a = _dot(lhs, wfi_ref[:, lo:lo + size])
        lhs = tied(xn_f32, tokens)
        tokens = run_slot(a)
        g = _dot(lhs, wfi_ref[:, D_FF + lo:D_FF + lo + size])
        lhs = tied((a * _sigmoid(a)) * g, tokens)
        tokens = run_slot(g)
        part = _dot(lhs, wfo_ref[lo:lo + size, :])
        prev = part
        acc = part if acc is None else acc + part
    assert not slots
    out = x1 + mod_ref[0, 3:4, :] * acc
    if tokens:
        zero = sum(_exact_zero(t) for t in tokens)
        out = out + jnp.tile(zero, (tm // V7X_SUBLANES, out.shape[1] // V7X_LANES))
    o_ref[0] = out


def _mix_ffn(x, attn, yf, a_conv, u_pool, mod4, g2, w_f, dw_w, cvec, w_pw, w_pool_bd, w_o,
             w_fi, w_fo, tm):
    bsz, n, d = x.shape
    nt = n // tm
    n_steps = bsz * nt
    hb = tm // HALO
    last_halo = n // HALO - 1

    def nxt(s):
        s1 = jnp.minimum(s + 1, n_steps - 1)
        return s1 // nt, s1 % nt

    tok = lambda width: pl.BlockSpec((1, tm, width), lambda s: (s // nt, s % nt, 0))
    cur_main = lambda w: pl.BlockSpec((1, tm, w), lambda s: (0, 0, 0))
    cur_prev = lambda w: pl.BlockSpec((1, HALO, w), lambda s: (0, 0, 0))
    cur_next = lambda w: pl.BlockSpec((1, HALO, w), lambda s: (0, min(hb, last_halo), 0))
    nxt_main = lambda w: pl.BlockSpec((1, tm, w), lambda s: (nxt(s)[0], nxt(s)[1], 0))
    nxt_prev = lambda w: pl.BlockSpec(
        (1, HALO, w), lambda s: (nxt(s)[0], jnp.maximum(nxt(s)[1] * hb - 1, 0), 0))
    nxt_next = lambda w: pl.BlockSpec(
        (1, HALO, w), lambda s: (nxt(s)[0], jnp.minimum((nxt(s)[1] + 1) * hb, last_halo), 0))
    rows_sh = tm + 2 * HALO - V7X_SUBLANES
    return pl.pallas_call(
        functools.partial(_mixffn_kernel, nt, n),
        grid=(n_steps,),
        in_specs=[
            tok(d), tok(W_ATTN),
            pl.BlockSpec((tm, W_FOURIER), lambda s: (s % nt, s // nt)),
            cur_main(2 * W_CONV), cur_prev(2 * W_CONV), cur_next(2 * W_CONV),
            cur_main(W_POOL), cur_prev(W_POOL), cur_next(W_POOL),
            nxt_main(2 * W_CONV), nxt_prev(2 * W_CONV), nxt_next(2 * W_CONV),
            nxt_main(W_POOL), nxt_prev(W_POOL), nxt_next(W_POOL),
            pl.BlockSpec((1, 4, d), lambda s: (s // nt, 0, 0)),
            _resident((1, d)),
            _resident((W_FOURIER, W_FOURIER)),
            _resident((CONV_TAP_ROWS, W_CONV)),
            _resident((V7X_SUBLANES, W_CONV)),
            _resident((W_CONV, W_CONV)),
            _resident((W_POOL, W_POOL)),
            _resident((4 * W_ATTN, d)),
            _resident((d, 2 * D_FF)),
            _resident((D_FF, d)),
        ],
        out_specs=tok(d),
        out_shape=jax.ShapeDtypeStruct((bsz, n, d), F32),
        scratch_shapes=[
            pltpu.VMEM((tm + 2 * HALO, W_CONV), F32),
            pltpu.VMEM((V7X_SUBLANES - 1, rows_sh, W_CONV), F32),
            pltpu.VMEM((tm + 2 * HALO, W_POOL), F32),
            pltpu.VMEM((V7X_SUBLANES - 1, rows_sh, W_POOL), F32),
            pltpu.VMEM((tm, W_CONV), BF16),
            pltpu.VMEM((tm, W_POOL), BF16),
        ],
        compiler_params=pltpu.CompilerParams(
            dimension_semantics=("arbitrary",), vmem_limit_bytes=56 * 1024 * 1024),
        name="mix_ffn_n%d" % n,
    )(x, attn, yf, a_conv, a_conv, a_conv, u_pool, u_pool, u_pool,
      a_conv, a_conv, a_conv, u_pool, u_pool, u_pool, mod4, g2,
      w_f, dw_w, cvec, w_pw, w_pool_bd, w_o, w_fi, w_fo)


def _pool_block_diag(w_pool_l):
    groups = w_pool_l.shape[0]
    out = jnp.zeros((W_POOL, W_POOL), w_pool_l.dtype)
    for g in range(groups):
        sl = slice(g * POOL_GROUP, (g + 1) * POOL_GROUP)
        out = out.at[sl, sl].set(w_pool_l[g])
    return out


def kernel(x, c, ctx, c_ctx, w_mod, b_mod, g_norm1, g_norm2, w_in, q_norm_g, k_norm_g,
           w_fourier, conv_dw_w, conv_dw_b, conv_ln_g, conv_ln_b, w_conv_pw, w_pool,
           pool_scale, w_out, w_ffn_in, w_ffn_out):
    bsz, n, d = x.shape
    n_ctx = ctx.shape[1]
    tm_lat, tm_ctx = 512, n_ctx
    tq_lat, tk_lat = 256, 1024

    cc = jnp.zeros((V7X_SUBLANES, d), F32).at[:bsz].set(c).at[bsz].set(c_ctx)
    mod_all = _modulation(cc, w_mod, b_mod)

    h = ctx
    for l in range(DEPTH):
        last = l == DEPTH - 1
        m6 = mod_all[l].reshape(V7X_SUBLANES, 6, d)
        lat = m6[:bsz]
        cx = jnp.broadcast_to(m6[bsz:bsz + 1], (bsz, 6, d))
        w_in_l = w_in[l].astype(BF16)
        qk_gain = jnp.concatenate([jnp.tile(q_norm_g[l], N_Q_HEADS),
                                   jnp.tile(k_norm_g[l], N_KV_HEADS)])[None, :]
        g1 = g_norm1[l][None, :]
        g2 = g_norm2[l][None, :]
        w_f = w_fourier[l].astype(BF16)
        w_pw = w_conv_pw[l].astype(BF16)
        w_pl = _pool_block_diag(w_pool[l]).astype(BF16)
        w_o = w_out[l].astype(BF16)
        w_fi = w_ffn_in[l].astype(BF16)
        w_fo = w_ffn_out[l].astype(BF16)
        dw_taps = jnp.zeros((CONV_TAP_ROWS, W_CONV), F32).at[:CONV_WIDTH].set(conv_dw_w[l])
        cvec = jnp.zeros((V7X_SUBLANES, W_CONV), F32)
        cvec = cvec.at[0].set(conv_dw_b[l]).at[1].set(conv_ln_g[l])
        cvec = cvec.at[2].set(conv_ln_b[l]).at[3].set(pool_scale[l])

        qc, kc, vc, gfc, ac, uc = _inproj(h, cx[:, 0:2], g1, w_in_l, qk_gain, False, tm_ctx)
        q, k, v, gf, a, u = _inproj(x, lat[:, 0:2], g1, w_in_l, qk_gain, True, tm_lat)

        attn = _attention(q, [(kc, vc), (k, v)], tq_lat, tk_lat)
        yf = _fourier_latent(gf)
        x_new = _mix_ffn(x, attn, yf, a, u, lat[:, 2:6], g2, w_f, dw_taps, cvec, w_pw, w_pl,
                         w_o, w_fi, w_fo, tm_lat)

        if not last:
            attn_c = _attention(qc, [(kc, vc)], n_ctx, n_ctx)
            yfc = _fourier_ctx(gfc)
            h = _mix_ffn(h, attn_c, yfc, ac, uc, cx[:, 2:6], g2, w_f, dw_taps, cvec, w_pw,
                         w_pl, w_o, w_fi, w_fo, tm_ctx)
        x = x_new
    return x
```

```python
import functools

import numpy as np
import jax
import jax.numpy as jnp
from jax import lax
from jax.experimental import pallas as pl
from jax.experimental.pallas import tpu as pltpu

F32 = jnp.float32
BF16 = jnp.bfloat16

D_MODEL = 1024
DEPTH = 2
GRID_W = 64
HEAD_DIM = 64
N_Q_HEADS = 4
N_KV_HEADS = 2
W_ATTN = 256
W_FOURIER = 256
W_CONV = 256
W_POOL = 256
CONV_WIDTH = 31
POOL_GROUP = 64
D_FF = 2816
ROPE_THETA = 10000.0
EPS = 1e-6
ATTN_SCALE = HEAD_DIM ** -0.5
Q_SCALE_LOG2 = ATTN_SCALE * float(np.log2(np.e))
W_QK = W_ATTN + N_KV_HEADS * HEAD_DIM
OFF_V = W_QK
OFF_F = OFF_V + N_KV_HEADS * HEAD_DIM
OFF_C = OFF_F + W_FOURIER
OFF_P = OFF_C + 2 * W_CONV
D_IN = OFF_P + W_POOL

V7X_LANES = 128
V7X_SUBLANES = 8
V7X_BF16_ROWS = 16
V7X_VMEM_BYTES = 64 * 1024 * 1024

HALO = V7X_BF16_ROWS
DFT_N1 = 64
NEG_BIG = -1e30


def _dot(a, b):
    return jnp.dot(a, b, preferred_element_type=F32)


def _sigmoid(x):
    return 1.0 / (1.0 + jnp.exp(-x))


def _split_bf16(x):
    hi = x.astype(BF16)
    lo = (x - hi.astype(F32)).astype(BF16)
    return hi, lo


def _resident(shape):
    nd = len(shape)
    return pl.BlockSpec(shape, lambda *_: (0,) * nd, pipeline_mode=pl.Buffered(1))


def _params(n_grid, vmem_mb):
    return pltpu.CompilerParams(
        dimension_semantics=("parallel",) * n_grid,
        vmem_limit_bytes=vmem_mb * 1024 * 1024)


@functools.lru_cache(maxsize=None)
def _rope_tables(n):
    rows = n // GRID_W
    row = np.repeat(np.arange(rows, dtype=np.float64), GRID_W)
    col = np.tile(np.arange(GRID_W, dtype=np.float64), rows)
    n_freq = HEAD_DIM // 4
    inv_freq = ROPE_THETA ** (-np.arange(n_freq, dtype=np.float64) / n_freq)
    ang = np.concatenate([row[:, None] * inv_freq, col[:, None] * inv_freq], axis=-1)
    cos = np.repeat(np.cos(ang), 2, axis=1)
    sin = np.repeat(np.sin(ang), 2, axis=1)
    sign = np.tile(np.array([-1.0, 1.0]), HEAD_DIM // 2)
    cos2 = np.tile(cos, (1, 2)).astype(np.float32)
    sin2 = np.tile(sin * sign, (1, 2)).astype(np.float32)
    return cos2, sin2


@functools.lru_cache(maxsize=None)
def _head_sum_matrix():
    idx = np.arange(W_QK) // HEAD_DIM
    return np.asarray((idx[:, None] == idx[None, :]).astype(np.float32), dtype=BF16)


@functools.lru_cache(maxsize=None)
def _channel_dft():
    c = np.arange(W_FOURIER)
    same = (c[:, None] // HEAD_DIM) == (c[None, :] // HEAD_DIM)
    ang = 2.0 * np.pi * ((c[:, None] % HEAD_DIM) * (c[None, :] % HEAD_DIM)) / HEAD_DIM
    cr = np.where(same, np.cos(ang), 0.0) / 8.0
    ci = np.where(same, -np.sin(ang), 0.0) / 8.0
    return np.concatenate([cr, ci], axis=1).astype(np.float32)


@functools.lru_cache(maxsize=None)
def _seq_dft_tables(n):
    n1, n2 = DFT_N1, n // DFT_N1
    t1 = np.arange(n1)[:, None, None]
    k2 = np.arange(n2)[None, :, None]
    t2 = np.arange(n2)[None, None, :]
    theta = 2.0 * np.pi * ((k2 * (t1 + n1 * t2)) % n) / n
    dr = np.cos(theta) / np.sqrt(n2)
    di = -np.sin(theta) / np.sqrt(n2)
    tab_r = np.concatenate([dr, di], axis=1)
    tab_i = np.concatenate([-di, dr], axis=1)
    k1 = np.arange(n1)[:, None]
    tt = np.arange(n1)[None, :]
    phi = 2.0 * np.pi * ((k1 * tt) % n1) / n1
    tab_c = np.concatenate([np.cos(phi), np.sin(phi)], axis=1) / np.sqrt(n1)
    return tab_r.astype(np.float32), tab_i.astype(np.float32), tab_c.astype(np.float32)


@functools.lru_cache(maxsize=None)
def _ctx_dft_tables(n):
    k = np.arange(n)[:, None]
    t = np.arange(n)[None, :]
    ang = 2.0 * np.pi * ((k * t) % n) / n
    s = 1.0 / np.sqrt(n)
    return (np.cos(ang) * s).astype(np.float32), (np.sin(ang) * s).astype(np.float32)


def _mod_kernel(c_ref, w_ref, b_ref, o_ref):
    c = c_ref[...]
    sc = c * _sigmoid(c)
    a_hi, a_lo = _split_bf16(sc)
    w_hi, w_lo = _split_bf16(w_ref[0])
    o_ref[0] = _dot(a_hi, w_hi) + _dot(a_lo, w_hi) + _dot(a_hi, w_lo) + b_ref[0]


def _modulation(cc, w_mod, b_mod):
    depth, d, d6 = w_mod.shape
    tn = 1536
    return pl.pallas_call(
        _mod_kernel,
        grid=(depth, d6 // tn),
        in_specs=[
            pl.BlockSpec((V7X_SUBLANES, d), lambda l, j: (0, 0)),
            pl.BlockSpec((1, d, tn), lambda l, j: (l, 0, j)),
            pl.BlockSpec((1, 1, tn), lambda l, j: (l, 0, j)),
        ],
        out_specs=pl.BlockSpec((1, V7X_SUBLANES, tn), lambda l, j: (l, 0, j)),
        out_shape=jax.ShapeDtypeStruct((depth, V7X_SUBLANES, d6), F32),
        compiler_params=_params(2, 40),
        name="modulation",
    )(cc, w_mod, b_mod.reshape(depth, 1, d6))


INPROJ_ROWS = 256


def _rope(x, cos, sin):
    lane = lax.broadcasted_iota(jnp.int32, x.shape, 1)
    swapped = jnp.where((lane & 1) == 0,
                        pltpu.roll(x, V7X_LANES - 1, axis=1),
                        pltpu.roll(x, 1, axis=1))
    return x * cos + swapped * sin


def _inproj_kernel(rope, x_ref, modv_ref, g1_ref, w_ref, qkg_ref, bd_ref, cd_ref, *rest):
    if rope:
        cos_ref, sin_ref = rest[:2]
        rest = rest[2:]
    q_ref, k_ref, v_ref, gf_ref, a_ref, u_ref = rest
    tm = x_ref.shape[1]
    rg = min(tm, INPROJ_ROWS)
    gain = g1_ref[...] * (1.0 + modv_ref[0, 1:2, :])
    lane = lax.broadcasted_iota(jnp.int32, (rg, V7X_LANES), 1)
    one_col = jnp.where(lane == HEAD_DIM, 1.0, 0.0)
    row = lax.broadcasted_iota(jnp.int32, (V_AUG - HEAD_DIM, rg), 0)
    ones_rows = jnp.where(row == 0, 1.0, 0.0).astype(BF16)
    for r0 in range(0, tm, rg):
        rows = slice(r0, r0 + rg)
        x = x_ref[0, rows, :]
        ms = jnp.mean(x * x, axis=-1, keepdims=True)
        xn = (x * lax.rsqrt(ms + EPS)) * gain + modv_ref[0, 0:1, :]
        p = _dot(xn.astype(BF16), w_ref[...])

        qk = p[:, :W_QK]
        ss = _dot((qk * qk).astype(BF16), bd_ref[...])
        qkn = qk * lax.rsqrt(ss * (1.0 / HEAD_DIM) + EPS) * qkg_ref[...]
        slabs = []
        for j in range(W_QK // V7X_LANES):
            slab = qkn[:, j * V7X_LANES:(j + 1) * V7X_LANES]
            if rope:
                slab = _rope(slab, cos_ref[rows, :], sin_ref[rows, :])
            slabs.append(slab)
        for j in range(W_ATTN // V7X_LANES):
            qt = (slabs[j] * Q_SCALE_LOG2).T.astype(BF16)
            q_ref[0, 2 * j, :, rows] = qt[:HEAD_DIM]
            q_ref[0, 2 * j + 1, :, rows] = qt[HEAD_DIM:]
        k_ref[0, 0, rows, :] = jnp.where(lane < HEAD_DIM, slabs[2], one_col).astype(BF16)
        k_ref[0, 1, rows, :] = jnp.where(lane < HEAD_DIM, pltpu.roll(slabs[2], HEAD_DIM, axis=1),
                                         one_col).astype(BF16)
        vt = p[:, OFF_V:OFF_F].T.astype(BF16)
        for g in range(N_KV_HEADS):
            v_ref[0, g, 0:HEAD_DIM, rows] = vt[g * HEAD_DIM:(g + 1) * HEAD_DIM]
            v_ref[0, g, HEAD_DIM:, rows] = ones_rows
        gf_ref[0, rows, :] = _dot(p[:, OFF_F:OFF_C].astype(BF16), cd_ref[...]).astype(BF16)
        a_ref[0, rows, :] = p[:, OFF_C:OFF_P].astype(BF16)
        u_ref[0, rows, :] = p[:, OFF_P:].astype(BF16)


def _inproj(x, modv, g1, w_in, qk_gain, rope, tm):
    bsz, n, d = x.shape
    nt = n // tm
    tok = lambda width: pl.BlockSpec((1, tm, width), lambda b, i: (b, i, 0))
    in_specs = [
        tok(d),
        pl.BlockSpec((1, 2, d), lambda b, i: (b, 0, 0)),
        _resident((1, d)),
        _resident((d, D_IN)),
        _resident((1, W_QK)),
        _resident((W_QK, W_QK)),
        _resident((W_FOURIER, 2 * W_FOURIER)),
    ]
    args = [x, modv, g1, w_in, qk_gain, _head_sum_matrix(),
            jnp.asarray(_channel_dft()).astype(BF16)]
    if rope:
        cos2, sin2 = _rope_tables(n)
        in_specs += [pl.BlockSpec((tm, V7X_LANES), lambda b, i: (i, 0))] * 2
        args += [cos2, sin2]
    widths = (2 * W_FOURIER, 2 * W_CONV, W_POOL)
    out_specs = [
        pl.BlockSpec((1, N_Q_HEADS, HEAD_DIM, tm), lambda b, i: (b, 0, 0, i)),
        pl.BlockSpec((1, N_KV_HEADS, tm, K_AUG), lambda b, i: (b, 0, i, 0)),
        pl.BlockSpec((1, N_KV_HEADS, V_AUG, tm), lambda b, i: (b, 0, 0, i)),
    ] + [tok(w) for w in widths]
    out_shape = [
        jax.ShapeDtypeStruct((bsz, N_Q_HEADS, HEAD_DIM, n), BF16),
        jax.ShapeDtypeStruct((bsz, N_KV_HEADS, n, K_AUG), BF16),
        jax.ShapeDtypeStruct((bsz, N_KV_HEADS, V_AUG, n), BF16),
    ] + [jax.ShapeDtypeStruct((bsz, n, w), BF16) for w in widths]
    return pl.pallas_call(
        functools.partial(_inproj_kernel, rope),
        grid=(bsz, nt),
        in_specs=in_specs,
        out_specs=out_specs,
        out_shape=out_shape,
        compiler_params=_params(2, 40),
        name="inproj_rope" if rope else "inproj_ctx",
    )(*args)


K_AUG = V7X_LANES
V_AUG = HEAD_DIM + V7X_BF16_ROWS
SAFE_SHIFT_LOG2 = 50.0


def _attn_kernel(chunks, q_ref, *refs):
    n_src = (len(refs) - 3) // 2
    k_refs, v_refs = refs[:n_src], refs[n_src:2 * n_src]
    o_ref, kmax_ref, acc_ref = refs[2 * n_src:]
    n_chunks = len(chunks)
    tq = q_ref.shape[-1]
    i = pl.program_id(2)

    def keys(c):
        src, start, size = chunks[c]
        return k_refs[src][0, 0, start:start + size, :]

    def values(c):
        src, start, size = chunks[c]
        return v_refs[src][0, 0, :, start:start + size]

    @pl.when(i == 0)
    def _():
        ones = jnp.ones((K_AUG, K_AUG), BF16)
        mx = jnp.zeros((1, K_AUG), F32)
        for c in range(n_chunks):
            kf = keys(c).astype(F32)
            norms = _dot((kf * kf).astype(BF16), ones)
            mx = jnp.maximum(mx, jnp.max(norms, axis=0, keepdims=True))
        kmax_ref[...] = jnp.sqrt(mx)

    q2 = jnp.concatenate([q_ref[0, 0], q_ref[0, 1]], axis=1)
    qf = q2.astype(F32)
    kmax = jnp.tile(kmax_ref[...], (1, 2 * tq // K_AUG))
    bound = jnp.sqrt(jnp.sum(qf * qf, axis=0, keepdims=True)) * kmax
    safe = jnp.max(bound) <= SAFE_SHIFT_LOG2
    row = lax.broadcasted_iota(jnp.int32, (K_AUG - HEAD_DIM, 2 * tq), 0)
    shift_rows = jnp.where(row == 0, -bound, 0.0).astype(BF16)
    q_aug = jnp.concatenate([q2, shift_rows], axis=0)

    def logits(c):
        return _dot(keys(c), q_aug)

    @pl.when(safe)
    def _():
        acc = None
        for c in range(n_chunks):
            p = jnp.exp2(logits(c)).astype(BF16)
            part = _dot(values(c), p)
            acc = part if acc is None else acc + part
        acc_ref[...] = acc

    @pl.when(jnp.logical_not(safe))
    def _():
        m = jnp.full((1, 2 * tq), NEG_BIG, F32)
        acc = jnp.zeros((V_AUG, 2 * tq), F32)
        for c in range(n_chunks):
            s = logits(c)
            m_new = jnp.maximum(m, jnp.max(s, axis=0, keepdims=True))
            p = jnp.exp2(s - m_new).astype(BF16)
            acc = jnp.exp2(m - m_new) * acc + _dot(values(c), p)
            m = m_new
        acc_ref[...] = acc

    o = acc_ref[0:HEAD_DIM, :] * (1.0 / acc_ref[HEAD_DIM:HEAD_DIM + 1, :])
    o2 = jnp.concatenate([o[:, :tq], o[:, tq:]], axis=0)
    o_ref[0] = o2.T.astype(BF16)


def _attention(q_t, kv_sources, tq, tk):
    bsz, _, _, nq = q_t.shape
    q_per_kv = N_Q_HEADS // N_KV_HEADS
    chunks = []
    for src, (k_aug, _) in enumerate(kv_sources):
        nk = k_aug.shape[2]
        step = min(tk, nk)
        chunks += [(src, start, step) for start in range(0, nk, step)]
    k_specs = [pl.BlockSpec((1, 1, k.shape[2], K_AUG), lambda b, g, i: (b, g, 0, 0))
               for k, _ in kv_sources]
    v_specs = [pl.BlockSpec((1, 1, V_AUG, v.shape[3]), lambda b, g, i: (b, g, 0, 0))
               for _, v in kv_sources]
    return pl.pallas_call(
        functools.partial(_attn_kernel, tuple(chunks)),
        grid=(bsz, N_KV_HEADS, nq // tq),
        in_specs=[pl.BlockSpec((1, q_per_kv, HEAD_DIM, tq), lambda b, g, i: (b, g, 0, i))]
        + k_specs + v_specs,
        out_specs=pl.BlockSpec((1, tq, q_per_kv * HEAD_DIM), lambda b, g, i: (b, i, g)),
        out_shape=jax.ShapeDtypeStruct((bsz, nq, W_ATTN), BF16),
        scratch_shapes=[pltpu.VMEM((1, K_AUG), F32),
                        pltpu.VMEM((V_AUG, q_per_kv * tq), F32)],
        compiler_params=pltpu.CompilerParams(
            dimension_semantics=("parallel", "parallel", "arbitrary"),
            vmem_limit_bytes=40 * 1024 * 1024),
        name="attention_%dsrc" % len(kv_sources),
    )(q_t, *[k for k, _ in kv_sources], *[v for _, v in kv_sources])


DFT_A_GROUP = 8


def _dft_a_kernel(bsz, g_ref, tr_ref, ti_ref, o_ref):
    n2 = g_ref.shape[2]
    for j in range(DFT_A_GROUP):
        tab_r = tr_ref[j].astype(BF16)
        tab_i = ti_ref[j].astype(BF16)
        for b in range(bsz):
            res = (_dot(tab_r, g_ref[j, b, :, :W_FOURIER])
                   + _dot(tab_i, g_ref[j, b, :, W_FOURIER:]))
            o_ref[0, j, :, b * W_FOURIER:(b + 1) * W_FOURIER] = res[:n2].astype(BF16)
            o_ref[1, j, :, b * W_FOURIER:(b + 1) * W_FOURIER] = res[n2:].astype(BF16)


def _dft_c_kernel(a_ref, tc_ref, o_ref):
    o_ref[...] = _dot(tc_ref[...], a_ref[...]).astype(BF16)


def _fourier_latent(gf):
    bsz, n, _ = gf.shape
    n1, n2 = DFT_N1, n // DFT_N1
    tab_r, tab_i, tab_c = _seq_dft_tables(n)
    g_t = jnp.transpose(gf.reshape(bsz, n2, n1, 2 * W_FOURIER), (2, 0, 1, 3))
    lanes = bsz * W_FOURIER
    a = pl.pallas_call(
        functools.partial(_dft_a_kernel, bsz),
        grid=(n1 // DFT_A_GROUP,),
        in_specs=[
            pl.BlockSpec((DFT_A_GROUP, bsz, n2, 2 * W_FOURIER), lambda t: (t, 0, 0, 0)),
            pl.BlockSpec((DFT_A_GROUP, 2 * n2, n2), lambda t: (t, 0, 0)),
            pl.BlockSpec((DFT_A_GROUP, 2 * n2, n2), lambda t: (t, 0, 0)),
        ],
        out_specs=pl.BlockSpec((2, DFT_A_GROUP, n2, lanes), lambda t: (0, t, 0, 0)),
        out_shape=jax.ShapeDtypeStruct((2, n1, n2, lanes), BF16),
        compiler_params=_params(1, 40),
        name="dft_stage_a",
    )(g_t, tab_r, tab_i)
    a2 = a.reshape(2 * n1, n2 * lanes)
    tl = 4096
    y = pl.pallas_call(
        _dft_c_kernel,
        grid=(n2 * lanes // tl,),
        in_specs=[pl.BlockSpec((2 * n1, tl), lambda j: (0, j)), _resident((n1, 2 * n1))],
        out_specs=pl.BlockSpec((n1, tl), lambda j: (0, j)),
        out_shape=jax.ShapeDtypeStruct((n1, n2 * lanes), BF16),
        compiler_params=_params(1, 40),
        name="dft_stage_c",
    )(a2, jnp.asarray(tab_c).astype(BF16))
    return y.reshape(n, lanes)


def _dft_ctx_kernel(g_ref, c_ref, s_ref, o_ref):
    o_ref[...] = (_dot(c_ref[...], g_ref[0, :, :W_FOURIER])
                  + _dot(s_ref[...], g_ref[0, :, W_FOURIER:])).astype(BF16)


def _fourier_ctx(gf):
    bsz, n, _ = gf.shape
    tab_cos, tab_sin = _ctx_dft_tables(n)
    return pl.pallas_call(
        _dft_ctx_kernel,
        grid=(bsz,),
        in_specs=[pl.BlockSpec((1, n, 2 * W_FOURIER), lambda b: (b, 0, 0)),
                  _resident((n, n)), _resident((n, n))],
        out_specs=pl.BlockSpec((n, W_FOURIER), lambda b: (0, b)),
        out_shape=jax.ShapeDtypeStruct((n, bsz * W_FOURIER), BF16),
        compiler_params=_params(1, 40),
        name="dft_ctx",
    )(gf, jnp.asarray(tab_cos).astype(BF16), jnp.asarray(tab_sin).astype(BF16))


CONV_ROWS = 64
CONV_TAP_ROWS = 32
FFN_CHUNKS = ((0, 768), (768, 768), (1536, 768), (2304, 512))


def _exact_zero(v):
    return jnp.minimum(jnp.abs(v), 0.0)


def _conv_pool_units(tile, n_tiles, n_seq, tm, am_ref, ap_ref, an_ref, um_ref, up_ref, un_ref,
                     dww_ref, cvec_ref, g_ref, gsh_ref, pu_ref, psh_ref, cv_ref, pl_ref):
    first = tile == 0
    last = tile == n_tiles - 1
    rows_sh = gsh_ref.shape[1]
    corner = lambda v: v[0:V7X_SUBLANES, 0:V7X_LANES]

    def gated(v, gate):
        if gate is None:
            return v
        rows, cols = v.shape
        return v + jnp.tile(gate, (rows // V7X_SUBLANES, cols // V7X_LANES))

    def glu(a_blk):
        a = a_blk.astype(F32)
        return a[:, :W_CONV] * _sigmoid(a[:, W_CONV:])

    def stage(gate):
        main = gated(glu(am_ref[0]), gate)
        g_ref[0:HALO] = jnp.where(first, 0.0, glu(ap_ref[0]))
        g_ref[HALO:HALO + tm] = main
        g_ref[HALO + tm:] = jnp.where(last, 0.0, glu(an_ref[0]))
        pu_ref[0:HALO] = jnp.where(first, 0.0, up_ref[0].astype(F32))
        pu_ref[HALO:HALO + tm] = um_ref[0].astype(F32)
        pu_ref[HALO + tm:] = jnp.where(last, 0.0, un_ref[0].astype(F32))
        return corner(main)

    def copy(b):
        def run(gate):
            gv = gated(g_ref[pl.ds(b, rows_sh), :], gate)
            gsh_ref[b - 1] = gv
            psh_ref[b - 1] = gated(pu_ref[pl.ds(b, rows_sh), :], gate)
            return corner(gv)
        return run

    lane = lax.broadcasted_iota(jnp.int32, (CONV_ROWS, W_POOL), 1)
    grp = lane // POOL_GROUP
    half = jnp.left_shift(1, grp)
    row = lax.broadcasted_iota(jnp.int32, (CONV_ROWS, W_POOL), 0)

    def shifted(base_ref, sh_ref, c0, j):
        src = base_ref if j % V7X_SUBLANES == 0 else sh_ref.at[j % V7X_SUBLANES - 1]
        return src[pl.ds(c0 + (j // V7X_SUBLANES) * V7X_SUBLANES, CONV_ROWS), :]

    def conv(c0):
        def run(gate):
            wts = gated(dww_ref[...], gate)
            acc = jnp.broadcast_to(cvec_ref[0:1, :], (CONV_ROWS, W_CONV))
            for k in range(CONV_WIDTH):
                acc = acc + wts[k:k + 1, :] * shifted(g_ref, gsh_ref, c0, k + 1)
            mu = jnp.mean(acc, axis=-1, keepdims=True)
            yc = acc - mu
            var = jnp.mean(yc * yc, axis=-1, keepdims=True)
            yn = yc * lax.rsqrt(var + EPS) * cvec_ref[1:2, :] + cvec_ref[2:3, :]
            act = yn * _sigmoid(yn)
            cv_ref[c0:c0 + CONV_ROWS, :] = act.astype(BF16)
            return corner(act)
        return run

    def pool(c0):
        def run(gate):
            ld = lambda d: shifted(pu_ref, psh_ref, c0, HALO + d)
            u0 = ld(0)
            s2 = gated(ld(-1) + u0, gate)
            s4 = s2 + ld(-2) + ld(1)
            s8 = s4 + ld(-4) + ld(-3) + ld(2) + ld(3)
            s16 = s8 + ld(-8) + ld(-7) + ld(-6) + ld(-5) + ld(4) + ld(5) + ld(6) + ld(7)
            win = jnp.where(grp == 0, s2, jnp.where(grp == 1, s4, jnp.where(grp == 2, s8, s16)))
            t = tile * tm + c0 + row
            cnt = jnp.minimum(t + half, n_seq) - jnp.maximum(t - half, 0)
            centred = win / cnt.astype(F32) - u0
            pl_ref[c0:c0 + CONV_ROWS, :] = centred.astype(BF16)
            return corner(centred)
        return run

    units = [stage] + [copy(b) for b in range(1, V7X_SUBLANES)]
    for c in range(tm // CONV_ROWS):
        units += [conv(c * CONV_ROWS), pool(c * CONV_ROWS)]
    return units


def _mixffn_kernel(n_tiles, n_seq, x_ref, attn_ref, yf_ref,
                   amc_ref, apc_ref, anc_ref, umc_ref, upc_ref, unc_ref,
                   amn_ref, apn_ref, ann_ref, umn_ref, upn_ref, unn_ref,
                   mod_ref, g2_ref, wf_ref, dww_ref, cvec_ref, wpw_ref, wpl_ref, wo_ref,
                   wfi_ref, wfo_ref, o_ref, g_ref, gsh_ref, pu_ref, psh_ref, cv_ref, pl_ref):
    tm = x_ref.shape[1]
    s = pl.program_id(0)
    n_steps = pl.num_programs(0)
    scratch = (dww_ref, cvec_ref, g_ref, gsh_ref, pu_ref, psh_ref, cv_ref, pl_ref)

    @pl.when(s == 0)
    def _():
        for unit in _conv_pool_units(0, n_tiles, n_seq, tm, amc_ref, apc_ref, anc_ref, umc_ref,
                                     upc_ref, unc_ref, *scratch):
            unit(None)

    cv = cv_ref[...]
    pooled = pl_ref[...]
    nxt_tile = jnp.minimum(s + 1, n_steps - 1) % n_tiles
    units = _conv_pool_units(nxt_tile, n_tiles, n_seq, tm, amn_ref, apn_ref, ann_ref, umn_ref,
                             upn_ref, unn_ref, *scratch)
    n_pre = V7X_SUBLANES
    slots = [units[:n_pre // 2], units[n_pre // 2:n_pre]]
    slots += [units[i:i + 2] for i in range(n_pre, len(units), 2)]

    def run_slot(gate_src):
        if not slots:
            return []
        gate = None if gate_src is None else _exact_zero(
            gate_src[-V7X_SUBLANES:, -V7X_LANES:])
        return [unit(gate) for unit in slots.pop(0)]

    def tied(lhs_f32, tokens):
        lhs = lhs_f32.astype(BF16)
        if not tokens:
            return lhs
        zero = sum(_exact_zero(t) for t in tokens)
        head = lhs_f32[0:V7X_BF16_ROWS] + jnp.tile(
            zero, (V7X_BF16_ROWS // V7X_SUBLANES, lhs_f32.shape[1] // V7X_LANES))
        return jnp.concatenate([head.astype(BF16), lhs[V7X_BF16_ROWS:]], axis=0)

    tokens = run_slot(None)
    four = _dot(yf_ref[...], wf_ref[...]).astype(BF16)
    conv = _dot(cv, wpw_ref[...]).astype(BF16)
    pool = (_dot(pooled, wpl_ref[...]) * cvec_ref[3:4, :]).astype(BF16)
    mix = (_dot(attn_ref[0], wo_ref[0:W_ATTN, :])
           + _dot(four, wo_ref[W_ATTN:2 * W_ATTN, :])
           + _dot(conv, wo_ref[2 * W_ATTN:3 * W_ATTN, :])
           + _dot(pool, wo_ref[3 * W_ATTN:, :]))
    x1 = x_ref[0] + mod_ref[0, 0:1, :] * mix

    ms = jnp.mean(x1 * x1, axis=-1, keepdims=True)
    gain = g2_ref[...] * (1.0 + mod_ref[0, 2:3, :])
    xn_f32 = (x1 * lax.rsqrt(ms + EPS)) * gain + mod_ref[0, 1:2, :]
    prev = mix
    acc = None
    for lo, size in FFN_CHUNKS:
        lhs = tied(xn_f32, tokens)
        tokens = run_slot(prev)
        a = _dot(lhs, wfi_ref[:, lo:lo + size])
        lhs = tied(xn_f32, tokens)
        tokens = run_slot(a)
        g = _dot(lhs, wfi_ref[:, D_FF + lo:D_FF + lo + size])
        lhs = tied((a * _sigmoid(a)) * g, tokens)
        tokens = run_slot(g)
        part = _dot(lhs, wfo_ref[lo:lo + size, :])
        prev = part
        acc = part if acc is None else acc + part
    assert not slots
    out = x1 + mod_ref[0, 3:4, :] * acc
    if tokens:
        zero = sum(_exact_zero(t) for t in tokens)
        out = out + jnp.tile(zero, (tm // V7X_SUBLANES, out.shape[1] // V7X_LANES))
    o_ref[0] = out


def _mix_ffn(x, attn, yf, a_conv, u_pool, mod4, g2, w_f, dw_w, cvec, w_pw, w_pool_bd, w_o,
             w_fi, w_fo, tm):
    bsz, n, d = x.shape
    nt = n // tm
    n_steps = bsz * nt
    hb = tm // HALO
    last_halo = n // HALO - 1

    def nxt(s):
        s1 = jnp.minimum(s + 1, n_steps - 1)
        return s1 // nt, s1 % nt

    tok = lambda width: pl.BlockSpec((1, tm, width), lambda s: (s // nt, s % nt, 0))
    cur_main = lambda w: pl.BlockSpec((1, tm, w), lambda s: (0, 0, 0))
    cur_prev = lambda w: pl.BlockSpec((1, HALO, w), lambda s: (0, 0, 0))
    cur_next = lambda w: pl.BlockSpec((1, HALO, w), lambda s: (0, min(hb, last_halo), 0))
    nxt_main = lambda w: pl.BlockSpec((1, tm, w), lambda s: (nxt(s)[0], nxt(s)[1], 0))
    nxt_prev = lambda w: pl.BlockSpec(
        (1, HALO, w), lambda s: (nxt(s)[0], jnp.maximum(nxt(s)[1] * hb - 1, 0), 0))
    nxt_next = lambda w: pl.BlockSpec(
        (1, HALO, w), lambda s: (nxt(s)[0], jnp.minimum((nxt(s)[1] + 1) * hb, last_halo), 0))
    rows_sh = tm + 2 * HALO - V7X_SUBLANES
    return pl.pallas_call(
        functools.partial(_mixffn_kernel, nt, n),
        grid=(n_steps,),
        in_specs=[
            tok(d), tok(W_ATTN),
            pl.BlockSpec((tm, W_FOURIER), lambda s: (s % nt, s // nt)),
            cur_main(2 * W_CONV), cur_prev(2 * W_CONV), cur_next(2 * W_CONV),
            cur_main(W_POOL), cur_prev(W_POOL), cur_next(W_POOL),
            nxt_main(2 * W_CONV), nxt_prev(2 * W_CONV), nxt_next(2 * W_CONV),
            nxt_main(W_POOL), nxt_prev(W_POOL), nxt_next(W_POOL),
            pl.BlockSpec((1, 4, d), lambda s: (s // nt, 0, 0)),
            _resident((1, d)),
            _resident((W_FOURIER, W_FOURIER)),
            _resident((CONV_TAP_ROWS, W_CONV)),
            _resident((V7X_SUBLANES, W_CONV)),
            _resident((W_CONV, W_CONV)),
            _resident((W_POOL, W_POOL)),
            _resident((4 * W_ATTN, d)),
            _resident((d, 2 * D_FF)),
            _resident((D_FF, d)),
        ],
        out_specs=tok(d),
        out_shape=jax.ShapeDtypeStruct((bsz, n, d), F32),
        scratch_shapes=[
            pltpu.VMEM((tm + 2 * HALO, W_CONV), F32),
            pltpu.VMEM((V7X_SUBLANES - 1, rows_sh, W_CONV), F32),
            pltpu.VMEM((tm + 2 * HALO, W_POOL), F32),
            pltpu.VMEM((V7X_SUBLANES - 1, rows_sh, W_POOL), F32),
            pltpu.VMEM((tm, W_CONV), BF16),
            pltpu.VMEM((tm, W_POOL), BF16),
        ],
        compiler_params=pltpu.CompilerParams(
            dimension_semantics=("arbitrary",), vmem_limit_bytes=56 * 1024 * 1024),
        name="mix_ffn_n%d" % n,
    )(x, attn, yf, a_conv, a_conv, a_conv, u_pool, u_pool, u_pool,
      a_conv, a_conv, a_conv, u_pool, u_pool, u_pool, mod4, g2,
      w_f, dw_w, cvec, w_pw, w_pool_bd, w_o, w_fi, w_fo)


def _pool_block_diag(w_pool_l):
    groups = w_pool_l.shape[0]
    out = jnp.zeros((W_POOL, W_POOL), w_pool_l.dtype)
    for g in range(groups):
        sl = slice(g * POOL_GROUP, (g + 1) * POOL_GROUP)
        out = out.at[sl, sl].set(w_pool_l[g])
    return out


def kernel(x, c, ctx, c_ctx, w_mod, b_mod, g_norm1, g_norm2, w_in, q_norm_g, k_norm_g,
           w_fourier, conv_dw_w, conv_dw_b, conv_ln_g, conv_ln_b, w_conv_pw, w_pool,
           pool_scale, w_out, w_ffn_in, w_ffn_out):
    bsz, n, d = x.shape
    n_ctx = ctx.shape[1]
    tm_lat, tm_ctx = 512, n_ctx
    tq_lat, tk_lat = 512, 1024

    cc = jnp.zeros((V7X_SUBLANES, d), F32).at[:bsz].set(c).at[bsz].set(c_ctx)
    mod_all = _modulation(cc, w_mod, b_mod)

    h = ctx
    for l in range(DEPTH):
        last = l == DEPTH - 1
        m6 = mod_all[l].reshape(V7X_SUBLANES, 6, d)
        lat = m6[:bsz]
        cx = jnp.broadcast_to(m6[bsz:bsz + 1], (bsz, 6, d))
        w_in_l = w_in[l].astype(BF16)
        qk_gain = jnp.concatenate([jnp.tile(q_norm_g[l], N_Q_HEADS),
                                   jnp.tile(k_norm_g[l], N_KV_HEADS)])[None, :]
        g1 = g_norm1[l][None, :]
        g2 = g_norm2[l][None, :]
        w_f = w_fourier[l].astype(BF16)
        w_pw = w_conv_pw[l].astype(BF16)
        w_pl = _pool_block_diag(w_pool[l]).astype(BF16)
        w_o = w_out[l].astype(BF16)
        w_fi = w_ffn_in[l].astype(BF16)
        w_fo = w_ffn_out[l].astype(BF16)
        dw_taps = jnp.zeros((CONV_TAP_ROWS, W_CONV), F32).at[:CONV_WIDTH].set(conv_dw_w[l])
        cvec = jnp.zeros((V7X_SUBLANES, W_CONV), F32)
        cvec = cvec.at[0].set(conv_dw_b[l]).at[1].set(conv_ln_g[l])
        cvec = cvec.at[2].set(conv_ln_b[l]).at[3].set(pool_scale[l])

        qc, kc, vc, gfc, ac, uc = _inproj(h, cx[:, 0:2], g1, w_in_l, qk_gain, False, tm_ctx)
        q, k, v, gf, a, u = _inproj(x, lat[:, 0:2], g1, w_in_l, qk_gain, True, tm_lat)

        attn = _attention(q, [(kc, vc), (k, v)], tq_lat, tk_lat)
        yf = _fourier_latent(gf)
        x_new = _mix_ffn(x, attn, yf, a, u, lat[:, 2:6], g2, w_f, dw_taps, cvec, w_pw, w_pl,
                         w_o, w_fi, w_fo, tm_lat)

        if not last:
            attn_c = _attention(qc, [(kc, vc)], n_ctx, n_ctx)
            yfc = _fourier_ctx(gfc)
            h = _mix_ffn(h, attn_c, yfc, ac, uc, cx[:, 2:6], g2, w_f, dw_taps, cvec, w_pw,
                         w_pl, w_o, w_fi, w_fo, tm_ctx)
        x = x_new
    return x
```

```python
import functools

import numpy as np
import jax
import jax.numpy as jnp
from jax import lax
from jax.experimental import pallas as pl
from jax.experimental.pallas import tpu as pltpu

F32 = jnp.float32
BF16 = jnp.bfloat16

D_MODEL = 1024
DEPTH = 2
GRID_W = 64
HEAD_DIM = 64
N_Q_HEADS = 4
N_KV_HEADS = 2
W_ATTN = 256
W_FOURIER = 256
W_CONV = 256
W_POOL = 256
CONV_WIDTH = 31
POOL_GROUP = 64
D_FF = 2816
ROPE_THETA = 10000.0
EPS = 1e-6
ATTN_SCALE = HEAD_DIM ** -0.5
Q_SCALE_LOG2 = ATTN_SCALE * float(np.log2(np.e))
W_QK = W_ATTN + N_KV_HEADS * HEAD_DIM
OFF_V = W_QK
OFF_F = OFF_V + N_KV_HEADS * HEAD_DIM
OFF_C = OFF_F + W_FOURIER
OFF_P = OFF_C + 2 * W_CONV
D_IN = OFF_P + W_POOL

V7X_LANES = 128
V7X_SUBLANES = 8
V7X_BF16_ROWS = 16
V7X_VMEM_BYTES = 64 * 1024 * 1024

MOD_SHIFT1, MOD_SCALE1, MOD_GATE1, MOD_SHIFT2, MOD_SCALE2, MOD_GATE2 = range(6)
N_MOD = 6

HALO = V7X_BF16_ROWS
DFT_N1 = 64
NEG_BIG = -1e30


def _dot(a, b):
    return jnp.dot(a, b, preferred_element_type=F32)


def _sigmoid(x):
    return 1.0 / (1.0 + jnp.exp(-x))


def _split_bf16(x):
    hi = x.astype(BF16)
    lo = (x - hi.astype(F32)).astype(BF16)
    return hi, lo


def _resident(shape):
    nd = len(shape)
    return pl.BlockSpec(shape, lambda *_: (0,) * nd, pipeline_mode=pl.Buffered(1))


def _layer_resident(layer, shape):
    nd = len(shape)
    return pl.BlockSpec((None,) + tuple(shape), lambda *_: (layer,) + (0,) * nd,
                        pipeline_mode=pl.Buffered(1))


def _params(n_grid, vmem_mb):
    return pltpu.CompilerParams(
        dimension_semantics=("parallel",) * n_grid,
        vmem_limit_bytes=vmem_mb * 1024 * 1024)


@functools.lru_cache(maxsize=None)
def _rope_tables(n):
    rows = n // GRID_W
    row = np.repeat(np.arange(rows, dtype=np.float64), GRID_W)
    col = np.tile(np.arange(GRID_W, dtype=np.float64), rows)
    n_freq = HEAD_DIM // 4
    inv_freq = ROPE_THETA ** (-np.arange(n_freq, dtype=np.float64) / n_freq)
    ang = np.concatenate([row[:, None] * inv_freq, col[:, None] * inv_freq], axis=-1)
    cos = np.repeat(np.cos(ang), 2, axis=1)
    sin = np.repeat(np.sin(ang), 2, axis=1)
    sign = np.tile(np.array([-1.0, 1.0]), HEAD_DIM // 2)
    cos2 = np.tile(cos, (1, 2)).astype(np.float32)
    sin2 = np.tile(sin * sign, (1, 2)).astype(np.float32)
    return cos2, sin2


@functools.lru_cache(maxsize=None)
def _head_sum_matrix():
    idx = np.arange(W_QK) // HEAD_DIM
    return np.asarray((idx[:, None] == idx[None, :]).astype(np.float32), dtype=BF16)


@functools.lru_cache(maxsize=None)
def _channel_dft():
    c = np.arange(W_FOURIER)
    same = (c[:, None] // HEAD_DIM) == (c[None, :] // HEAD_DIM)
    ang = 2.0 * np.pi * ((c[:, None] % HEAD_DIM) * (c[None, :] % HEAD_DIM)) / HEAD_DIM
    cr = np.where(same, np.cos(ang), 0.0) / 8.0
    ci = np.where(same, -np.sin(ang), 0.0) / 8.0
    return np.concatenate([cr, ci], axis=1).astype(np.float32)


@functools.lru_cache(maxsize=None)
def _seq_dft_tables(n):
    n1, n2 = DFT_N1, n // DFT_N1
    t1 = np.arange(n1)[:, None, None]
    k2 = np.arange(n2)[None, :, None]
    t2 = np.arange(n2)[None, None, :]
    theta = 2.0 * np.pi * ((k2 * (t1 + n1 * t2)) % n) / n
    dr = np.cos(theta) / np.sqrt(n2)
    di = -np.sin(theta) / np.sqrt(n2)
    tab_r = np.concatenate([dr, di], axis=1)
    tab_i = np.concatenate([-di, dr], axis=1)
    k1 = np.arange(n1)[:, None]
    tt = np.arange(n1)[None, :]
    phi = 2.0 * np.pi * ((k1 * tt) % n1) / n1
    tab_c = np.concatenate([np.cos(phi), np.sin(phi)], axis=1) / np.sqrt(n1)
    return tab_r.astype(np.float32), tab_i.astype(np.float32), tab_c.astype(np.float32)


@functools.lru_cache(maxsize=None)
def _ctx_dft_tables(n):
    k = np.arange(n)[:, None]
    t = np.arange(n)[None, :]
    ang = 2.0 * np.pi * ((k * t) % n) / n
    s = 1.0 / np.sqrt(n)
    return (np.cos(ang) * s).astype(np.float32), (np.sin(ang) * s).astype(np.float32)


def _mod_kernel(c_ref, w_ref, b_ref, o_ref):
    c = c_ref[...]
    sc = c * _sigmoid(c)
    a_hi, a_lo = _split_bf16(sc)
    w_hi, w_lo = _split_bf16(w_ref[0])
    o_ref[0] = _dot(a_hi, w_hi) + _dot(a_lo, w_hi) + _dot(a_hi, w_lo) + b_ref[0]


def _modulation(cc, w_mod, b_mod):
    depth, d, d6 = w_mod.shape
    tn = 1536
    return pl.pallas_call(
        _mod_kernel,
        grid=(depth, d6 // tn),
        in_specs=[
            pl.BlockSpec((V7X_SUBLANES, d), lambda l, j: (0, 0)),
            pl.BlockSpec((1, d, tn), lambda l, j: (l, 0, j)),
            pl.BlockSpec((1, 1, tn), lambda l, j: (l, 0, j)),
        ],
        out_specs=pl.BlockSpec((1, V7X_SUBLANES, tn), lambda l, j: (l, 0, j)),
        out_shape=jax.ShapeDtypeStruct((depth, V7X_SUBLANES, d6), F32),
        compiler_params=_params(2, 40),
        name="modulation",
    )(cc, w_mod, b_mod.reshape(depth, 1, d6))


INPROJ_ROWS = 256


def _rope(x, cos, sin):
    lane = lax.broadcasted_iota(jnp.int32, x.shape, 1)
    swapped = jnp.where((lane & 1) == 0,
                        pltpu.roll(x, V7X_LANES - 1, axis=1),
                        pltpu.roll(x, 1, axis=1))
    return x * cos + swapped * sin


def _inproj_kernel(rope, x_ref, mod_ref, g1_ref, w_ref, qkg_ref, bd_ref, cd_ref, *rest):
    if rope:
        cos_ref, sin_ref = rest[:2]
        rest = rest[2:]
    q_ref, k_ref, v_ref, gf_ref, a_ref, u_ref = rest
    tm = x_ref.shape[1]
    rg = min(tm, INPROJ_ROWS)
    gain = g1_ref[...] * (1.0 + mod_ref[MOD_SCALE1:MOD_SCALE1 + 1, :])
    lane = lax.broadcasted_iota(jnp.int32, (rg, V7X_LANES), 1)
    one_col = jnp.where(lane == HEAD_DIM, 1.0, 0.0)
    row = lax.broadcasted_iota(jnp.int32, (V_AUG - HEAD_DIM, rg), 0)
    ones_rows = jnp.where(row == 0, 1.0, 0.0).astype(BF16)
    for r0 in range(0, tm, rg):
        rows = slice(r0, r0 + rg)
        x = x_ref[0, rows, :]
        ms = jnp.mean(x * x, axis=-1, keepdims=True)
        xn = (x * lax.rsqrt(ms + EPS)) * gain + mod_ref[MOD_SHIFT1:MOD_SHIFT1 + 1, :]
        p = _dot(xn.astype(BF16), w_ref[...])

        qk = p[:, :W_QK]
        ss = _dot((qk * qk).astype(BF16), bd_ref[...])
        qkn = qk * lax.rsqrt(ss * (1.0 / HEAD_DIM) + EPS) * qkg_ref[...]
        slabs = []
        for j in range(W_QK // V7X_LANES):
            slab = qkn[:, j * V7X_LANES:(j + 1) * V7X_LANES]
            if rope:
                slab = _rope(slab, cos_ref[rows, :], sin_ref[rows, :])
            slabs.append(slab)
        for j in range(W_ATTN // V7X_LANES):
            qt = (slabs[j] * Q_SCALE_LOG2).T.astype(BF16)
            q_ref[0, 2 * j, :, rows] = qt[:HEAD_DIM]
            q_ref[0, 2 * j + 1, :, rows] = qt[HEAD_DIM:]
        k_ref[0, 0, rows, :] = jnp.where(lane < HEAD_DIM, slabs[2], one_col).astype(BF16)
        k_ref[0, 1, rows, :] = jnp.where(lane < HEAD_DIM, pltpu.roll(slabs[2], HEAD_DIM, axis=1),
                                         one_col).astype(BF16)
        vt = p[:, OFF_V:OFF_F].T.astype(BF16)
        for g in range(N_KV_HEADS):
            v_ref[0, g, 0:HEAD_DIM, rows] = vt[g * HEAD_DIM:(g + 1) * HEAD_DIM]
            v_ref[0, g, HEAD_DIM:, rows] = ones_rows
        gf_ref[0, rows, :] = _dot(p[:, OFF_F:OFF_C].astype(BF16), cd_ref[...]).astype(BF16)
        a_ref[0, rows, :] = p[:, OFF_C:OFF_P].astype(BF16)
        u_ref[0, rows, :] = p[:, OFF_P:].astype(BF16)


def _inproj(layer, x, mod6, mod_row, g1, w_in, qk_gain, rope, tm):
    bsz, n, d = x.shape
    nt = n // tm
    tok = lambda width: pl.BlockSpec((1, tm, width), lambda b, i: (b, i, 0))
    in_specs = [
        tok(d),
        pl.BlockSpec((None, None, N_MOD, d), lambda b, i: (layer, mod_row(b), 0, 0)),
        _layer_resident(layer, (1, d)),
        _layer_resident(layer, (d, D_IN)),
        _layer_resident(layer, (1, W_QK)),
        _resident((W_QK, W_QK)),
        _resident((W_FOURIER, 2 * W_FOURIER)),
    ]
    args = [x, mod6, g1, w_in, qk_gain, _head_sum_matrix(),
            jnp.asarray(_channel_dft()).astype(BF16)]
    if rope:
        cos2, sin2 = _rope_tables(n)
        in_specs += [pl.BlockSpec((tm, V7X_LANES), lambda b, i: (i, 0))] * 2
        args += [cos2, sin2]
    widths = (2 * W_FOURIER, 2 * W_CONV, W_POOL)
    out_specs = [
        pl.BlockSpec((1, N_Q_HEADS, HEAD_DIM, tm), lambda b, i: (b, 0, 0, i)),
        pl.BlockSpec((1, N_KV_HEADS, tm, K_AUG), lambda b, i: (b, 0, i, 0)),
        pl.BlockSpec((1, N_KV_HEADS, V_AUG, tm), lambda b, i: (b, 0, 0, i)),
    ] + [tok(w) for w in widths]
    out_shape = [
        jax.ShapeDtypeStruct((bsz, N_Q_HEADS, HEAD_DIM, n), BF16),
        jax.ShapeDtypeStruct((bsz, N_KV_HEADS, n, K_AUG), BF16),
        jax.ShapeDtypeStruct((bsz, N_KV_HEADS, V_AUG, n), BF16),
    ] + [jax.ShapeDtypeStruct((bsz, n, w), BF16) for w in widths]
    return pl.pallas_call(
        functools.partial(_inproj_kernel, rope),
        grid=(bsz, nt),
        in_specs=in_specs,
        out_specs=out_specs,
        out_shape=out_shape,
        compiler_params=_params(2, 40),
        name="inproj_rope" if rope else "inproj_ctx",
    )(*args)


K_AUG = V7X_LANES
V_AUG = HEAD_DIM + V7X_BF16_ROWS
SAFE_SHIFT_LOG2 = 50.0


def _attn_kernel(chunks, q_ref, *refs):
    n_src = (len(refs) - 3) // 2
    k_refs, v_refs = refs[:n_src], refs[n_src:2 * n_src]
    o_ref, kmax_ref, acc_ref = refs[2 * n_src:]
    n_chunks = len(chunks)
    tq = q_ref.shape[-1]
    i = pl.program_id(2)

    def keys(c):
        src, start, size = chunks[c]
        return k_refs[src][0, 0, start:start + size, :]

    def values(c):
        src, start, size = chunks[c]
        return v_refs[src][0, 0, :, start:start + size]

    @pl.when(i == 0)
    def _():
        ones = jnp.ones((K_AUG, K_AUG), BF16)
        mx = jnp.zeros((1, K_AUG), F32)
        for c in range(n_chunks):
            kf = keys(c).astype(F32)
            norms = _dot((kf * kf).astype(BF16), ones)
            mx = jnp.maximum(mx, jnp.max(norms, axis=0, keepdims=True))
        kmax_ref[...] = jnp.sqrt(mx)

    q2 = jnp.concatenate([q_ref[0, 0], q_ref[0, 1]], axis=1)
    qf = q2.astype(F32)
    kmax = jnp.tile(kmax_ref[...], (1, 2 * tq // K_AUG))
    bound = jnp.sqrt(jnp.sum(qf * qf, axis=0, keepdims=True)) * kmax
    safe = jnp.max(bound) <= SAFE_SHIFT_LOG2
    row = lax.broadcasted_iota(jnp.int32, (K_AUG - HEAD_DIM, 2 * tq), 0)
    shift_rows = jnp.where(row == 0, -bound, 0.0).astype(BF16)
    q_aug = jnp.concatenate([q2, shift_rows], axis=0)

    def logits(c):
        return _dot(keys(c), q_aug)

    @pl.when(safe)
    def _():
        acc = None
        for c in range(n_chunks):
            p = jnp.exp2(logits(c)).astype(BF16)
            part = _dot(values(c), p)
            acc = part if acc is None else acc + part
        acc_ref[...] = acc

    @pl.when(jnp.logical_not(safe))
    def _():
        m = jnp.full((1, 2 * tq), NEG_BIG, F32)
        acc = jnp.zeros((V_AUG, 2 * tq), F32)
        for c in range(n_chunks):
            s = logits(c)
            m_new = jnp.maximum(m, jnp.max(s, axis=0, keepdims=True))
            p = jnp.exp2(s - m_new).astype(BF16)
            acc = jnp.exp2(m - m_new) * acc + _dot(values(c), p)
            m = m_new
        acc_ref[...] = acc

    o = acc_ref[0:HEAD_DIM, :] * (1.0 / acc_ref[HEAD_DIM:HEAD_DIM + 1, :])
    o2 = jnp.concatenate([o[:, :tq], o[:, tq:]], axis=0)
    o_ref[0] = o2.T.astype(BF16)


def _attention(q_t, kv_sources, tq, tk):
    bsz, _, _, nq = q_t.shape
    q_per_kv = N_Q_HEADS // N_KV_HEADS
    chunks = []
    for src, (k_aug, _) in enumerate(kv_sources):
        nk = k_aug.shape[2]
        step = min(tk, nk)
        chunks += [(src, start, step) for start in range(0, nk, step)]
    k_specs = [pl.BlockSpec((1, 1, k.shape[2], K_AUG), lambda b, g, i: (b, g, 0, 0))
               for k, _ in kv_sources]
    v_specs = [pl.BlockSpec((1, 1, V_AUG, v.shape[3]), lambda b, g, i: (b, g, 0, 0))
               for _, v in kv_sources]
    return pl.pallas_call(
        functools.partial(_attn_kernel, tuple(chunks)),
        grid=(bsz, N_KV_HEADS, nq // tq),
        in_specs=[pl.BlockSpec((1, q_per_kv, HEAD_DIM, tq), lambda b, g, i: (b, g, 0, i))]
        + k_specs + v_specs,
        out_specs=pl.BlockSpec((1, tq, q_per_kv * HEAD_DIM), lambda b, g, i: (b, i, g)),
        out_shape=jax.ShapeDtypeStruct((bsz, nq, W_ATTN), BF16),
        scratch_shapes=[pltpu.VMEM((1, K_AUG), F32),
                        pltpu.VMEM((V_AUG, q_per_kv * tq), F32)],
        compiler_params=pltpu.CompilerParams(
            dimension_semantics=("parallel", "parallel", "arbitrary"),
            vmem_limit_bytes=40 * 1024 * 1024),
        name="attention_%dsrc" % len(kv_sources),
    )(q_t, *[k for k, _ in kv_sources], *[v for _, v in kv_sources])


DFT_A_GROUP = 8


def _dft_a_kernel(bsz, g_ref, tr_ref, ti_ref, o_ref):
    n2 = g_ref.shape[2]
    for j in range(DFT_A_GROUP):
        tab_r = tr_ref[j].astype(BF16)
        tab_i = ti_ref[j].astype(BF16)
        for b in range(bsz):
            res = (_dot(tab_r, g_ref[j, b, :, :W_FOURIER])
                   + _dot(tab_i, g_ref[j, b, :, W_FOURIER:]))
            o_ref[0, j, :, b * W_FOURIER:(b + 1) * W_FOURIER] = res[:n2].astype(BF16)
            o_ref[1, j, :, b * W_FOURIER:(b + 1) * W_FOURIER] = res[n2:].astype(BF16)


def _dft_c_kernel(a_ref, tc_ref, o_ref):
    o_ref[...] = _dot(tc_ref[...], a_ref[...]).astype(BF16)


def _fourier_latent(gf):
    bsz, n, _ = gf.shape
    n1, n2 = DFT_N1, n // DFT_N1
    tab_r, tab_i, tab_c = _seq_dft_tables(n)
    g_t = jnp.transpose(gf.reshape(bsz, n2, n1, 2 * W_FOURIER), (2, 0, 1, 3))
    lanes = bsz * W_FOURIER
    a = pl.pallas_call(
        functools.partial(_dft_a_kernel, bsz),
        grid=(n1 // DFT_A_GROUP,),
        in_specs=[
            pl.BlockSpec((DFT_A_GROUP, bsz, n2, 2 * W_FOURIER), lambda t: (t, 0, 0, 0)),
            pl.BlockSpec((DFT_A_GROUP, 2 * n2, n2), lambda t: (t, 0, 0)),
            pl.BlockSpec((DFT_A_GROUP, 2 * n2, n2), lambda t: (t, 0, 0)),
        ],
        out_specs=pl.BlockSpec((2, DFT_A_GROUP, n2, lanes), lambda t: (0, t, 0, 0)),
        out_shape=jax.ShapeDtypeStruct((2, n1, n2, lanes), BF16),
        compiler_params=_params(1, 40),
        name="dft_stage_a",
    )(g_t, tab_r, tab_i)
    a2 = a.reshape(2 * n1, n2 * lanes)
    tl = 4096
    y = pl.pallas_call(
        _dft_c_kernel,
        grid=(n2 * lanes // tl,),
        in_specs=[pl.BlockSpec((2 * n1, tl), lambda j: (0, j)), _resident((n1, 2 * n1))],
        out_specs=pl.BlockSpec((n1, tl), lambda j: (0, j)),
        out_shape=jax.ShapeDtypeStruct((n1, n2 * lanes), BF16),
        compiler_params=_params(1, 40),
        name="dft_stage_c",
    )(a2, jnp.asarray(tab_c).astype(BF16))
    return y.reshape(n, lanes)


def _dft_ctx_kernel(g_ref, c_ref, s_ref, o_ref):
    o_ref[...] = (_dot(c_ref[...], g_ref[0, :, :W_FOURIER])
                  + _dot(s_ref[...], g_ref[0, :, W_FOURIER:])).astype(BF16)


def _fourier_ctx(gf):
    bsz, n, _ = gf.shape
    tab_cos, tab_sin = _ctx_dft_tables(n)
    return pl.pallas_call(
        _dft_ctx_kernel,
        grid=(bsz,),
        in_specs=[pl.BlockSpec((1, n, 2 * W_FOURIER), lambda b: (b, 0, 0)),
                  _resident((n, n)), _resident((n, n))],
        out_specs=pl.BlockSpec((n, W_FOURIER), lambda b: (0, b)),
        out_shape=jax.ShapeDtypeStruct((n, bsz * W_FOURIER), BF16),
        compiler_params=_params(1, 40),
        name="dft_ctx",
    )(gf, jnp.asarray(tab_cos).astype(BF16), jnp.asarray(tab_sin).astype(BF16))


CONV_ROWS = 64
CONV_TAP_ROWS = 32
FFN_CHUNKS = ((0, 768), (768, 768), (1536, 768), (2304, 512))


def _exact_zero(v):
    return jnp.minimum(jnp.abs(v), 0.0)


def _conv_pool_units(tile, n_tiles, n_seq, tm, am_ref, ap_ref, an_ref, um_ref, up_ref, un_ref,
                     dww_ref, cvec_ref, g_ref, gsh_ref, pu_ref, psh_ref, cv_ref, pl_ref):
    first = tile == 0
    last = tile == n_tiles - 1
    rows_sh = gsh_ref.shape[1]
    corner = lambda v: v[0:V7X_SUBLANES, 0:V7X_LANES]

    def gated(v, gate):
        if gate is None:
            return v
        rows, cols = v.shape
        return v + jnp.tile(gate, (rows // V7X_SUBLANES, cols // V7X_LANES))

    def glu(a_blk):
        a = a_blk.astype(F32)
        return a[:, :W_CONV] * _sigmoid(a[:, W_CONV:])

    def stage(gate):
        main = gated(glu(am_ref[0]), gate)
        g_ref[0:HALO] = jnp.where(first, 0.0, glu(ap_ref[0]))
        g_ref[HALO:HALO + tm] = main
        g_ref[HALO + tm:] = jnp.where(last, 0.0, glu(an_ref[0]))
        pu_ref[0:HALO] = jnp.where(first, 0.0, up_ref[0].astype(F32))
        pu_ref[HALO:HALO + tm] = um_ref[0].astype(F32)
        pu_ref[HALO + tm:] = jnp.where(last, 0.0, un_ref[0].astype(F32))
        return corner(main)

    def copy(b):
        def run(gate):
            gv = gated(g_ref[pl.ds(b, rows_sh), :], gate)
            gsh_ref[b - 1] = gv
            psh_ref[b - 1] = gated(pu_ref[pl.ds(b, rows_sh), :], gate)
            return corner(gv)
        return run

    lane = lax.broadcasted_iota(jnp.int32, (CONV_ROWS, W_POOL), 1)
    grp = lane // POOL_GROUP
    half = jnp.left_shift(1, grp)
    row = lax.broadcasted_iota(jnp.int32, (CONV_ROWS, W_POOL), 0)

    def shifted(base_ref, sh_ref, c0, j):
        src = base_ref if j % V7X_SUBLANES == 0 else sh_ref.at[j % V7X_SUBLANES - 1]
        return src[pl.ds(c0 + (j // V7X_SUBLANES) * V7X_SUBLANES, CONV_ROWS), :]

    def conv(c0):
        def run(gate):
            wts = gated(dww_ref[...], gate)
            acc = jnp.broadcast_to(cvec_ref[0:1, :], (CONV_ROWS, W_CONV))
            for k in range(CONV_WIDTH):
                acc = acc + wts[k:k + 1, :] * shifted(g_ref, gsh_ref, c0, k + 1)
            mu = jnp.mean(acc, axis=-1, keepdims=True)
            yc = acc - mu
            var = jnp.mean(yc * yc, axis=-1, keepdims=True)
            yn = yc * lax.rsqrt(var + EPS) * cvec_ref[1:2, :] + cvec_ref[2:3, :]
            act = yn * _sigmoid(yn)
            cv_ref[c0:c0 + CONV_ROWS, :] = act.astype(BF16)
            return corner(act)
        return run

    def pool(c0):
        def run(gate):
            ld = lambda d: shifted(pu_ref, psh_ref, c0, HALO + d)
            u0 = ld(0)
            s2 = gated(ld(-1) + u0, gate)
            s4 = s2 + ld(-2) + ld(1)
            s8 = s4 + ld(-4) + ld(-3) + ld(2) + ld(3)
            s16 = s8 + ld(-8) + ld(-7) + ld(-6) + ld(-5) + ld(4) + ld(5) + ld(6) + ld(7)
            win = jnp.where(grp == 0, s2, jnp.where(grp == 1, s4, jnp.where(grp == 2, s8, s16)))
            t = tile * tm + c0 + row
            cnt = jnp.minimum(t + half, n_seq) - jnp.maximum(t - half, 0)
            centred = win / cnt.astype(F32) - u0
            pl_ref[c0:c0 + CONV_ROWS, :] = centred.astype(BF16)
            return corner(centred)
        return run

    units = [stage] + [copy(b) for b in range(1, V7X_SUBLANES)]
    for c in range(tm // CONV_ROWS):
        units += [conv(c * CONV_ROWS), pool(c * CONV_ROWS)]
    return units


def _mixffn_kernel(n_tiles, n_seq, x_ref, attn_ref, yf_ref,
                   amc_ref, apc_ref, anc_ref, umc_ref, upc_ref, unc_ref,
                   amn_ref, apn_ref, ann_ref, umn_ref, upn_ref, unn_ref,
                   mod_ref, g2_ref, wf_ref, dww_ref, cvec_ref, wpw_ref, wpl_ref, wo_ref,
                   wfi_ref, wfo_ref, o_ref, g_ref, gsh_ref, pu_ref, psh_ref, cv_ref, pl_ref):
    tm = x_ref.shape[1]
    s = pl.program_id(0)
    n_steps = pl.num_programs(0)
    scratch = (dww_ref, cvec_ref, g_ref, gsh_ref, pu_ref, psh_ref, cv_ref, pl_ref)

    @pl.when(s == 0)
    def _():
        for unit in _conv_pool_units(0, n_tiles, n_seq, tm, amc_ref, apc_ref, anc_ref, umc_ref,
                                     upc_ref, unc_ref, *scratch):
            unit(None)

    cv = cv_ref[...]
    pooled = pl_ref[...]
    nxt_tile = jnp.minimum(s + 1, n_steps - 1) % n_tiles
    units = _conv_pool_units(nxt_tile, n_tiles, n_seq, tm, amn_ref, apn_ref, ann_ref, umn_ref,
                             upn_ref, unn_ref, *scratch)
    n_pre = V7X_SUBLANES
    slots = [units[:n_pre // 2], units[n_pre // 2:n_pre]]
    slots += [units[i:i + 2] for i in range(n_pre, len(units), 2)]

    def run_slot(gate_src):
        if not slots:
            return []
        gate = None if gate_src is None else _exact_zero(
            gate_src[-V7X_SUBLANES:, -V7X_LANES:])
        return [unit(gate) for unit in slots.pop(0)]

    def tied(lhs_f32, tokens):
        lhs = lhs_f32.astype(BF16)
        if not tokens:
            return lhs
        zero = sum(_exact_zero(t) for t in tokens)
        head = lhs_f32[0:V7X_BF16_ROWS] + jnp.tile(
            zero, (V7X_BF16_ROWS // V7X_SUBLANES, lhs_f32.shape[1] // V7X_LANES))
        return jnp.concatenate([head.astype(BF16), lhs[V7X_BF16_ROWS:]], axis=0)

    tokens = run_slot(None)
    four = _dot(yf_ref[...], wf_ref[...]).astype(BF16)
    conv = _dot(cv, wpw_ref[...]).astype(BF16)
    pool = (_dot(pooled, wpl_ref[...]) * cvec_ref[3:4, :]).astype(BF16)
    mix = (_dot(attn_ref[0], wo_ref[0:W_ATTN, :])
           + _dot(four, wo_ref[W_ATTN:2 * W_ATTN, :])
           + _dot(conv, wo_ref[2 * W_ATTN:3 * W_ATTN, :])
           + _dot(pool, wo_ref[3 * W_ATTN:, :]))
    x1 = x_ref[0] + mod_ref[MOD_GATE1:MOD_GATE1 + 1, :] * mix

    ms = jnp.mean(x1 * x1, axis=-1, keepdims=True)
    gain = g2_ref[...] * (1.0 + mod_ref[MOD_SCALE2:MOD_SCALE2 + 1, :])
    xn_f32 = (x1 * lax.rsqrt(ms + EPS)) * gain + mod_ref[MOD_SHIFT2:MOD_SHIFT2 + 1, :]
    prev = mix
    acc = None
    for lo, size in FFN_CHUNKS:
        lhs = tied(xn_f32, tokens)
        tokens = run_slot(prev)
        a = _dot(lhs, wfi_ref[:, lo:lo + size])
        lhs = tied(xn_f32, tokens)
        tokens = run_slot(a)
        g = _dot(lhs, wfi_ref[:, D_FF + lo:D_FF + lo + size])
        lhs = tied((a * _sigmoid(a)) * g, tokens)
        tokens = run_slot(g)
        part = _dot(lhs, wfo_ref[lo:lo + size, :])
        prev = part
        acc = part if acc is None else acc + part
    assert not slots
    out = x1 + mod_ref[MOD_GATE2:MOD_GATE2 + 1, :] * acc
    if tokens:
        zero = sum(_exact_zero(t) for t in tokens)
        out = out + jnp.tile(zero, (tm // V7X_SUBLANES, out.shape[1] // V7X_LANES))
    o_ref[0] = out


def _mix_ffn(layer, x, attn, yf, a_conv, u_pool, mod6, mod_row, g2, w_f, dw_w, cvec, w_pw,
             w_pool_bd, w_o, w_fi, w_fo, tm):
    bsz, n, d = x.shape
    nt = n // tm
    n_steps = bsz * nt
    hb = tm // HALO
    last_halo = n // HALO - 1

    def nxt(s):
        s1 = jnp.minimum(s + 1, n_steps - 1)
        return s1 // nt, s1 % nt

    tok = lambda width: pl.BlockSpec((1, tm, width), lambda s: (s // nt, s % nt, 0))
    cur_main = lambda w: pl.BlockSpec((1, tm, w), lambda s: (0, 0, 0))
    cur_prev = lambda w: pl.BlockSpec((1, HALO, w), lambda s: (0, 0, 0))
    cur_next = lambda w: pl.BlockSpec((1, HALO, w), lambda s: (0, min(hb, last_halo), 0))
    nxt_main = lambda w: pl.BlockSpec((1, tm, w), lambda s: (nxt(s)[0], nxt(s)[1], 0))
    nxt_prev = lambda w: pl.BlockSpec(
        (1, HALO, w), lambda s: (nxt(s)[0], jnp.maximum(nxt(s)[1] * hb - 1, 0), 0))
    nxt_next = lambda w: pl.BlockSpec(
        (1, HALO, w), lambda s: (nxt(s)[0], jnp.minimum((nxt(s)[1] + 1) * hb, last_halo), 0))
    rows_sh = tm + 2 * HALO - V7X_SUBLANES
    return pl.pallas_call(
        functools.partial(_mixffn_kernel, nt, n),
        grid=(n_steps,),
        in_specs=[
            tok(d), tok(W_ATTN),
            pl.BlockSpec((tm, W_FOURIER), lambda s: (s % nt, s // nt)),
            cur_main(2 * W_CONV), cur_prev(2 * W_CONV), cur_next(2 * W_CONV),
            cur_main(W_POOL), cur_prev(W_POOL), cur_next(W_POOL),
            nxt_main(2 * W_CONV), nxt_prev(2 * W_CONV), nxt_next(2 * W_CONV),
            nxt_main(W_POOL), nxt_prev(W_POOL), nxt_next(W_POOL),
            pl.BlockSpec((None, None, N_MOD, d), lambda s: (layer, mod_row(s // nt), 0, 0)),
            _layer_resident(layer, (1, d)),
            _layer_resident(layer, (W_FOURIER, W_FOURIER)),
            _layer_resident(layer, (CONV_TAP_ROWS, W_CONV)),
            _layer_resident(layer, (V7X_SUBLANES, W_CONV)),
            _layer_resident(layer, (W_CONV, W_CONV)),
            _layer_resident(layer, (W_POOL, W_POOL)),
            _layer_resident(layer, (4 * W_ATTN, d)),
            _layer_resident(layer, (d, 2 * D_FF)),
            _layer_resident(layer, (D_FF, d)),
        ],
        out_specs=tok(d),
        out_shape=jax.ShapeDtypeStruct((bsz, n, d), F32),
        scratch_shapes=[
            pltpu.VMEM((tm + 2 * HALO, W_CONV), F32),
            pltpu.VMEM((V7X_SUBLANES - 1, rows_sh, W_CONV), F32),
            pltpu.VMEM((tm + 2 * HALO, W_POOL), F32),
            pltpu.VMEM((V7X_SUBLANES - 1, rows_sh, W_POOL), F32),
            pltpu.VMEM((tm, W_CONV), BF16),
            pltpu.VMEM((tm, W_POOL), BF16),
        ],
        compiler_params=pltpu.CompilerParams(
            dimension_semantics=("arbitrary",), vmem_limit_bytes=56 * 1024 * 1024),
        name="mix_ffn_n%d" % n,
    )(x, attn, yf, a_conv, a_conv, a_conv, u_pool, u_pool, u_pool,
      a_conv, a_conv, a_conv, u_pool, u_pool, u_pool, mod6, g2,
      w_f, dw_w, cvec, w_pw, w_pool_bd, w_o, w_fi, w_fo)


def _pool_block_diag(w_pool):
    depth, groups = w_pool.shape[:2]
    eye = jnp.eye(groups, dtype=w_pool.dtype)
    blocks = w_pool[:, :, :, None, :] * eye[None, :, None, :, None]
    return blocks.reshape(depth, W_POOL, W_POOL)


def kernel(x, c, ctx, c_ctx, w_mod, b_mod, g_norm1, g_norm2, w_in, q_norm_g, k_norm_g,
           w_fourier, conv_dw_w, conv_dw_b, conv_ln_g, conv_ln_b, w_conv_pw, w_pool,
           pool_scale, w_out, w_ffn_in, w_ffn_out):
    bsz, n, d = x.shape
    n_ctx = ctx.shape[1]
    depth = w_in.shape[0]
    tm_lat, tm_ctx = 512, n_ctx
    tq_lat, tk_lat = 512, 1024

    cc = jnp.concatenate([c, c_ctx[None, :],
                          jnp.zeros((V7X_SUBLANES - bsz - 1, d), F32)], axis=0)
    mod6 = _modulation(cc, w_mod, b_mod).reshape(depth, V7X_SUBLANES, N_MOD, d)
    lat_row = lambda b: b
    ctx_row = lambda b: bsz

    w_in_b = w_in.astype(BF16)
    w_f = w_fourier.astype(BF16)
    w_pw = w_conv_pw.astype(BF16)
    w_pl = _pool_block_diag(w_pool).astype(BF16)
    w_o = w_out.astype(BF16)
    w_fi = w_ffn_in.astype(BF16)
    w_fo = w_ffn_out.astype(BF16)
    qk_gain = jnp.concatenate([jnp.tile(q_norm_g, (1, N_Q_HEADS)),
                               jnp.tile(k_norm_g, (1, N_KV_HEADS))], axis=1)[:, None, :]
    g1 = g_norm1[:, None, :]
    g2 = g_norm2[:, None, :]
    dw_taps = jnp.concatenate(
        [conv_dw_w, jnp.zeros((depth, CONV_TAP_ROWS - CONV_WIDTH, W_CONV), F32)], axis=1)
    cvec = jnp.concatenate(
        [conv_dw_b[:, None], conv_ln_g[:, None], conv_ln_b[:, None], pool_scale[:, None],
         jnp.zeros((depth, V7X_SUBLANES - 4, W_CONV), F32)], axis=1)

    h = ctx
    for l in range(DEPTH):
        last = l == DEPTH - 1
        qc, kc, vc, gfc, ac, uc = _inproj(l, h, mod6, ctx_row, g1, w_in_b, qk_gain, False, tm_ctx)
        q, k, v, gf, a, u = _inproj(l, x, mod6, lat_row, g1, w_in_b, qk_gain, True, tm_lat)

        attn = _attention(q, [(kc, vc), (k, v)], tq_lat, tk_lat)
        yf = _fourier_latent(gf)
        x_new = _mix_ffn(l, x, attn, yf, a, u, mod6, lat_row, g2, w_f, dw_taps, cvec, w_pw, w_pl,
                         w_o, w_fi, w_fo, tm_lat)

        if not last:
            attn_c = _attention(qc, [(kc, vc)], n_ctx, n_ctx)
            yfc = _fourier_ctx(gfc)
            h = _mix_ffn(l, h, attn_c, yfc, ac, uc, mod6, ctx_row, g2, w_f, dw_taps, cvec, w_pw,
                         w_pl, w_o, w_fi, w_fo, tm_ctx)
        x = x_new
    return x
```

```python
import functools

import numpy as np
import jax
import jax.numpy as jnp
from jax import lax
from jax.experimental import pallas as pl
from jax.experimental.pallas import tpu as pltpu

F32 = jnp.float32
BF16 = jnp.bfloat16

D_MODEL = 1024
DEPTH = 2
GRID_W = 64
HEAD_DIM = 64
N_Q_HEADS = 4
N_KV_HEADS = 2
W_ATTN = 256
W_FOURIER = 256
W_CONV = 256
W_POOL = 256
CONV_WIDTH = 31
POOL_GROUP = 64
D_FF = 2816
ROPE_THETA = 10000.0
EPS = 1e-6
ATTN_SCALE = HEAD_DIM ** -0.5
Q_SCALE_LOG2 = ATTN_SCALE * float(np.log2(np.e))
W_QK = W_ATTN + N_KV_HEADS * HEAD_DIM
OFF_V = W_QK
OFF_F = OFF_V + N_KV_HEADS * HEAD_DIM
OFF_C = OFF_F + W_FOURIER
OFF_P = OFF_C + 2 * W_CONV
D_IN = OFF_P + W_POOL

V7X_LANES = 128
V7X_SUBLANES = 8
V7X_BF16_ROWS = 16
V7X_VMEM_BYTES = 64 * 1024 * 1024

MOD_SHIFT1, MOD_SCALE1, MOD_GATE1, MOD_SHIFT2, MOD_SCALE2, MOD_GATE2 = range(6)
N_MOD = 6

HALO = V7X_BF16_ROWS
DFT_N1 = 64
NEG_BIG = -1e30


def _dot(a, b):
    return jnp.dot(a, b, preferred_element_type=F32)


def _sigmoid(x):
    return 1.0 / (1.0 + jnp.exp(-x))


def _split_bf16(x):
    hi = x.astype(BF16)
    lo = (x - hi.astype(F32)).astype(BF16)
    return hi, lo


def _resident(shape):
    nd = len(shape)
    return pl.BlockSpec(shape, lambda *_: (0,) * nd, pipeline_mode=pl.Buffered(1))


def _layer_resident(layer, shape):
    nd = len(shape)
    return pl.BlockSpec((None,) + tuple(shape), lambda *_: (layer,) + (0,) * nd,
                        pipeline_mode=pl.Buffered(1))


def _params(n_grid, vmem_mb):
    return pltpu.CompilerParams(
        dimension_semantics=("parallel",) * n_grid,
        vmem_limit_bytes=vmem_mb * 1024 * 1024)


@functools.lru_cache(maxsize=None)
def _rope_tables(n):
    rows = n // GRID_W
    row = np.repeat(np.arange(rows, dtype=np.float64), GRID_W)
    col = np.tile(np.arange(GRID_W, dtype=np.float64), rows)
    n_freq = HEAD_DIM // 4
    inv_freq = ROPE_THETA ** (-np.arange(n_freq, dtype=np.float64) / n_freq)
    ang = np.concatenate([row[:, None] * inv_freq, col[:, None] * inv_freq], axis=-1)
    cos = np.repeat(np.cos(ang), 2, axis=1)
    sin = np.repeat(np.sin(ang), 2, axis=1)
    sign = np.tile(np.array([-1.0, 1.0]), HEAD_DIM // 2)
    cos2 = np.tile(cos, (1, 2)).astype(np.float32)
    sin2 = np.tile(sin * sign, (1, 2)).astype(np.float32)
    return cos2, sin2


@functools.lru_cache(maxsize=None)
def _head_sum_matrix():
    idx = np.arange(W_QK) // HEAD_DIM
    return np.asarray((idx[:, None] == idx[None, :]).astype(np.float32), dtype=BF16)


@functools.lru_cache(maxsize=None)
def _channel_dft():
    c = np.arange(W_FOURIER)
    same = (c[:, None] // HEAD_DIM) == (c[None, :] // HEAD_DIM)
    ang = 2.0 * np.pi * ((c[:, None] % HEAD_DIM) * (c[None, :] % HEAD_DIM)) / HEAD_DIM
    cr = np.where(same, np.cos(ang), 0.0) / 8.0
    ci = np.where(same, -np.sin(ang), 0.0) / 8.0
    return np.concatenate([cr, ci], axis=1).astype(np.float32)


@functools.lru_cache(maxsize=None)
def _seq_dft_tables(n):
    n1, n2 = DFT_N1, n // DFT_N1
    t1 = np.arange(n1)[:, None, None]
    k2 = np.arange(n2)[None, :, None]
    t2 = np.arange(n2)[None, None, :]
    theta = 2.0 * np.pi * ((k2 * (t1 + n1 * t2)) % n) / n
    dr = np.cos(theta) / np.sqrt(n2)
    di = -np.sin(theta) / np.sqrt(n2)
    tab_r = np.concatenate([dr, di], axis=1)
    tab_i = np.concatenate([-di, dr], axis=1)
    k1 = np.arange(n1)[:, None]
    tt = np.arange(n1)[None, :]
    phi = 2.0 * np.pi * ((k1 * tt) % n1) / n1
    tab_c = np.concatenate([np.cos(phi), np.sin(phi)], axis=1) / np.sqrt(n1)
    return tab_r.astype(np.float32), tab_i.astype(np.float32), tab_c.astype(np.float32)


@functools.lru_cache(maxsize=None)
def _ctx_dft_tables(n):
    k = np.arange(n)[:, None]
    t = np.arange(n)[None, :]
    ang = 2.0 * np.pi * ((k * t) % n) / n
    s = 1.0 / np.sqrt(n)
    return (np.cos(ang) * s).astype(np.float32), (np.sin(ang) * s).astype(np.float32)


def _mod_kernel(c_ref, w_ref, b_ref, o_ref):
    c = c_ref[...]
    sc = c * _sigmoid(c)
    a_hi, a_lo = _split_bf16(sc)
    w_hi, w_lo = _split_bf16(w_ref[0])
    o_ref[0] = _dot(a_hi, w_hi) + _dot(a_lo, w_hi) + _dot(a_hi, w_lo) + b_ref[0]


def _modulation(cc, w_mod, b_mod):
    depth, d, d6 = w_mod.shape
    tn = 1536
    return pl.pallas_call(
        _mod_kernel,
        grid=(depth, d6 // tn),
        in_specs=[
            pl.BlockSpec((V7X_SUBLANES, d), lambda l, j: (0, 0)),
            pl.BlockSpec((1, d, tn), lambda l, j: (l, 0, j)),
            pl.BlockSpec((1, 1, tn), lambda l, j: (l, 0, j)),
        ],
        out_specs=pl.BlockSpec((1, V7X_SUBLANES, tn), lambda l, j: (l, 0, j)),
        out_shape=jax.ShapeDtypeStruct((depth, V7X_SUBLANES, d6), F32),
        compiler_params=_params(2, 40),
        name="modulation",
    )(cc, w_mod, b_mod.reshape(depth, 1, d6))


INPROJ_ROWS = 256


def _rope(x, cos, sin):
    lane = lax.broadcasted_iota(jnp.int32, x.shape, 1)
    swapped = jnp.where((lane & 1) == 0,
                        pltpu.roll(x, V7X_LANES - 1, axis=1),
                        pltpu.roll(x, 1, axis=1))
    return x * cos + swapped * sin


def _inproj_kernel(rope, x_ref, mod_ref, g1_ref, w_ref, qkg_ref, bd_ref, cd_ref, *rest):
    if rope:
        cos_ref, sin_ref = rest[:2]
        rest = rest[2:]
    q_ref, k_ref, v_ref, gf_ref, a_ref, u_ref = rest
    tm = x_ref.shape[1]
    rg = min(tm, INPROJ_ROWS)
    gain = g1_ref[...] * (1.0 + mod_ref[MOD_SCALE1:MOD_SCALE1 + 1, :])
    lane = lax.broadcasted_iota(jnp.int32, (rg, V7X_LANES), 1)
    one_col = jnp.where(lane == HEAD_DIM, 1.0, 0.0)
    row = lax.broadcasted_iota(jnp.int32, (V_AUG - HEAD_DIM, rg), 0)
    ones_rows = jnp.where(row == 0, 1.0, 0.0).astype(BF16)
    for r0 in range(0, tm, rg):
        rows = slice(r0, r0 + rg)
        x = x_ref[0, rows, :]
        ms = jnp.mean(x * x, axis=-1, keepdims=True)
        xn = (x * lax.rsqrt(ms + EPS)) * gain + mod_ref[MOD_SHIFT1:MOD_SHIFT1 + 1, :]
        p = _dot(xn.astype(BF16), w_ref[...])

        qk = p[:, :W_QK]
        ss = _dot((qk * qk).astype(BF16), bd_ref[...])
        qkn = qk * lax.rsqrt(ss * (1.0 / HEAD_DIM) + EPS) * qkg_ref[...]
        slabs = []
        for j in range(W_QK // V7X_LANES):
            slab = qkn[:, j * V7X_LANES:(j + 1) * V7X_LANES]
            if rope:
                slab = _rope(slab, cos_ref[rows, :], sin_ref[rows, :])
            slabs.append(slab)
        for j in range(W_ATTN // V7X_LANES):
            qt = (slabs[j] * Q_SCALE_LOG2).T.astype(BF16)
            q_ref[0, 2 * j, :, rows] = qt[:HEAD_DIM]
            q_ref[0, 2 * j + 1, :, rows] = qt[HEAD_DIM:]
        k_ref[0, 0, rows, :] = jnp.where(lane < HEAD_DIM, slabs[2], one_col).astype(BF16)
        k_ref[0, 1, rows, :] = jnp.where(lane < HEAD_DIM, pltpu.roll(slabs[2], HEAD_DIM, axis=1),
                                         one_col).astype(BF16)
        vt = p[:, OFF_V:OFF_F].T.astype(BF16)
        for g in range(N_KV_HEADS):
            v_ref[0, g, 0:HEAD_DIM, rows] = vt[g * HEAD_DIM:(g + 1) * HEAD_DIM]
            v_ref[0, g, HEAD_DIM:, rows] = ones_rows
        gf_ref[0, rows, :] = _dot(p[:, OFF_F:OFF_C].astype(BF16), cd_ref[...]).astype(BF16)
        a_ref[0, rows, :] = p[:, OFF_C:OFF_P].astype(BF16)
        u_ref[0, rows, :] = p[:, OFF_P:].astype(BF16)


def _inproj(layer, x, mod6, mod_row, g1, w_in, qk_gain, rope, tm):
    bsz, n, d = x.shape
    nt = n // tm
    tok = lambda width: pl.BlockSpec((1, tm, width), lambda b, i: (b, i, 0))
    in_specs = [
        tok(d),
        pl.BlockSpec((None, None, N_MOD, d), lambda b, i: (layer, mod_row(b), 0, 0)),
        _layer_resident(layer, (1, d)),
        _layer_resident(layer, (d, D_IN)),
        _layer_resident(layer, (1, W_QK)),
        _resident((W_QK, W_QK)),
        _resident((W_FOURIER, 2 * W_FOURIER)),
    ]
    args = [x, mod6, g1, w_in, qk_gain, _head_sum_matrix(),
            jnp.asarray(_channel_dft()).astype(BF16)]
    if rope:
        cos2, sin2 = _rope_tables(n)
        in_specs += [pl.BlockSpec((tm, V7X_LANES), lambda b, i: (i, 0))] * 2
        args += [cos2, sin2]
    widths = (2 * W_FOURIER, 2 * W_CONV, W_POOL)
    out_specs = [
        pl.BlockSpec((1, N_Q_HEADS, HEAD_DIM, tm), lambda b, i: (b, 0, 0, i)),
        pl.BlockSpec((1, N_KV_HEADS, tm, K_AUG), lambda b, i: (b, 0, i, 0)),
        pl.BlockSpec((1, N_KV_HEADS, V_AUG, tm), lambda b, i: (b, 0, 0, i)),
    ] + [tok(w) for w in widths]
    out_shape = [
        jax.ShapeDtypeStruct((bsz, N_Q_HEADS, HEAD_DIM, n), BF16),
        jax.ShapeDtypeStruct((bsz, N_KV_HEADS, n, K_AUG), BF16),
        jax.ShapeDtypeStruct((bsz, N_KV_HEADS, V_AUG, n), BF16),
    ] + [jax.ShapeDtypeStruct((bsz, n, w), BF16) for w in widths]
    return pl.pallas_call(
        functools.partial(_inproj_kernel, rope),
        grid=(bsz, nt),
        in_specs=in_specs,
        out_specs=out_specs,
        out_shape=out_shape,
        compiler_params=_params(2, 40),
        name="inproj_rope" if rope else "inproj_ctx",
    )(*args)


K_AUG = V7X_LANES
V_AUG = HEAD_DIM + V7X_BF16_ROWS
SAFE_SHIFT_LOG2 = 50.0


def _attn_kernel(chunks, q_ref, *refs):
    n_src = (len(refs) - 3) // 2
    k_refs, v_refs = refs[:n_src], refs[n_src:2 * n_src]
    o_ref, kmax_ref, acc_ref = refs[2 * n_src:]
    n_chunks = len(chunks)
    tq = q_ref.shape[-1]
    i = pl.program_id(2)

    def keys(c):
        src, start, size = chunks[c]
        return k_refs[src][0, 0, start:start + size, :]

    def values(c):
        src, start, size = chunks[c]
        return v_refs[src][0, 0, :, start:start + size]

    @pl.when(i == 0)
    def _():
        ones = jnp.ones((K_AUG, K_AUG), BF16)
        mx = jnp.zeros((1, K_AUG), F32)
        for c in range(n_chunks):
            kf = keys(c).astype(F32)
            norms = _dot((kf * kf).astype(BF16), ones)
            mx = jnp.maximum(mx, jnp.max(norms, axis=0, keepdims=True))
        kmax_ref[...] = jnp.sqrt(mx)

    q2 = jnp.concatenate([q_ref[0, 0], q_ref[0, 1]], axis=1)
    qf = q2.astype(F32)
    kmax = jnp.tile(kmax_ref[...], (1, 2 * tq // K_AUG))
    bound = jnp.sqrt(jnp.sum(qf * qf, axis=0, keepdims=True)) * kmax
    safe = jnp.max(bound) <= SAFE_SHIFT_LOG2
    row = lax.broadcasted_iota(jnp.int32, (K_AUG - HEAD_DIM, 2 * tq), 0)
    shift_rows = jnp.where(row == 0, -bound, 0.0).astype(BF16)
    q_aug = jnp.concatenate([q2, shift_rows], axis=0)

    def logits(c):
        return _dot(keys(c), q_aug)

    @pl.when(safe)
    def _():
        acc = None
        for c in range(n_chunks):
            p = jnp.exp2(logits(c)).astype(BF16)
            part = _dot(values(c), p)
            acc = part if acc is None else acc + part
        acc_ref[...] = acc

    @pl.when(jnp.logical_not(safe))
    def _():
        m = jnp.full((1, 2 * tq), NEG_BIG, F32)
        acc = jnp.zeros((V_AUG, 2 * tq), F32)
        for c in range(n_chunks):
            s = logits(c)
            m_new = jnp.maximum(m, jnp.max(s, axis=0, keepdims=True))
            p = jnp.exp2(s - m_new).astype(BF16)
            acc = jnp.exp2(m - m_new) * acc + _dot(values(c), p)
            m = m_new
        acc_ref[...] = acc

    o = acc_ref[0:HEAD_DIM, :] * (1.0 / acc_ref[HEAD_DIM:HEAD_DIM + 1, :])
    o2 = jnp.concatenate([o[:, :tq], o[:, tq:]], axis=0)
    o_ref[0] = o2.T.astype(BF16)


def _attention(q_t, kv_sources, tq, tk):
    bsz, _, _, nq = q_t.shape
    q_per_kv = N_Q_HEADS // N_KV_HEADS
    chunks = []
    for src, (k_aug, _) in enumerate(kv_sources):
        nk = k_aug.shape[2]
        step = min(tk, nk)
        chunks += [(src, start, step) for start in range(0, nk, step)]
    k_specs = [pl.BlockSpec((1, 1, k.shape[2], K_AUG), lambda b, g, i: (b, g, 0, 0))
               for k, _ in kv_sources]
    v_specs = [pl.BlockSpec((1, 1, V_AUG, v.shape[3]), lambda b, g, i: (b, g, 0, 0))
               for _, v in kv_sources]
    return pl.pallas_call(
        functools.partial(_attn_kernel, tuple(chunks)),
        grid=(bsz, N_KV_HEADS, nq // tq),
        in_specs=[pl.BlockSpec((1, q_per_kv, HEAD_DIM, tq), lambda b, g, i: (b, g, 0, i))]
        + k_specs + v_specs,
        out_specs=pl.BlockSpec((1, tq, q_per_kv * HEAD_DIM), lambda b, g, i: (b, i, g)),
        out_shape=jax.ShapeDtypeStruct((bsz, nq, W_ATTN), BF16),
        scratch_shapes=[pltpu.VMEM((1, K_AUG), F32),
                        pltpu.VMEM((V_AUG, q_per_kv * tq), F32)],
        compiler_params=pltpu.CompilerParams(
            dimension_semantics=("parallel", "parallel", "arbitrary"),
            vmem_limit_bytes=40 * 1024 * 1024),
        name="attention_%dsrc" % len(kv_sources),
    )(q_t, *[k for k, _ in kv_sources], *[v for _, v in kv_sources])


DFT_A_GROUP = 8


def _dft_a_kernel(bsz, g_ref, tr_ref, ti_ref, o_ref):
    n2 = g_ref.shape[2]
    for j in range(DFT_A_GROUP):
        tab_r = tr_ref[j].astype(BF16)
        tab_i = ti_ref[j].astype(BF16)
        for b in range(bsz):
            res = (_dot(tab_r, g_ref[j, b, :, :W_FOURIER])
                   + _dot(tab_i, g_ref[j, b, :, W_FOURIER:]))
            o_ref[0, j, :, b * W_FOURIER:(b + 1) * W_FOURIER] = res[:n2].astype(BF16)
            o_ref[1, j, :, b * W_FOURIER:(b + 1) * W_FOURIER] = res[n2:].astype(BF16)


def _dft_c_kernel(a_ref, tc_ref, o_ref):
    o_ref[...] = _dot(tc_ref[...], a_ref[...]).astype(BF16)


def _fourier_latent(gf):
    bsz, n, _ = gf.shape
    n1, n2 = DFT_N1, n // DFT_N1
    tab_r, tab_i, tab_c = _seq_dft_tables(n)
    g_t = jnp.transpose(gf.reshape(bsz, n2, n1, 2 * W_FOURIER), (2, 0, 1, 3))
    lanes = bsz * W_FOURIER
    a = pl.pallas_call(
        functools.partial(_dft_a_kernel, bsz),
        grid=(n1 // DFT_A_GROUP,),
        in_specs=[
            pl.BlockSpec((DFT_A_GROUP, bsz, n2, 2 * W_FOURIER), lambda t: (t, 0, 0, 0)),
            pl.BlockSpec((DFT_A_GROUP, 2 * n2, n2), lambda t: (t, 0, 0)),
            pl.BlockSpec((DFT_A_GROUP, 2 * n2, n2), lambda t: (t, 0, 0)),
        ],
        out_specs=pl.BlockSpec((2, DFT_A_GROUP, n2, lanes), lambda t: (0, t, 0, 0)),
        out_shape=jax.ShapeDtypeStruct((2, n1, n2, lanes), BF16),
        compiler_params=_params(1, 40),
        name="dft_stage_a",
    )(g_t, tab_r, tab_i)
    a2 = a.reshape(2 * n1, n2 * lanes)
    tl = 4096
    y = pl.pallas_call(
        _dft_c_kernel,
        grid=(n2 * lanes // tl,),
        in_specs=[pl.BlockSpec((2 * n1, tl), lambda j: (0, j)), _resident((n1, 2 * n1))],
        out_specs=pl.BlockSpec((n1, tl), lambda j: (0, j)),
        out_shape=jax.ShapeDtypeStruct((n1, n2 * lanes), BF16),
        compiler_params=_params(1, 40),
        name="dft_stage_c",
    )(a2, jnp.asarray(tab_c).astype(BF16))
    return y.reshape(n, lanes)


def _dft_ctx_kernel(g_ref, c_ref, s_ref, o_ref):
    o_ref[...] = (_dot(c_ref[...], g_ref[0, :, :W_FOURIER])
                  + _dot(s_ref[...], g_ref[0, :, W_FOURIER:])).astype(BF16)


def _fourier_ctx(gf):
    bsz, n, _ = gf.shape
    tab_cos, tab_sin = _ctx_dft_tables(n)
    return pl.pallas_call(
        _dft_ctx_kernel,
        grid=(bsz,),
        in_specs=[pl.BlockSpec((1, n, 2 * W_FOURIER), lambda b: (b, 0, 0)),
                  _resident((n, n)), _resident((n, n))],
        out_specs=pl.BlockSpec((n, W_FOURIER), lambda b: (0, b)),
        out_shape=jax.ShapeDtypeStruct((n, bsz * W_FOURIER), BF16),
        compiler_params=_params(1, 40),
        name="dft_ctx",
    )(gf, jnp.asarray(tab_cos).astype(BF16), jnp.asarray(tab_sin).astype(BF16))


CONV_ROWS = 64
CONV_TAP_ROWS = 32
FFN_CHUNKS = ((0, 768), (768, 768), (1536, 768), (2304, 512))


def _exact_zero(v):
    return jnp.minimum(jnp.abs(v), 0.0)


def _conv_pool_units(tile, n_tiles, n_seq, tm, am_ref, ap_ref, an_ref, um_ref, up_ref, un_ref,
                     dww_ref, cvec_ref, g_ref, gsh_ref, pu_ref, psh_ref, cv_ref, pl_ref):
    first = tile == 0
    last = tile == n_tiles - 1
    rows_sh = gsh_ref.shape[1]
    corner = lambda v: v[0:V7X_SUBLANES, 0:V7X_LANES]

    def gated(v, gate):
        if gate is None:
            return v
        rows, cols = v.shape
        return v + jnp.tile(gate, (rows // V7X_SUBLANES, cols // V7X_LANES))

    def glu(a_blk):
        a = a_blk.astype(F32)
        return a[:, :W_CONV] * _sigmoid(a[:, W_CONV:])

    def stage(gate):
        main = gated(glu(am_ref[0]), gate)
        g_ref[0:HALO] = jnp.where(first, 0.0, glu(ap_ref[0]))
        g_ref[HALO:HALO + tm] = main
        g_ref[HALO + tm:] = jnp.where(last, 0.0, glu(an_ref[0]))
        pu_ref[0:HALO] = jnp.where(first, 0.0, up_ref[0].astype(F32))
        pu_ref[HALO:HALO + tm] = um_ref[0].astype(F32)
        pu_ref[HALO + tm:] = jnp.where(last, 0.0, un_ref[0].astype(F32))
        return corner(main)

    def copy(b):
        def run(gate):
            gv = gated(g_ref[pl.ds(b, rows_sh), :], gate)
            gsh_ref[b - 1] = gv
            psh_ref[b - 1] = gated(pu_ref[pl.ds(b, rows_sh), :], gate)
            return corner(gv)
        return run

    lane = lax.broadcasted_iota(jnp.int32, (CONV_ROWS, W_POOL), 1)
    grp = lane // POOL_GROUP
    half = jnp.left_shift(1, grp)
    row = lax.broadcasted_iota(jnp.int32, (CONV_ROWS, W_POOL), 0)

    def shifted(base_ref, sh_ref, c0, j):
        src = base_ref if j % V7X_SUBLANES == 0 else sh_ref.at[j % V7X_SUBLANES - 1]
        return src[pl.ds(c0 + (j // V7X_SUBLANES) * V7X_SUBLANES, CONV_ROWS), :]

    def conv(c0):
        def run(gate):
            wts = gated(dww_ref[...], gate)
            acc = jnp.broadcast_to(cvec_ref[0:1, :], (CONV_ROWS, W_CONV))
            for k in range(CONV_WIDTH):
                acc = acc + wts[k:k + 1, :] * shifted(g_ref, gsh_ref, c0, k + 1)
            mu = jnp.mean(acc, axis=-1, keepdims=True)
            yc = acc - mu
            var = jnp.mean(yc * yc, axis=-1, keepdims=True)
            yn = yc * lax.rsqrt(var + EPS) * cvec_ref[1:2, :] + cvec_ref[2:3, :]
            act = yn * _sigmoid(yn)
            cv_ref[c0:c0 + CONV_ROWS, :] = act.astype(BF16)
            return corner(act)
        return run

    def pool(c0):
        def run(gate):
            ld = lambda d: shifted(pu_ref, psh_ref, c0, HALO + d)
            u0 = ld(0)
            s2 = gated(ld(-1) + u0, gate)
            s4 = s2 + ld(-2) + ld(1)
            s8 = s4 + ld(-4) + ld(-3) + ld(2) + ld(3)
            s16 = s8 + ld(-8) + ld(-7) + ld(-6) + ld(-5) + ld(4) + ld(5) + ld(6) + ld(7)
            win = jnp.where(grp == 0, s2, jnp.where(grp == 1, s4, jnp.where(grp == 2, s8, s16)))
            t = tile * tm + c0 + row
            cnt = jnp.minimum(t + half, n_seq) - jnp.maximum(t - half, 0)
            centred = win / cnt.astype(F32) - u0
            pl_ref[c0:c0 + CONV_ROWS, :] = centred.astype(BF16)
            return corner(centred)
        return run

    units = [stage] + [copy(b) for b in range(1, V7X_SUBLANES)]
    for c in range(tm // CONV_ROWS):
        units += [conv(c * CONV_ROWS), pool(c * CONV_ROWS)]
    return units


def _mixffn_kernel(n_tiles, n_seq, x_ref, attn_ref, yf_ref,
                   amc_ref, apc_ref, anc_ref, umc_ref, upc_ref, unc_ref,
                   amn_ref, apn_ref, ann_ref, umn_ref, upn_ref, unn_ref,
                   mod_ref, g2_ref, wf_ref, dww_ref, cvec_ref, wpw_ref, wpl_ref, wo_ref,
                   wfi_ref, wfo_ref, o_ref, g_ref, gsh_ref, pu_ref, psh_ref, cv_ref, pl_ref):
    tm = x_ref.shape[1]
    s = pl.program_id(0)
    n_steps = pl.num_programs(0)
    scratch = (dww_ref, cvec_ref, g_ref, gsh_ref, pu_ref, psh_ref, cv_ref, pl_ref)

    @pl.when(s == 0)
    def _():
        for unit in _conv_pool_units(0, n_tiles, n_seq, tm, amc_ref, apc_ref, anc_ref, umc_ref,
                                     upc_ref, unc_ref, *scratch):
            unit(None)

    cv = cv_ref[...]
    pooled = pl_ref[...]
    nxt_tile = jnp.minimum(s + 1, n_steps - 1) % n_tiles
    units = _conv_pool_units(nxt_tile, n_tiles, n_seq, tm, amn_ref, apn_ref, ann_ref, umn_ref,
                             upn_ref, unn_ref, *scratch)
    n_pre = V7X_SUBLANES
    slots = [units[:n_pre // 2], units[n_pre // 2:n_pre]]
    slots += [units[i:i + 2] for i in range(n_pre, len(units), 2)]

    def run_slot(gate_src):
        if not slots:
            return []
        gate = None if gate_src is None else _exact_zero(
            gate_src[-V7X_SUBLANES:, -V7X_LANES:])
        return [unit(gate) for unit in slots.pop(0)]

    def tied(lhs_f32, tokens):
        lhs = lhs_f32.astype(BF16)
        if not tokens:
            return lhs
        zero = sum(_exact_zero(t) for t in tokens)
        head = lhs_f32[0:V7X_BF16_ROWS] + jnp.tile(
            zero, (V7X_BF16_ROWS // V7X_SUBLANES, lhs_f32.shape[1] // V7X_LANES))
        return jnp.concatenate([head.astype(BF16), lhs[V7X_BF16_ROWS:]], axis=0)

    tokens = run_slot(None)
    four = _dot(yf_ref[...], wf_ref[...]).astype(BF16)
    conv = _dot(cv, wpw_ref[...]).astype(BF16)
    pool = (_dot(pooled, wpl_ref[...]) * cvec_ref[3:4, :]).astype(BF16)
    mix = (_dot(attn_ref[0], wo_ref[0:W_ATTN, :])
           + _dot(four, wo_ref[W_ATTN:2 * W_ATTN, :])
           + _dot(conv, wo_ref[2 * W_ATTN:3 * W_ATTN, :])
           + _dot(pool, wo_ref[3 * W_ATTN:, :]))
    x1 = x_ref[0] + mod_ref[MOD_GATE1:MOD_GATE1 + 1, :] * mix

    ms = jnp.mean(x1 * x1, axis=-1, keepdims=True)
    gain = g2_ref[...] * (1.0 + mod_ref[MOD_SCALE2:MOD_SCALE2 + 1, :])
    xn_f32 = (x1 * lax.rsqrt(ms + EPS)) * gain + mod_ref[MOD_SHIFT2:MOD_SHIFT2 + 1, :]
    prev = mix
    acc = None
    for lo, size in FFN_CHUNKS:
        lhs = tied(xn_f32, tokens)
        tokens = run_slot(prev)
        a = _dot(lhs, wfi_ref[:, lo:lo + size])
        lhs = tied(xn_f32, tokens)
        tokens = run_slot(a)
        g = _dot(lhs, wfi_ref[:, D_FF + lo:D_FF + lo + size])
        lhs = tied((a * _sigmoid(a)) * g, tokens)
        tokens = run_slot(g)
        part = _dot(lhs, wfo_ref[lo:lo + size, :])
        prev = part
        acc = part if acc is None else acc + part
    assert not slots
    out = x1 + mod_ref[MOD_GATE2:MOD_GATE2 + 1, :] * acc
    if tokens:
        zero = sum(_exact_zero(t) for t in tokens)
        out = out + jnp.tile(zero, (tm // V7X_SUBLANES, out.shape[1] // V7X_LANES))
    o_ref[0] = out


def _mix_ffn(layer, x, attn, yf, a_conv, u_pool, mod6, mod_row, g2, w_f, dw_w, cvec, w_pw,
             w_pool_bd, w_o, w_fi, w_fo, tm):
    bsz, n, d = x.shape
    nt = n // tm
    n_steps = bsz * nt
    hb = tm // HALO
    last_halo = n // HALO - 1

    def nxt(s):
        s1 = jnp.minimum(s + 1, n_steps - 1)
        return s1 // nt, s1 % nt

    tok = lambda width: pl.BlockSpec((1, tm, width), lambda s: (s // nt, s % nt, 0))
    cur_main = lambda w: pl.BlockSpec((1, tm, w), lambda s: (0, 0, 0))
    cur_prev = lambda w: pl.BlockSpec((1, HALO, w), lambda s: (0, 0, 0))
    cur_next = lambda w: pl.BlockSpec((1, HALO, w), lambda s: (0, min(hb, last_halo), 0))
    nxt_main = lambda w: pl.BlockSpec((1, tm, w), lambda s: (nxt(s)[0], nxt(s)[1], 0))
    nxt_prev = lambda w: pl.BlockSpec(
        (1, HALO, w), lambda s: (nxt(s)[0], jnp.maximum(nxt(s)[1] * hb - 1, 0), 0))
    nxt_next = lambda w: pl.BlockSpec(
        (1, HALO, w), lambda s: (nxt(s)[0], jnp.minimum((nxt(s)[1] + 1) * hb, last_halo), 0))
    rows_sh = tm + 2 * HALO - V7X_SUBLANES
    return pl.pallas_call(
        functools.partial(_mixffn_kernel, nt, n),
        grid=(n_steps,),
        in_specs=[
            tok(d), tok(W_ATTN),
            pl.BlockSpec((tm, W_FOURIER), lambda s: (s % nt, s // nt)),
            cur_main(2 * W_CONV), cur_prev(2 * W_CONV), cur_next(2 * W_CONV),
            cur_main(W_POOL), cur_prev(W_POOL), cur_next(W_POOL),
            nxt_main(2 * W_CONV), nxt_prev(2 * W_CONV), nxt_next(2 * W_CONV),
            nxt_main(W_POOL), nxt_prev(W_POOL), nxt_next(W_POOL),
            pl.BlockSpec((None, None, N_MOD, d), lambda s: (layer, mod_row(s // nt), 0, 0)),
            _layer_resident(layer, (1, d)),
            _layer_resident(layer, (W_FOURIER, W_FOURIER)),
            _layer_resident(layer, (CONV_TAP_ROWS, W_CONV)),
            _layer_resident(layer, (V7X_SUBLANES, W_CONV)),
            _layer_resident(layer, (W_CONV, W_CONV)),
            _layer_resident(layer, (W_POOL, W_POOL)),
            _layer_resident(layer, (4 * W_ATTN, d)),
            _layer_resident(layer, (d, 2 * D_FF)),
            _layer_resident(layer, (D_FF, d)),
        ],
        out_specs=tok(d),
        out_shape=jax.ShapeDtypeStruct((bsz, n, d), F32),
        scratch_shapes=[
            pltpu.VMEM((tm + 2 * HALO, W_CONV), F32),
            pltpu.VMEM((V7X_SUBLANES - 1, rows_sh, W_CONV), F32),
            pltpu.VMEM((tm + 2 * HALO, W_POOL), F32),
            pltpu.VMEM((V7X_SUBLANES - 1, rows_sh, W_POOL), F32),
            pltpu.VMEM((tm, W_CONV), BF16),
            pltpu.VMEM((tm, W_POOL), BF16),
        ],
        compiler_params=pltpu.CompilerParams(
            dimension_semantics=("arbitrary",), vmem_limit_bytes=56 * 1024 * 1024),
        name="mix_ffn_n%d" % n,
    )(x, attn, yf, a_conv, a_conv, a_conv, u_pool, u_pool, u_pool,
      a_conv, a_conv, a_conv, u_pool, u_pool, u_pool, mod6, g2,
      w_f, dw_w, cvec, w_pw, w_pool_bd, w_o, w_fi, w_fo)


def _pool_block_diag(w_pool):
    depth, groups = w_pool.shape[:2]
    eye = jnp.eye(groups, dtype=w_pool.dtype)
    blocks = w_pool[:, :, :, None, :] * eye[None, :, None, :, None]
    return blocks.reshape(depth, W_POOL, W_POOL)


def kernel(x, c, ctx, c_ctx, w_mod, b_mod, g_norm1, g_norm2, w_in, q_norm_g, k_norm_g,
           w_fourier, conv_dw_w, conv_dw_b, conv_ln_g, conv_ln_b, w_conv_pw, w_pool,
           pool_scale, w_out, w_ffn_in, w_ffn_out):
    bsz, n, d = x.shape
    n_ctx = ctx.shape[1]
    depth = w_in.shape[0]
    tm_lat, tm_ctx = 512, n_ctx
    tq_lat, tk_lat = 1024, 1024

    cc = jnp.concatenate([c, c_ctx[None, :],
                          jnp.zeros((V7X_SUBLANES - bsz - 1, d), F32)], axis=0)
    mod6 = _modulation(cc, w_mod, b_mod).reshape(depth, V7X_SUBLANES, N_MOD, d)
    lat_row = lambda b: b
    ctx_row = lambda b: bsz

    w_in_b = w_in.astype(BF16)
    w_f = w_fourier.astype(BF16)
    w_pw = w_conv_pw.astype(BF16)
    w_pl = _pool_block_diag(w_pool).astype(BF16)
    w_o = w_out.astype(BF16)
    w_fi = w_ffn_in.astype(BF16)
    w_fo = w_ffn_out.astype(BF16)
    qk_gain = jnp.concatenate([jnp.tile(q_norm_g, (1, N_Q_HEADS)),
                               jnp.tile(k_norm_g, (1, N_KV_HEADS))], axis=1)[:, None, :]
    g1 = g_norm1[:, None, :]
    g2 = g_norm2[:, None, :]
    dw_taps = jnp.concatenate(
        [conv_dw_w, jnp.zeros((depth, CONV_TAP_ROWS - CONV_WIDTH, W_CONV), F32)], axis=1)
    cvec = jnp.concatenate(
        [conv_dw_b[:, None], conv_ln_g[:, None], conv_ln_b[:, None], pool_scale[:, None],
         jnp.zeros((depth, V7X_SUBLANES - 4, W_CONV), F32)], axis=1)

    h = ctx
    for l in range(DEPTH):
        last = l == DEPTH - 1
        qc, kc, vc, gfc, ac, uc = _inproj(l, h, mod6, ctx_row, g1, w_in_b, qk_gain, False, tm_ctx)
        q, k, v, gf, a, u = _inproj(l, x, mod6, lat_row, g1, w_in_b, qk_gain, True, 2 * tm_lat)

        attn = _attention(q, [(kc, vc), (k, v)], tq_lat, tk_lat)
        yf = _fourier_latent(gf)
        x_new = _mix_ffn(l, x, attn, yf, a, u, mod6, lat_row, g2, w_f, dw_taps, cvec, w_pw, w_pl,
                         w_o, w_fi, w_fo, tm_lat)

        if not last:
            attn_c = _attention(qc, [(kc, vc)], n_ctx, n_ctx)
            yfc = _fourier_ctx(gfc)
            h = _mix_ffn(l, h, attn_c, yfc, ac, uc, mod6, ctx_row, g2, w_f, dw_taps, cvec, w_pw,
                         w_pl, w_o, w_fi, w_fo, tm_ctx)
        x = x_new
    return x
```

```python
import functools

import numpy as np
import jax
import jax.numpy as jnp
from jax import lax
from jax.experimental import pallas as pl
from jax.experimental.pallas import tpu as pltpu

F32 = jnp.float32
BF16 = jnp.bfloat16

D_MODEL = 1024
DEPTH = 2
GRID_W = 64
HEAD_DIM = 64
N_Q_HEADS = 4
N_KV_HEADS = 2
W_ATTN = 256
W_FOURIER = 256
W_CONV = 256
W_POOL = 256
CONV_WIDTH = 31
POOL_GROUP = 64
D_FF = 2816
ROPE_THETA = 10000.0
EPS = 1e-6
ATTN_SCALE = HEAD_DIM ** -0.5
Q_SCALE_LOG2 = ATTN_SCALE * float(np.log2(np.e))
W_QK = W_ATTN + N_KV_HEADS * HEAD_DIM
OFF_V = W_QK
OFF_F = OFF_V + N_KV_HEADS * HEAD_DIM
OFF_C = OFF_F + W_FOURIER
OFF_P = OFF_C + 2 * W_CONV
D_IN = OFF_P + W_POOL

V7X_LANES = 128
V7X_SUBLANES = 8
V7X_BF16_ROWS = 16
V7X_VMEM_BYTES = 64 * 1024 * 1024

MOD_SHIFT1, MOD_SCALE1, MOD_GATE1, MOD_SHIFT2, MOD_SCALE2, MOD_GATE2 = range(6)
N_MOD = 6

HALO = V7X_BF16_ROWS
DFT_N1 = 64
NEG_BIG = -1e30


def _dot(a, b):
    return jnp.dot(a, b, preferred_element_type=F32)


def _sigmoid(x):
    return 1.0 / (1.0 + jnp.exp(-x))


def _split_bf16(x):
    hi = x.astype(BF16)
    lo = (x - hi.astype(F32)).astype(BF16)
    return hi, lo


def _resident(shape):
    nd = len(shape)
    return pl.BlockSpec(shape, lambda *_: (0,) * nd, pipeline_mode=pl.Buffered(1))


def _layer_resident(layer, shape):
    nd = len(shape)
    return pl.BlockSpec((None,) + tuple(shape), lambda *_: (layer,) + (0,) * nd,
                        pipeline_mode=pl.Buffered(1))


def _params(n_grid, vmem_mb):
    return pltpu.CompilerParams(
        dimension_semantics=("parallel",) * n_grid,
        vmem_limit_bytes=vmem_mb * 1024 * 1024)


@functools.lru_cache(maxsize=None)
def _rope_tables(n):
    rows = n // GRID_W
    row = np.repeat(np.arange(rows, dtype=np.float64), GRID_W)
    col = np.tile(np.arange(GRID_W, dtype=np.float64), rows)
    n_freq = HEAD_DIM // 4
    inv_freq = ROPE_THETA ** (-np.arange(n_freq, dtype=np.float64) / n_freq)
    ang = np.concatenate([row[:, None] * inv_freq, col[:, None] * inv_freq], axis=-1)
    cos = np.repeat(np.cos(ang), 2, axis=1)
    sin = np.repeat(np.sin(ang), 2, axis=1)
    sign = np.tile(np.array([-1.0, 1.0]), HEAD_DIM // 2)
    cos2 = np.tile(cos, (1, 2)).astype(np.float32)
    sin2 = np.tile(sin * sign, (1, 2)).astype(np.float32)
    return cos2, sin2


@functools.lru_cache(maxsize=None)
def _head_sum_matrix():
    idx = np.arange(W_QK) // HEAD_DIM
    return np.asarray((idx[:, None] == idx[None, :]).astype(np.float32), dtype=BF16)


@functools.lru_cache(maxsize=None)
def _channel_dft():
    c = np.arange(W_FOURIER)
    same = (c[:, None] // HEAD_DIM) == (c[None, :] // HEAD_DIM)
    ang = 2.0 * np.pi * ((c[:, None] % HEAD_DIM) * (c[None, :] % HEAD_DIM)) / HEAD_DIM
    cr = np.where(same, np.cos(ang), 0.0) / 8.0
    ci = np.where(same, -np.sin(ang), 0.0) / 8.0
    return np.concatenate([cr, ci], axis=1).astype(np.float32)


@functools.lru_cache(maxsize=None)
def _seq_dft_tables(n):
    n1, n2 = DFT_N1, n // DFT_N1
    t1 = np.arange(n1)[:, None, None]
    k2 = np.arange(n2)[None, :, None]
    t2 = np.arange(n2)[None, None, :]
    theta = 2.0 * np.pi * ((k2 * (t1 + n1 * t2)) % n) / n
    dr = np.cos(theta) / np.sqrt(n2)
    di = -np.sin(theta) / np.sqrt(n2)
    tab_r = np.concatenate([dr, di], axis=1)
    tab_i = np.concatenate([-di, dr], axis=1)
    k1 = np.arange(n1)[:, None]
    tt = np.arange(n1)[None, :]
    phi = 2.0 * np.pi * ((k1 * tt) % n1) / n1
    tab_c = np.concatenate([np.cos(phi), np.sin(phi)], axis=1) / np.sqrt(n1)
    return tab_r.astype(np.float32), tab_i.astype(np.float32), tab_c.astype(np.float32)


@functools.lru_cache(maxsize=None)
def _ctx_dft_tables(n):
    k = np.arange(n)[:, None]
    t = np.arange(n)[None, :]
    ang = 2.0 * np.pi * ((k * t) % n) / n
    s = 1.0 / np.sqrt(n)
    return (np.cos(ang) * s).astype(np.float32), (np.sin(ang) * s).astype(np.float32)


def _mod_kernel(c_ref, w_ref, b_ref, o_ref):
    c = c_ref[...]
    sc = c * _sigmoid(c)
    a_hi, a_lo = _split_bf16(sc)
    w_hi, w_lo = _split_bf16(w_ref[0])
    o_ref[0] = _dot(a_hi, w_hi) + _dot(a_lo, w_hi) + _dot(a_hi, w_lo) + b_ref[0]


def _modulation(cc, w_mod, b_mod):
    depth, d, d6 = w_mod.shape
    tn = 1536
    return pl.pallas_call(
        _mod_kernel,
        grid=(depth, d6 // tn),
        in_specs=[
            pl.BlockSpec((V7X_SUBLANES, d), lambda l, j: (0, 0)),
            pl.BlockSpec((1, d, tn), lambda l, j: (l, 0, j)),
            pl.BlockSpec((1, 1, tn), lambda l, j: (l, 0, j)),
        ],
        out_specs=pl.BlockSpec((1, V7X_SUBLANES, tn), lambda l, j: (l, 0, j)),
        out_shape=jax.ShapeDtypeStruct((depth, V7X_SUBLANES, d6), F32),
        compiler_params=_params(2, 40),
        name="modulation",
    )(cc, w_mod, b_mod.reshape(depth, 1, d6))


INPROJ_ROWS = 256


def _rope(x, cos, sin):
    lane = lax.broadcasted_iota(jnp.int32, x.shape, 1)
    swapped = jnp.where((lane & 1) == 0,
                        pltpu.roll(x, V7X_LANES - 1, axis=1),
                        pltpu.roll(x, 1, axis=1))
    return x * cos + swapped * sin


def _inproj_kernel(rope, x_ref, mod_ref, g1_ref, w_ref, qkg_ref, bd_ref, cd_ref, *rest):
    if rope:
        cos_ref, sin_ref = rest[:2]
        rest = rest[2:]
    q_ref, k_ref, v_ref, gf_ref, a_ref, u_ref = rest
    tm = x_ref.shape[1]
    rg = min(tm, INPROJ_ROWS)
    gain = g1_ref[...] * (1.0 + mod_ref[MOD_SCALE1:MOD_SCALE1 + 1, :])
    lane = lax.broadcasted_iota(jnp.int32, (rg, V7X_LANES), 1)
    one_col = jnp.where(lane == HEAD_DIM, 1.0, 0.0)
    row = lax.broadcasted_iota(jnp.int32, (V_AUG - HEAD_DIM, rg), 0)
    ones_rows = jnp.where(row == 0, 1.0, 0.0).astype(BF16)
    for r0 in range(0, tm, rg):
        rows = slice(r0, r0 + rg)
        x = x_ref[0, rows, :]
        ms = jnp.mean(x * x, axis=-1, keepdims=True)
        xn = (x * lax.rsqrt(ms + EPS)) * gain + mod_ref[MOD_SHIFT1:MOD_SHIFT1 + 1, :]
        p = _dot(xn.astype(BF16), w_ref[...])

        qk = p[:, :W_QK]
        ss = _dot((qk * qk).astype(BF16), bd_ref[...])
        qkn = qk * lax.rsqrt(ss * (1.0 / HEAD_DIM) + EPS) * qkg_ref[...]
        slabs = []
        for j in range(W_QK // V7X_LANES):
            slab = qkn[:, j * V7X_LANES:(j + 1) * V7X_LANES]
            if rope:
                slab = _rope(slab, cos_ref[rows, :], sin_ref[rows, :])
            slabs.append(slab)
        for j in range(W_ATTN // V7X_LANES):
            qt = (slabs[j] * Q_SCALE_LOG2).T.astype(BF16)
            q_ref[0, 2 * j, :, rows] = qt[:HEAD_DIM]
            q_ref[0, 2 * j + 1, :, rows] = qt[HEAD_DIM:]
        k_ref[0, 0, rows, :] = jnp.where(lane < HEAD_DIM, slabs[2], one_col).astype(BF16)
        k_ref[0, 1, rows, :] = jnp.where(lane < HEAD_DIM, pltpu.roll(slabs[2], HEAD_DIM, axis=1),
                                         one_col).astype(BF16)
        vt = p[:, OFF_V:OFF_F].T.astype(BF16)
        for g in range(N_KV_HEADS):
            v_ref[0, g, 0:HEAD_DIM, rows] = vt[g * HEAD_DIM:(g + 1) * HEAD_DIM]
            v_ref[0, g, HEAD_DIM:, rows] = ones_rows
        gf_ref[0, rows, :] = _dot(p[:, OFF_F:OFF_C].astype(BF16), cd_ref[...]).astype(BF16)
        a_ref[0, rows, :] = p[:, OFF_C:OFF_P].astype(BF16)
        u_ref[0, rows, :] = p[:, OFF_P:].astype(BF16)


def _inproj(layer, x, mod6, mod_row, g1, w_in, qk_gain, rope, tm):
    bsz, n, d = x.shape
    nt = n // tm
    tok = lambda width: pl.BlockSpec((1, tm, width), lambda b, i: (b, i, 0))
    in_specs = [
        tok(d),
        pl.BlockSpec((None, None, N_MOD, d), lambda b, i: (layer, mod_row(b), 0, 0)),
        _layer_resident(layer, (1, d)),
        _layer_resident(layer, (d, D_IN)),
        _layer_resident(layer, (1, W_QK)),
        _resident((W_QK, W_QK)),
        _resident((W_FOURIER, 2 * W_FOURIER)),
    ]
    args = [x, mod6, g1, w_in, qk_gain, _head_sum_matrix(),
            jnp.asarray(_channel_dft()).astype(BF16)]
    if rope:
        cos2, sin2 = _rope_tables(n)
        in_specs += [pl.BlockSpec((tm, V7X_LANES), lambda b, i: (i, 0))] * 2
        args += [cos2, sin2]
    widths = (2 * W_FOURIER, 2 * W_CONV, W_POOL)
    out_specs = [
        pl.BlockSpec((1, N_Q_HEADS, HEAD_DIM, tm), lambda b, i: (b, 0, 0, i)),
        pl.BlockSpec((1, N_KV_HEADS, tm, K_AUG), lambda b, i: (b, 0, i, 0)),
        pl.BlockSpec((1, N_KV_HEADS, V_AUG, tm), lambda b, i: (b, 0, 0, i)),
    ] + [tok(w) for w in widths]
    out_shape = [
        jax.ShapeDtypeStruct((bsz, N_Q_HEADS, HEAD_DIM, n), BF16),
        jax.ShapeDtypeStruct((bsz, N_KV_HEADS, n, K_AUG), BF16),
        jax.ShapeDtypeStruct((bsz, N_KV_HEADS, V_AUG, n), BF16),
    ] + [jax.ShapeDtypeStruct((bsz, n, w), BF16) for w in widths]
    return pl.pallas_call(
        functools.partial(_inproj_kernel, rope),
        grid=(bsz, nt),
        in_specs=in_specs,
        out_specs=out_specs,
        out_shape=out_shape,
        compiler_params=_params(2, 40),
        name="inproj_rope" if rope else "inproj_ctx",
    )(*args)


K_AUG = V7X_LANES
V_AUG = HEAD_DIM + V7X_BF16_ROWS
SAFE_SHIFT_LOG2 = 50.0


def _attn_kernel(chunks, q_ref, *refs):
    n_src = (len(refs) - 3) // 2
    k_refs, v_refs = refs[:n_src], refs[n_src:2 * n_src]
    o_ref, kmax_ref, acc_ref = refs[2 * n_src:]
    n_chunks = len(chunks)
    tq = q_ref.shape[-1]
    i = pl.program_id(2)

    def keys(c):
        src, start, size = chunks[c]
        return k_refs[src][0, 0, start:start + size, :]

    def values(c):
        src, start, size = chunks[c]
        return v_refs[src][0, 0, :, start:start + size]

    @pl.when(i == 0)
    def _():
        ones = jnp.ones((K_AUG, K_AUG), BF16)
        mx = jnp.zeros((1, K_AUG), F32)
        for c in range(n_chunks):
            kf = keys(c).astype(F32)
            norms = _dot((kf * kf).astype(BF16), ones)
            mx = jnp.maximum(mx, jnp.max(norms, axis=0, keepdims=True))
        kmax_ref[...] = jnp.sqrt(mx)

    q2 = jnp.concatenate([q_ref[0, 0], q_ref[0, 1]], axis=1)
    qf = q2.astype(F32)
    kmax = jnp.tile(kmax_ref[...], (1, 2 * tq // K_AUG))
    bound = jnp.sqrt(jnp.sum(qf * qf, axis=0, keepdims=True)) * kmax
    safe = jnp.max(bound) <= SAFE_SHIFT_LOG2
    row = lax.broadcasted_iota(jnp.int32, (K_AUG - HEAD_DIM, 2 * tq), 0)
    shift_rows = jnp.where(row == 0, -bound, 0.0).astype(BF16)
    q_aug = jnp.concatenate([q2, shift_rows], axis=0)

    def logits(c):
        return _dot(keys(c), q_aug)

    @pl.when(safe)
    def _():
        acc = None
        for c in range(n_chunks):
            p = jnp.exp2(logits(c)).astype(BF16)
            part = _dot(values(c), p)
            acc = part if acc is None else acc + part
        acc_ref[...] = acc

    @pl.when(jnp.logical_not(safe))
    def _():
        m = jnp.full((1, 2 * tq), NEG_BIG, F32)
        acc = jnp.zeros((V_AUG, 2 * tq), F32)
        for c in range(n_chunks):
            s = logits(c)
            m_new = jnp.maximum(m, jnp.max(s, axis=0, keepdims=True))
            p = jnp.exp2(s - m_new).astype(BF16)
            acc = jnp.exp2(m - m_new) * acc + _dot(values(c), p)
            m = m_new
        acc_ref[...] = acc

    o = acc_ref[0:HEAD_DIM, :] * (1.0 / acc_ref[HEAD_DIM:HEAD_DIM + 1, :])
    o2 = jnp.concatenate([o[:, :tq], o[:, tq:]], axis=0)
    o_ref[0] = o2.T.astype(BF16)


def _attention(q_t, kv_sources, tq, tk):
    bsz, _, _, nq = q_t.shape
    q_per_kv = N_Q_HEADS // N_KV_HEADS
    chunks = []
    for src, (k_aug, _) in enumerate(kv_sources):
        nk = k_aug.shape[2]
        step = min(tk, nk)
        chunks += [(src, start, step) for start in range(0, nk, step)]
    k_specs = [pl.BlockSpec((1, 1, k.shape[2], K_AUG), lambda b, g, i: (b, g, 0, 0))
               for k, _ in kv_sources]
    v_specs = [pl.BlockSpec((1, 1, V_AUG, v.shape[3]), lambda b, g, i: (b, g, 0, 0))
               for _, v in kv_sources]
    return pl.pallas_call(
        functools.partial(_attn_kernel, tuple(chunks)),
        grid=(bsz, N_KV_HEADS, nq // tq),
        in_specs=[pl.BlockSpec((1, q_per_kv, HEAD_DIM, tq), lambda b, g, i: (b, g, 0, i))]
        + k_specs + v_specs,
        out_specs=pl.BlockSpec((1, tq, q_per_kv * HEAD_DIM), lambda b, g, i: (b, i, g)),
        out_shape=jax.ShapeDtypeStruct((bsz, nq, W_ATTN), BF16),
        scratch_shapes=[pltpu.VMEM((1, K_AUG), F32),
                        pltpu.VMEM((V_AUG, q_per_kv * tq), F32)],
        compiler_params=pltpu.CompilerParams(
            dimension_semantics=("parallel", "parallel", "arbitrary"),
            vmem_limit_bytes=40 * 1024 * 1024),
        name="attention_%dsrc" % len(kv_sources),
    )(q_t, *[k for k, _ in kv_sources], *[v for _, v in kv_sources])


DFT_A_GROUP = 8


def _dft_a_kernel(bsz, g_ref, tr_ref, ti_ref, o_ref):
    n2 = g_ref.shape[2]
    for j in range(DFT_A_GROUP):
        tab_r = tr_ref[j].astype(BF16)
        tab_i = ti_ref[j].astype(BF16)
        for b in range(bsz):
            res = (_dot(tab_r, g_ref[j, b, :, :W_FOURIER])
                   + _dot(tab_i, g_ref[j, b, :, W_FOURIER:]))
            o_ref[0, j, :, b * W_FOURIER:(b + 1) * W_FOURIER] = res[:n2].astype(BF16)
            o_ref[1, j, :, b * W_FOURIER:(b + 1) * W_FOURIER] = res[n2:].astype(BF16)


def _dft_c_kernel(a_ref, tc_ref, o_ref):
    res = _dot(tc_ref[...], a_ref[...])
    k2s, lanes = o_ref.shape[1], o_ref.shape[2]
    for j in range(k2s):
        o_ref[:, j, :] = res[:, j * lanes:(j + 1) * lanes]


def _fourier_latent(gf):
    bsz, n, _ = gf.shape
    n1, n2 = DFT_N1, n // DFT_N1
    tab_r, tab_i, tab_c = _seq_dft_tables(n)
    g_t = jnp.transpose(gf.reshape(bsz, n2, n1, 2 * W_FOURIER), (2, 0, 1, 3))
    lanes = bsz * W_FOURIER
    a = pl.pallas_call(
        functools.partial(_dft_a_kernel, bsz),
        grid=(n1 // DFT_A_GROUP,),
        in_specs=[
            pl.BlockSpec((DFT_A_GROUP, bsz, n2, 2 * W_FOURIER), lambda t: (t, 0, 0, 0)),
            pl.BlockSpec((DFT_A_GROUP, 2 * n2, n2), lambda t: (t, 0, 0)),
            pl.BlockSpec((DFT_A_GROUP, 2 * n2, n2), lambda t: (t, 0, 0)),
        ],
        out_specs=pl.BlockSpec((2, DFT_A_GROUP, n2, lanes), lambda t: (0, t, 0, 0)),
        out_shape=jax.ShapeDtypeStruct((2, n1, n2, lanes), BF16),
        compiler_params=_params(1, 40),
        name="dft_stage_a",
    )(g_t, tab_r, tab_i)
    a2 = a.reshape(2 * n1, n2 * lanes)
    k2s = V7X_SUBLANES
    y = pl.pallas_call(
        _dft_c_kernel,
        grid=(n2 // k2s,),
        in_specs=[pl.BlockSpec((2 * n1, k2s * lanes), lambda j: (0, j)),
                  _resident((n1, 2 * n1))],
        out_specs=pl.BlockSpec((n1, k2s, lanes), lambda j: (0, j, 0)),
        out_shape=jax.ShapeDtypeStruct((n1, n2, lanes), F32),
        compiler_params=_params(1, 40),
        name="dft_stage_c",
    )(a2, jnp.asarray(tab_c).astype(BF16))
    return y.reshape(n, lanes)


def _dft_ctx_kernel(g_ref, c_ref, s_ref, o_ref):
    o_ref[...] = (_dot(c_ref[...], g_ref[0, :, :W_FOURIER])
                  + _dot(s_ref[...], g_ref[0, :, W_FOURIER:])).astype(BF16)


def _fourier_ctx(gf):
    bsz, n, _ = gf.shape
    tab_cos, tab_sin = _ctx_dft_tables(n)
    return pl.pallas_call(
        _dft_ctx_kernel,
        grid=(bsz,),
        in_specs=[pl.BlockSpec((1, n, 2 * W_FOURIER), lambda b: (b, 0, 0)),
                  _resident((n, n)), _resident((n, n))],
        out_specs=pl.BlockSpec((n, W_FOURIER), lambda b: (0, b)),
        out_shape=jax.ShapeDtypeStruct((n, bsz * W_FOURIER), BF16),
        compiler_params=_params(1, 40),
        name="dft_ctx",
    )(gf, jnp.asarray(tab_cos).astype(BF16), jnp.asarray(tab_sin).astype(BF16))


CONV_ROWS = 64
CONV_TAP_ROWS = 32
FFN_CHUNKS = ((0, 768), (768, 768), (1536, 768), (2304, 512))


def _exact_zero(v):
    return jnp.minimum(jnp.abs(v), 0.0)


def _conv_pool_units(tile, n_tiles, n_seq, tm, am_ref, ap_ref, an_ref, um_ref, up_ref, un_ref,
                     dww_ref, cvec_ref, g_ref, gsh_ref, pu_ref, psh_ref, cv_ref, pl_ref):
    first = tile == 0
    last = tile == n_tiles - 1
    rows_sh = gsh_ref.shape[1]
    corner = lambda v: v[0:V7X_SUBLANES, 0:V7X_LANES]

    def gated(v, gate):
        if gate is None:
            return v
        rows, cols = v.shape
        return v + jnp.tile(gate, (rows // V7X_SUBLANES, cols // V7X_LANES))

    def glu(a_blk):
        a = a_blk.astype(F32)
        return a[:, :W_CONV] * _sigmoid(a[:, W_CONV:])

    def stage(gate):
        main = gated(glu(am_ref[0]), gate)
        g_ref[0:HALO] = jnp.where(first, 0.0, glu(ap_ref[0]))
        g_ref[HALO:HALO + tm] = main
        g_ref[HALO + tm:] = jnp.where(last, 0.0, glu(an_ref[0]))
        pu_ref[0:HALO] = jnp.where(first, 0.0, up_ref[0].astype(F32))
        pu_ref[HALO:HALO + tm] = um_ref[0].astype(F32)
        pu_ref[HALO + tm:] = jnp.where(last, 0.0, un_ref[0].astype(F32))
        return corner(main)

    def copy(b):
        def run(gate):
            gv = gated(g_ref[pl.ds(b, rows_sh), :], gate)
            gsh_ref[b - 1] = gv
            psh_ref[b - 1] = gated(pu_ref[pl.ds(b, rows_sh), :], gate)
            return corner(gv)
        return run

    lane = lax.broadcasted_iota(jnp.int32, (CONV_ROWS, W_POOL), 1)
    grp = lane // POOL_GROUP
    half = jnp.left_shift(1, grp)
    row = lax.broadcasted_iota(jnp.int32, (CONV_ROWS, W_POOL), 0)

    def shifted(base_ref, sh_ref, c0, j):
        src = base_ref if j % V7X_SUBLANES == 0 else sh_ref.at[j % V7X_SUBLANES - 1]
        return src[pl.ds(c0 + (j // V7X_SUBLANES) * V7X_SUBLANES, CONV_ROWS), :]

    def conv(c0):
        def run(gate):
            wts = gated(dww_ref[...], gate)
            acc = jnp.broadcast_to(cvec_ref[0:1, :], (CONV_ROWS, W_CONV))
            for k in range(CONV_WIDTH):
                acc = acc + wts[k:k + 1, :] * shifted(g_ref, gsh_ref, c0, k + 1)
            mu = jnp.mean(acc, axis=-1, keepdims=True)
            yc = acc - mu
            var = jnp.mean(yc * yc, axis=-1, keepdims=True)
            yn = yc * lax.rsqrt(var + EPS) * cvec_ref[1:2, :] + cvec_ref[2:3, :]
            act = yn * _sigmoid(yn)
            cv_ref[c0:c0 + CONV_ROWS, :] = act.astype(BF16)
            return corner(act)
        return run

    def pool(c0):
        def run(gate):
            ld = lambda d: shifted(pu_ref, psh_ref, c0, HALO + d)
            u0 = ld(0)
            s2 = gated(ld(-1) + u0, gate)
            s4 = s2 + ld(-2) + ld(1)
            s8 = s4 + ld(-4) + ld(-3) + ld(2) + ld(3)
            s16 = s8 + ld(-8) + ld(-7) + ld(-6) + ld(-5) + ld(4) + ld(5) + ld(6) + ld(7)
            win = jnp.where(grp == 0, s2, jnp.where(grp == 1, s4, jnp.where(grp == 2, s8, s16)))
            t = tile * tm + c0 + row
            cnt = jnp.minimum(t + half, n_seq) - jnp.maximum(t - half, 0)
            centred = win / cnt.astype(F32) - u0
            pl_ref[c0:c0 + CONV_ROWS, :] = centred.astype(BF16)
            return corner(centred)
        return run

    units = [stage] + [copy(b) for b in range(1, V7X_SUBLANES)]
    for c in range(tm // CONV_ROWS):
        units += [conv(c * CONV_ROWS), pool(c * CONV_ROWS)]
    return units


def _mixffn_kernel(n_tiles, n_seq, x_ref, attn_ref, yf_ref,
                   amc_ref, apc_ref, anc_ref, umc_ref, upc_ref, unc_ref,
                   amn_ref, apn_ref, ann_ref, umn_ref, upn_ref, unn_ref,
                   mod_ref, g2_ref, wf_ref, dww_ref, cvec_ref, wpw_ref, wpl_ref, wo_ref,
                   wfi_ref, wfo_ref, o_ref, g_ref, gsh_ref, pu_ref, psh_ref, cv_ref, pl_ref):
    tm = x_ref.shape[1]
    s = pl.program_id(0)
    n_steps = pl.num_programs(0)
    scratch = (dww_ref, cvec_ref, g_ref, gsh_ref, pu_ref, psh_ref, cv_ref, pl_ref)

    @pl.when(s == 0)
    def _():
        for unit in _conv_pool_units(0, n_tiles, n_seq, tm, amc_ref, apc_ref, anc_ref, umc_ref,
                                     upc_ref, unc_ref, *scratch):
            unit(None)

    cv = cv_ref[...]
    pooled = pl_ref[...]
    nxt_tile = jnp.minimum(s + 1, n_steps - 1) % n_tiles
    units = _conv_pool_units(nxt_tile, n_tiles, n_seq, tm, amn_ref, apn_ref, ann_ref, umn_ref,
                             upn_ref, unn_ref, *scratch)
    n_pre = V7X_SUBLANES
    slots = [units[:n_pre // 2], units[n_pre // 2:n_pre]]
    slots += [units[i:i + 2] for i in range(n_pre, len(units), 2)]

    def run_slot(gate_src):
        if not slots:
            return []
        gate = None if gate_src is None else _exact_zero(
            gate_src[-V7X_SUBLANES:, -V7X_LANES:])
        return [unit(gate) for unit in slots.pop(0)]

    def tied(lhs_f32, tokens):
        lhs = lhs_f32.astype(BF16)
        if not tokens:
            return lhs
        zero = sum(_exact_zero(t) for t in tokens)
        head = lhs_f32[0:V7X_BF16_ROWS] + jnp.tile(
            zero, (V7X_BF16_ROWS // V7X_SUBLANES, lhs_f32.shape[1] // V7X_LANES))
        return jnp.concatenate([head.astype(BF16), lhs[V7X_BF16_ROWS:]], axis=0)

    tokens = run_slot(None)
    four = _dot(yf_ref[...].astype(BF16), wf_ref[...]).astype(BF16)
    conv = _dot(cv, wpw_ref[...]).astype(BF16)
    pool = (_dot(pooled, wpl_ref[...]) * cvec_ref[3:4, :]).astype(BF16)
    mix = (_dot(attn_ref[0], wo_ref[0:W_ATTN, :])
           + _dot(four, wo_ref[W_ATTN:2 * W_ATTN, :])
           + _dot(conv, wo_ref[2 * W_ATTN:3 * W_ATTN, :])
           + _dot(pool, wo_ref[3 * W_ATTN:, :]))
    x1 = x_ref[0] + mod_ref[MOD_GATE1:MOD_GATE1 + 1, :] * mix

    ms = jnp.mean(x1 * x1, axis=-1, keepdims=True)
    gain = g2_ref[...] * (1.0 + mod_ref[MOD_SCALE2:MOD_SCALE2 + 1, :])
    xn_f32 = (x1 * lax.rsqrt(ms + EPS)) * gain + mod_ref[MOD_SHIFT2:MOD_SHIFT2 + 1, :]
    prev = mix
    acc = None
    for lo, size in FFN_CHUNKS:
        lhs = tied(xn_f32, tokens)
        tokens = run_slot(prev)
        a = _dot(lhs, wfi_ref[:, lo:lo + size])
        lhs = tied(xn_f32, tokens)
        tokens = run_slot(a)
        g = _dot(lhs, wfi_ref[:, D_FF + lo:D_FF + lo + size])
        lhs = tied((a * _sigmoid(a)) * g, tokens)
        tokens = run_slot(g)
        part = _dot(lhs, wfo_ref[lo:lo + size, :])
        prev = part
        acc = part if acc is None else acc + part
    assert not slots
    out = x1 + mod_ref[MOD_GATE2:MOD_GATE2 + 1, :] * acc
    if tokens:
        zero = sum(_exact_zero(t) for t in tokens)
        out = out + jnp.tile(zero, (tm // V7X_SUBLANES, out.shape[1] // V7X_LANES))
    o_ref[0] = out


def _mix_ffn(layer, x, attn, yf, a_conv, u_pool, mod6, mod_row, g2, w_f, dw_w, cvec, w_pw,
             w_pool_bd, w_o, w_fi, w_fo, tm):
    bsz, n, d = x.shape
    nt = n // tm
    n_steps = bsz * nt
    hb = tm // HALO
    last_halo = n // HALO - 1

    def nxt(s):
        s1 = jnp.minimum(s + 1, n_steps - 1)
        return s1 // nt, s1 % nt

    tok = lambda width: pl.BlockSpec((1, tm, width), lambda s: (s // nt, s % nt, 0))
    cur_main = lambda w: pl.BlockSpec((1, tm, w), lambda s: (0, 0, 0))
    cur_prev = lambda w: pl.BlockSpec((1, HALO, w), lambda s: (0, 0, 0))
    cur_next = lambda w: pl.BlockSpec((1, HALO, w), lambda s: (0, min(hb, last_halo), 0))
    nxt_main = lambda w: pl.BlockSpec((1, tm, w), lambda s: (nxt(s)[0], nxt(s)[1], 0))
    nxt_prev = lambda w: pl.BlockSpec(
        (1, HALO, w), lambda s: (nxt(s)[0], jnp.maximum(nxt(s)[1] * hb - 1, 0), 0))
    nxt_next = lambda w: pl.BlockSpec(
        (1, HALO, w), lambda s: (nxt(s)[0], jnp.minimum((nxt(s)[1] + 1) * hb, last_halo), 0))
    rows_sh = tm + 2 * HALO - V7X_SUBLANES
    return pl.pallas_call(
        functools.partial(_mixffn_kernel, nt, n),
        grid=(n_steps,),
        in_specs=[
            tok(d), tok(W_ATTN),
            pl.BlockSpec((tm, W_FOURIER), lambda s: (s % nt, s // nt)),
            cur_main(2 * W_CONV), cur_prev(2 * W_CONV), cur_next(2 * W_CONV),
            cur_main(W_POOL), cur_prev(W_POOL), cur_next(W_POOL),
            nxt_main(2 * W_CONV), nxt_prev(2 * W_CONV), nxt_next(2 * W_CONV),
            nxt_main(W_POOL), nxt_prev(W_POOL), nxt_next(W_POOL),
            pl.BlockSpec((None, None, N_MOD, d), lambda s: (layer, mod_row(s // nt), 0, 0)),
            _layer_resident(layer, (1, d)),
            _layer_resident(layer, (W_FOURIER, W_FOURIER)),
            _layer_resident(layer, (CONV_TAP_ROWS, W_CONV)),
            _layer_resident(layer, (V7X_SUBLANES, W_CONV)),
            _layer_resident(layer, (W_CONV, W_CONV)),
            _layer_resident(layer, (W_POOL, W_POOL)),
            _layer_resident(layer, (4 * W_ATTN, d)),
            _layer_resident(layer, (d, 2 * D_FF)),
            _layer_resident(layer, (D_FF, d)),
        ],
        out_specs=tok(d),
        out_shape=jax.ShapeDtypeStruct((bsz, n, d), F32),
        scratch_shapes=[
            pltpu.VMEM((tm + 2 * HALO, W_CONV), F32),
            pltpu.VMEM((V7X_SUBLANES - 1, rows_sh, W_CONV), F32),
            pltpu.VMEM((tm + 2 * HALO, W_POOL), F32),
            pltpu.VMEM((V7X_SUBLANES - 1, rows_sh, W_POOL), F32),
            pltpu.VMEM((tm, W_CONV), BF16),
            pltpu.VMEM((tm, W_POOL), BF16),
        ],
        compiler_params=pltpu.CompilerParams(
            dimension_semantics=("arbitrary",), vmem_limit_bytes=56 * 1024 * 1024),
        name="mix_ffn_n%d" % n,
    )(x, attn, yf, a_conv, a_conv, a_conv, u_pool, u_pool, u_pool,
      a_conv, a_conv, a_conv, u_pool, u_pool, u_pool, mod6, g2,
      w_f, dw_w, cvec, w_pw, w_pool_bd, w_o, w_fi, w_fo)


def _pool_block_diag(w_pool):
    depth, groups = w_pool.shape[:2]
    eye = jnp.eye(groups, dtype=w_pool.dtype)
    blocks = w_pool[:, :, :, None, :] * eye[None, :, None, :, None]
    return blocks.reshape(depth, W_POOL, W_POOL)


def kernel(x, c, ctx, c_ctx, w_mod, b_mod, g_norm1, g_norm2, w_in, q_norm_g, k_norm_g,
           w_fourier, conv_dw_w, conv_dw_b, conv_ln_g, conv_ln_b, w_conv_pw, w_pool,
           pool_scale, w_out, w_ffn_in, w_ffn_out):
    bsz, n, d = x.shape
    n_ctx = ctx.shape[1]
    depth = w_in.shape[0]
    tm_lat, tm_ctx = 512, n_ctx
    tq_lat, tk_lat = 512, 1024

    cc = jnp.concatenate([c, c_ctx[None, :],
                          jnp.zeros((V7X_SUBLANES - bsz - 1, d), F32)], axis=0)
    mod6 = _modulation(cc, w_mod, b_mod).reshape(depth, V7X_SUBLANES, N_MOD, d)
    lat_row = lambda b: b
    ctx_row = lambda b: bsz

    w_in_b = w_in.astype(BF16)
    w_f = w_fourier.astype(BF16)
    w_pw = w_conv_pw.astype(BF16)
    w_pl = _pool_block_diag(w_pool).astype(BF16)
    w_o = w_out.astype(BF16)
    w_fi = w_ffn_in.astype(BF16)
    w_fo = w_ffn_out.astype(BF16)
    qk_gain = jnp.concatenate([jnp.tile(q_norm_g, (1, N_Q_HEADS)),
                               jnp.tile(k_norm_g, (1, N_KV_HEADS))], axis=1)[:, None, :]
    g1 = g_norm1[:, None, :]
    g2 = g_norm2[:, None, :]
    dw_taps = jnp.concatenate(
        [conv_dw_w, jnp.zeros((depth, CONV_TAP_ROWS - CONV_WIDTH, W_CONV), F32)], axis=1)
    cvec = jnp.concatenate(
        [conv_dw_b[:, None], conv_ln_g[:, None], conv_ln_b[:, None], pool_scale[:, None],
         jnp.zeros((depth, V7X_SUBLANES - 4, W_CONV), F32)], axis=1)

    h = ctx
    for l in range(DEPTH):
        last = l == DEPTH - 1
        qc, kc, vc, gfc, ac, uc = _inproj(l, h, mod6, ctx_row, g1, w_in_b, qk_gain, False, tm_ctx)
        q, k, v, gf, a, u = _inproj(l, x, mod6, lat_row, g1, w_in_b, qk_gain, True, 2 * tm_lat)

        attn = _attention(q, [(kc, vc), (k, v)], tq_lat, tk_lat)
        yf = _fourier_latent(gf)
        x_new = _mix_ffn(l, x, attn, yf, a, u, mod6, lat_row, g2, w_f, dw_taps, cvec, w_pw, w_pl,
                         w_o, w_fi, w_fo, tm_lat)

        if not last:
            attn_c = _attention(qc, [(kc, vc)], n_ctx, n_ctx)
            yfc = _fourier_ctx(gfc)
            h = _mix_ffn(l, h, attn_c, yfc, ac, uc, mod6, ctx_row, g2, w_f, dw_taps, cvec, w_pw,
                         w_pl, w_o, w_fi, w_fo, tm_ctx)
        x = x_new
    return x
```

```python
import functools

import numpy as np
import jax
import jax.numpy as jnp
from jax import lax
from jax.experimental import pallas as pl
from jax.experimental.pallas import tpu as pltpu

F32 = jnp.float32
BF16 = jnp.bfloat16

D_MODEL = 1024
DEPTH = 2
GRID_W = 64
HEAD_DIM = 64
N_Q_HEADS = 4
N_KV_HEADS = 2
W_ATTN = 256
W_FOURIER = 256
W_CONV = 256
W_POOL = 256
CONV_WIDTH = 31
POOL_GROUP = 64
D_FF = 2816
ROPE_THETA = 10000.0
EPS = 1e-6
ATTN_SCALE = HEAD_DIM ** -0.5
Q_SCALE_LOG2 = ATTN_SCALE * float(np.log2(np.e))
W_QK = W_ATTN + N_KV_HEADS * HEAD_DIM
OFF_V = W_QK
OFF_F = OFF_V + N_KV_HEADS * HEAD_DIM
OFF_C = OFF_F + W_FOURIER
OFF_P = OFF_C + 2 * W_CONV
D_IN = OFF_P + W_POOL

V7X_LANES = 128
V7X_SUBLANES = 8
V7X_BF16_ROWS = 16
V7X_VMEM_BYTES = 64 * 1024 * 1024

MOD_SHIFT1, MOD_SCALE1, MOD_GATE1, MOD_SHIFT2, MOD_SCALE2, MOD_GATE2 = range(6)
N_MOD = 6

HALO = V7X_BF16_ROWS
DFT_N1 = 64
NEG_BIG = -1e30


def _dot(a, b):
    return jnp.dot(a, b, preferred_element_type=F32)


def _sigmoid(x):
    return 1.0 / (1.0 + jnp.exp(-x))


def _split_bf16(x):
    hi = x.astype(BF16)
    lo = (x - hi.astype(F32)).astype(BF16)
    return hi, lo


def _resident(shape):
    nd = len(shape)
    return pl.BlockSpec(shape, lambda *_: (0,) * nd, pipeline_mode=pl.Buffered(1))


def _layer_resident(layer, shape):
    nd = len(shape)
    return pl.BlockSpec((None,) + tuple(shape), lambda *_: (layer,) + (0,) * nd,
                        pipeline_mode=pl.Buffered(1))


def _params(n_grid, vmem_mb):
    return pltpu.CompilerParams(
        dimension_semantics=("parallel",) * n_grid,
        vmem_limit_bytes=vmem_mb * 1024 * 1024)


@functools.lru_cache(maxsize=None)
def _rope_tables(n):
    rows = n // GRID_W
    row = np.repeat(np.arange(rows, dtype=np.float64), GRID_W)
    col = np.tile(np.arange(GRID_W, dtype=np.float64), rows)
    n_freq = HEAD_DIM // 4
    inv_freq = ROPE_THETA ** (-np.arange(n_freq, dtype=np.float64) / n_freq)
    ang = np.concatenate([row[:, None] * inv_freq, col[:, None] * inv_freq], axis=-1)
    cos = np.repeat(np.cos(ang), 2, axis=1)
    sin = np.repeat(np.sin(ang), 2, axis=1)
    sign = np.tile(np.array([-1.0, 1.0]), HEAD_DIM // 2)
    cos2 = np.tile(cos, (1, 2)).astype(np.float32)
    sin2 = np.tile(sin * sign, (1, 2)).astype(np.float32)
    return cos2, sin2


@functools.lru_cache(maxsize=None)
def _head_sum_matrix():
    idx = np.arange(W_QK) // HEAD_DIM
    return np.asarray((idx[:, None] == idx[None, :]).astype(np.float32), dtype=BF16)


@functools.lru_cache(maxsize=None)
def _channel_dft():
    c = np.arange(W_FOURIER)
    same = (c[:, None] // HEAD_DIM) == (c[None, :] // HEAD_DIM)
    ang = 2.0 * np.pi * ((c[:, None] % HEAD_DIM) * (c[None, :] % HEAD_DIM)) / HEAD_DIM
    cr = np.where(same, np.cos(ang), 0.0) / 8.0
    ci = np.where(same, -np.sin(ang), 0.0) / 8.0
    return np.concatenate([cr, ci], axis=1).astype(np.float32)


@functools.lru_cache(maxsize=None)
def _seq_dft_tables(n):
    n1, n2 = DFT_N1, n // DFT_N1
    t1 = np.arange(n1)[:, None, None]
    k2 = np.arange(n2)[None, :, None]
    t2 = np.arange(n2)[None, None, :]
    theta = 2.0 * np.pi * ((k2 * (t1 + n1 * t2)) % n) / n
    dr = np.cos(theta) / np.sqrt(n2)
    di = -np.sin(theta) / np.sqrt(n2)
    tab_r = np.concatenate([dr, di], axis=1)
    tab_i = np.concatenate([-di, dr], axis=1)
    k1 = np.arange(n1)[:, None]
    tt = np.arange(n1)[None, :]
    phi = 2.0 * np.pi * ((k1 * tt) % n1) / n1
    tab_c = np.concatenate([np.cos(phi), np.sin(phi)], axis=1) / np.sqrt(n1)
    return tab_r.astype(np.float32), tab_i.astype(np.float32), tab_c.astype(np.float32)


@functools.lru_cache(maxsize=None)
def _ctx_dft_tables(n):
    k = np.arange(n)[:, None]
    t = np.arange(n)[None, :]
    ang = 2.0 * np.pi * ((k * t) % n) / n
    s = 1.0 / np.sqrt(n)
    return (np.cos(ang) * s).astype(np.float32), (np.sin(ang) * s).astype(np.float32)


def _mod_kernel(c_ref, w_ref, b_ref, o_ref):
    c = c_ref[...]
    sc = c * _sigmoid(c)
    a_hi, a_lo = _split_bf16(sc)
    w_hi, w_lo = _split_bf16(w_ref[0])
    o_ref[0] = _dot(a_hi, w_hi) + _dot(a_lo, w_hi) + _dot(a_hi, w_lo) + b_ref[0]


def _modulation(cc, w_mod, b_mod):
    depth, d, d6 = w_mod.shape
    tn = 1536
    return pl.pallas_call(
        _mod_kernel,
        grid=(depth, d6 // tn),
        in_specs=[
            pl.BlockSpec((V7X_SUBLANES, d), lambda l, j: (0, 0)),
            pl.BlockSpec((1, d, tn), lambda l, j: (l, 0, j)),
            pl.BlockSpec((1, 1, tn), lambda l, j: (l, 0, j)),
        ],
        out_specs=pl.BlockSpec((1, V7X_SUBLANES, tn), lambda l, j: (l, 0, j)),
        out_shape=jax.ShapeDtypeStruct((depth, V7X_SUBLANES, d6), F32),
        compiler_params=_params(2, 40),
        name="modulation",
    )(cc, w_mod, b_mod.reshape(depth, 1, d6))


INPROJ_ROWS = 256


def _rope(x, cos, sin):
    lane = lax.broadcasted_iota(jnp.int32, x.shape, 1)
    swapped = jnp.where((lane & 1) == 0,
                        pltpu.roll(x, V7X_LANES - 1, axis=1),
                        pltpu.roll(x, 1, axis=1))
    return x * cos + swapped * sin


def _inproj_kernel(rope, x_ref, mod_ref, g1_ref, w_ref, qkg_ref, bd_ref, cd_ref, *rest):
    if rope:
        cos_ref, sin_ref = rest[:2]
        rest = rest[2:]
    q_ref, k_ref, v_ref, gf_ref, a_ref, u_ref = rest
    tm = x_ref.shape[1]
    rg = min(tm, INPROJ_ROWS)
    gain = g1_ref[...] * (1.0 + mod_ref[MOD_SCALE1:MOD_SCALE1 + 1, :])
    lane = lax.broadcasted_iota(jnp.int32, (rg, V7X_LANES), 1)
    one_col = jnp.where(lane == HEAD_DIM, 1.0, 0.0)
    row = lax.broadcasted_iota(jnp.int32, (V_AUG - HEAD_DIM, rg), 0)
    ones_rows = jnp.where(row == 0, 1.0, 0.0).astype(BF16)
    for r0 in range(0, tm, rg):
        rows = slice(r0, r0 + rg)
        x = x_ref[0, rows, :]
        ms = jnp.mean(x * x, axis=-1, keepdims=True)
        xn = (x * lax.rsqrt(ms + EPS)) * gain + mod_ref[MOD_SHIFT1:MOD_SHIFT1 + 1, :]
        p = _dot(xn.astype(BF16), w_ref[...])

        qk = p[:, :W_QK]
        ss = _dot((qk * qk).astype(BF16), bd_ref[...])
        qkn = qk * lax.rsqrt(ss * (1.0 / HEAD_DIM) + EPS) * qkg_ref[...]
        slabs = []
        for j in range(W_QK // V7X_LANES):
            slab = qkn[:, j * V7X_LANES:(j + 1) * V7X_LANES]
            if rope:
                slab = _rope(slab, cos_ref[rows, :], sin_ref[rows, :])
            slabs.append(slab)
        for j in range(W_ATTN // V7X_LANES):
            qt = (slabs[j] * Q_SCALE_LOG2).T.astype(BF16)
            q_ref[0, 2 * j, :, rows] = qt[:HEAD_DIM]
            q_ref[0, 2 * j + 1, :, rows] = qt[HEAD_DIM:]
        k_ref[0, 0, rows, :] = jnp.where(lane < HEAD_DIM, slabs[2], one_col).astype(BF16)
        k_ref[0, 1, rows, :] = jnp.where(lane < HEAD_DIM, pltpu.roll(slabs[2], HEAD_DIM, axis=1),
                                         one_col).astype(BF16)
        vt = p[:, OFF_V:OFF_F].T.astype(BF16)
        for g in range(N_KV_HEADS):
            v_ref[0, g, 0:HEAD_DIM, rows] = vt[g * HEAD_DIM:(g + 1) * HEAD_DIM]
            v_ref[0, g, HEAD_DIM:, rows] = ones_rows
        gf = _dot(p[:, OFF_F:OFF_C].astype(BF16), cd_ref[...])
        if rope:
            for t in range(rg // DFT_N1):
                gf_ref[:, 0, r0 // DFT_N1 + t, :] = gf[t * DFT_N1:(t + 1) * DFT_N1]
        else:
            gf_ref[0, rows, :] = gf.astype(BF16)
        a_ref[0, rows, :] = p[:, OFF_C:OFF_P].astype(BF16)
        u_ref[0, rows, :] = p[:, OFF_P:].astype(BF16)


def _inproj(layer, x, mod6, mod_row, g1, w_in, qk_gain, rope, tm):
    bsz, n, d = x.shape
    nt = n // tm
    tok = lambda width: pl.BlockSpec((1, tm, width), lambda b, i: (b, i, 0))
    in_specs = [
        tok(d),
        pl.BlockSpec((None, None, N_MOD, d), lambda b, i: (layer, mod_row(b), 0, 0)),
        _layer_resident(layer, (1, d)),
        _layer_resident(layer, (d, D_IN)),
        _layer_resident(layer, (1, W_QK)),
        _resident((W_QK, W_QK)),
        _resident((W_FOURIER, 2 * W_FOURIER)),
    ]
    args = [x, mod6, g1, w_in, qk_gain, _head_sum_matrix(),
            jnp.asarray(_channel_dft()).astype(BF16)]
    if rope:
        cos2, sin2 = _rope_tables(n)
        in_specs += [pl.BlockSpec((tm, V7X_LANES), lambda b, i: (i, 0))] * 2
        args += [cos2, sin2]
    widths = (2 * W_FOURIER, 2 * W_CONV, W_POOL)
    out_specs = [
        pl.BlockSpec((1, N_Q_HEADS, HEAD_DIM, tm), lambda b, i: (b, 0, 0, i)),
        pl.BlockSpec((1, N_KV_HEADS, tm, K_AUG), lambda b, i: (b, 0, i, 0)),
        pl.BlockSpec((1, N_KV_HEADS, V_AUG, tm), lambda b, i: (b, 0, 0, i)),
    ] + [tok(w) for w in widths]
    out_shape = [
        jax.ShapeDtypeStruct((bsz, N_Q_HEADS, HEAD_DIM, n), BF16),
        jax.ShapeDtypeStruct((bsz, N_KV_HEADS, n, K_AUG), BF16),
        jax.ShapeDtypeStruct((bsz, N_KV_HEADS, V_AUG, n), BF16),
    ] + [jax.ShapeDtypeStruct((bsz, n, w), BF16) for w in widths]
    if rope:
        out_specs[3] = pl.BlockSpec((DFT_N1, 1, tm // DFT_N1, 2 * W_FOURIER),
                                    lambda b, i: (0, b, i, 0))
        out_shape[3] = jax.ShapeDtypeStruct((DFT_N1, bsz, n // DFT_N1, 2 * W_FOURIER), F32)
    return pl.pallas_call(
        functools.partial(_inproj_kernel, rope),
        grid=(bsz, nt),
        in_specs=in_specs,
        out_specs=out_specs,
        out_shape=out_shape,
        compiler_params=_params(2, 40),
        name="inproj_rope" if rope else "inproj_ctx",
    )(*args)


K_AUG = V7X_LANES
V_AUG = HEAD_DIM + V7X_BF16_ROWS
SAFE_SHIFT_LOG2 = 50.0


def _attn_kernel(chunks, q_ref, *refs):
    n_src = (len(refs) - 3) // 2
    k_refs, v_refs = refs[:n_src], refs[n_src:2 * n_src]
    o_ref, kmax_ref, acc_ref = refs[2 * n_src:]
    n_chunks = len(chunks)
    tq = q_ref.shape[-1]
    i = pl.program_id(2)

    def keys(c):
        src, start, size = chunks[c]
        return k_refs[src][0, 0, start:start + size, :]

    def values(c):
        src, start, size = chunks[c]
        return v_refs[src][0, 0, :, start:start + size]

    @pl.when(i == 0)
    def _():
        ones = jnp.ones((K_AUG, K_AUG), BF16)
        mx = jnp.zeros((1, K_AUG), F32)
        for c in range(n_chunks):
            kf = keys(c).astype(F32)
            norms = _dot((kf * kf).astype(BF16), ones)
            mx = jnp.maximum(mx, jnp.max(norms, axis=0, keepdims=True))
        kmax_ref[...] = jnp.sqrt(mx)

    q2 = jnp.concatenate([q_ref[0, 0], q_ref[0, 1]], axis=1)
    qf = q2.astype(F32)
    kmax = jnp.tile(kmax_ref[...], (1, 2 * tq // K_AUG))
    bound = jnp.sqrt(jnp.sum(qf * qf, axis=0, keepdims=True)) * kmax
    safe = jnp.max(bound) <= SAFE_SHIFT_LOG2
    row = lax.broadcasted_iota(jnp.int32, (K_AUG - HEAD_DIM, 2 * tq), 0)
    shift_rows = jnp.where(row == 0, -bound, 0.0).astype(BF16)
    q_aug = jnp.concatenate([q2, shift_rows], axis=0)

    def logits(c):
        return _dot(keys(c), q_aug)

    @pl.when(safe)
    def _():
        acc = None
        for c in range(n_chunks):
            p = jnp.exp2(logits(c)).astype(BF16)
            part = _dot(values(c), p)
            acc = part if acc is None else acc + part
        acc_ref[...] = acc

    @pl.when(jnp.logical_not(safe))
    def _():
        m = jnp.full((1, 2 * tq), NEG_BIG, F32)
        acc = jnp.zeros((V_AUG, 2 * tq), F32)
        for c in range(n_chunks):
            s = logits(c)
            m_new = jnp.maximum(m, jnp.max(s, axis=0, keepdims=True))
            p = jnp.exp2(s - m_new).astype(BF16)
            acc = jnp.exp2(m - m_new) * acc + _dot(values(c), p)
            m = m_new
        acc_ref[...] = acc

    o = acc_ref[0:HEAD_DIM, :] * (1.0 / acc_ref[HEAD_DIM:HEAD_DIM + 1, :])
    o2 = jnp.concatenate([o[:, :tq], o[:, tq:]], axis=0)
    o_ref[0] = o2.T.astype(BF16)


def _attention(q_t, kv_sources, tq, tk):
    bsz, _, _, nq = q_t.shape
    q_per_kv = N_Q_HEADS // N_KV_HEADS
    chunks = []
    for src, (k_aug, _) in enumerate(kv_sources):
        nk = k_aug.shape[2]
        step = min(tk, nk)
        chunks += [(src, start, step) for start in range(0, nk, step)]
    k_specs = [pl.BlockSpec((1, 1, k.shape[2], K_AUG), lambda b, g, i: (b, g, 0, 0))
               for k, _ in kv_sources]
    v_specs = [pl.BlockSpec((1, 1, V_AUG, v.shape[3]), lambda b, g, i: (b, g, 0, 0))
               for _, v in kv_sources]
    return pl.pallas_call(
        functools.partial(_attn_kernel, tuple(chunks)),
        grid=(bsz, N_KV_HEADS, nq // tq),
        in_specs=[pl.BlockSpec((1, q_per_kv, HEAD_DIM, tq), lambda b, g, i: (b, g, 0, i))]
        + k_specs + v_specs,
        out_specs=pl.BlockSpec((1, tq, q_per_kv * HEAD_DIM), lambda b, g, i: (b, i, g)),
        out_shape=jax.ShapeDtypeStruct((bsz, nq, W_ATTN), BF16),
        scratch_shapes=[pltpu.VMEM((1, K_AUG), F32),
                        pltpu.VMEM((V_AUG, q_per_kv * tq), F32)],
        compiler_params=pltpu.CompilerParams(
            dimension_semantics=("parallel", "parallel", "arbitrary"),
            vmem_limit_bytes=40 * 1024 * 1024),
        name="attention_%dsrc" % len(kv_sources),
    )(q_t, *[k for k, _ in kv_sources], *[v for _, v in kv_sources])


DFT_A_GROUP = 8


def _dft_a_kernel(bsz, g_ref, tr_ref, ti_ref, o_ref):
    n2 = g_ref.shape[2]
    for j in range(DFT_A_GROUP):
        tab_r = tr_ref[j].astype(BF16)
        tab_i = ti_ref[j].astype(BF16)
        for b in range(bsz):
            res = (_dot(tab_r, g_ref[j, b, :, :W_FOURIER].astype(BF16))
                   + _dot(tab_i, g_ref[j, b, :, W_FOURIER:].astype(BF16)))
            o_ref[0, j, :, b * W_FOURIER:(b + 1) * W_FOURIER] = res[:n2].astype(BF16)
            o_ref[1, j, :, b * W_FOURIER:(b + 1) * W_FOURIER] = res[n2:].astype(BF16)


def _dft_c_kernel(a_ref, tc_ref, o_ref):
    res = _dot(tc_ref[...], a_ref[...])
    k2s, lanes = o_ref.shape[1], o_ref.shape[2]
    for j in range(k2s):
        o_ref[:, j, :] = res[:, j * lanes:(j + 1) * lanes]


def _fourier_latent(g_t):
    n1, bsz, n2, _ = g_t.shape
    n = n1 * n2
    tab_r, tab_i, tab_c = _seq_dft_tables(n)
    lanes = bsz * W_FOURIER
    a = pl.pallas_call(
        functools.partial(_dft_a_kernel, bsz),
        grid=(n1 // DFT_A_GROUP,),
        in_specs=[
            pl.BlockSpec((DFT_A_GROUP, bsz, n2, 2 * W_FOURIER), lambda t: (t, 0, 0, 0)),
            pl.BlockSpec((DFT_A_GROUP, 2 * n2, n2), lambda t: (t, 0, 0)),
            pl.BlockSpec((DFT_A_GROUP, 2 * n2, n2), lambda t: (t, 0, 0)),
        ],
        out_specs=pl.BlockSpec((2, DFT_A_GROUP, n2, lanes), lambda t: (0, t, 0, 0)),
        out_shape=jax.ShapeDtypeStruct((2, n1, n2, lanes), BF16),
        compiler_params=_params(1, 40),
        name="dft_stage_a",
    )(g_t, tab_r, tab_i)
    a2 = a.reshape(2 * n1, n2 * lanes)
    k2s = V7X_SUBLANES
    y = pl.pallas_call(
        _dft_c_kernel,
        grid=(n2 // k2s,),
        in_specs=[pl.BlockSpec((2 * n1, k2s * lanes), lambda j: (0, j)),
                  _resident((n1, 2 * n1))],
        out_specs=pl.BlockSpec((n1, k2s, lanes), lambda j: (0, j, 0)),
        out_shape=jax.ShapeDtypeStruct((n1, n2, lanes), F32),
        compiler_params=_params(1, 40),
        name="dft_stage_c",
    )(a2, jnp.asarray(tab_c).astype(BF16))
    return y.reshape(n, lanes)


def _dft_ctx_kernel(g_ref, c_ref, s_ref, o_ref):
    o_ref[...] = (_dot(c_ref[...], g_ref[0, :, :W_FOURIER])
                  + _dot(s_ref[...], g_ref[0, :, W_FOURIER:])).astype(BF16)


def _fourier_ctx(gf):
    bsz, n, _ = gf.shape
    tab_cos, tab_sin = _ctx_dft_tables(n)
    return pl.pallas_call(
        _dft_ctx_kernel,
        grid=(bsz,),
        in_specs=[pl.BlockSpec((1, n, 2 * W_FOURIER), lambda b: (b, 0, 0)),
                  _resident((n, n)), _resident((n, n))],
        out_specs=pl.BlockSpec((n, W_FOURIER), lambda b: (0, b)),
        out_shape=jax.ShapeDtypeStruct((n, bsz * W_FOURIER), BF16),
        compiler_params=_params(1, 40),
        name="dft_ctx",
    )(gf, jnp.asarray(tab_cos).astype(BF16), jnp.asarray(tab_sin).astype(BF16))


CONV_ROWS = 64
CONV_TAP_ROWS = 32
FFN_CHUNKS = ((0, 768), (768, 768), (1536, 768), (2304, 512))


def _exact_zero(v):
    return jnp.minimum(jnp.abs(v), 0.0)


def _conv_pool_units(tile, n_tiles, n_seq, tm, am_ref, ap_ref, an_ref, um_ref, up_ref, un_ref,
                     dww_ref, cvec_ref, g_ref, gsh_ref, pu_ref, psh_ref, cv_ref, pl_ref):
    first = tile == 0
    last = tile == n_tiles - 1
    rows_sh = gsh_ref.shape[1]
    corner = lambda v: v[0:V7X_SUBLANES, 0:V7X_LANES]

    def gated(v, gate):
        if gate is None:
            return v
        rows, cols = v.shape
        return v + jnp.tile(gate, (rows // V7X_SUBLANES, cols // V7X_LANES))

    def glu(a_blk):
        a = a_blk.astype(F32)
        return a[:, :W_CONV] * _sigmoid(a[:, W_CONV:])

    def stage(gate):
        main = gated(glu(am_ref[0]), gate)
        g_ref[0:HALO] = jnp.where(first, 0.0, glu(ap_ref[0]))
        g_ref[HALO:HALO + tm] = main
        g_ref[HALO + tm:] = jnp.where(last, 0.0, glu(an_ref[0]))
        pu_ref[0:HALO] = jnp.where(first, 0.0, up_ref[0].astype(F32))
        pu_ref[HALO:HALO + tm] = um_ref[0].astype(F32)
        pu_ref[HALO + tm:] = jnp.where(last, 0.0, un_ref[0].astype(F32))
        return corner(main)

    def copy(b):
        def run(gate):
            gv = gated(g_ref[pl.ds(b, rows_sh), :], gate)
            gsh_ref[b - 1] = gv
            psh_ref[b - 1] = gated(pu_ref[pl.ds(b, rows_sh), :], gate)
            return corner(gv)
        return run

    lane = lax.broadcasted_iota(jnp.int32, (CONV_ROWS, W_POOL), 1)
    grp = lane // POOL_GROUP
    half = jnp.left_shift(1, grp)
    row = lax.broadcasted_iota(jnp.int32, (CONV_ROWS, W_POOL), 0)

    def shifted(base_ref, sh_ref, c0, j):
        src = base_ref if j % V7X_SUBLANES == 0 else sh_ref.at[j % V7X_SUBLANES - 1]
        return src[pl.ds(c0 + (j // V7X_SUBLANES) * V7X_SUBLANES, CONV_ROWS), :]

    def conv(c0):
        def run(gate):
            wts = gated(dww_ref[...], gate)
            acc = jnp.broadcast_to(cvec_ref[0:1, :], (CONV_ROWS, W_CONV))
            for k in range(CONV_WIDTH):
                acc = acc + wts[k:k + 1, :] * shifted(g_ref, gsh_ref, c0, k + 1)
            mu = jnp.mean(acc, axis=-1, keepdims=True)
            yc = acc - mu
            var = jnp.mean(yc * yc, axis=-1, keepdims=True)
            yn = yc * lax.rsqrt(var + EPS) * cvec_ref[1:2, :] + cvec_ref[2:3, :]
            act = yn * _sigmoid(yn)
            cv_ref[c0:c0 + CONV_ROWS, :] = act.astype(BF16)
            return corner(act)
        return run

    def pool(c0):
        def run(gate):
            ld = lambda d: shifted(pu_ref, psh_ref, c0, HALO + d)
            u0 = ld(0)
            s2 = gated(ld(-1) + u0, gate)
            s4 = s2 + ld(-2) + ld(1)
            s8 = s4 + ld(-4) + ld(-3) + ld(2) + ld(3)
            s16 = s8 + ld(-8) + ld(-7) + ld(-6) + ld(-5) + ld(4) + ld(5) + ld(6) + ld(7)
            win = jnp.where(grp == 0, s2, jnp.where(grp == 1, s4, jnp.where(grp == 2, s8, s16)))
            t = tile * tm + c0 + row
            cnt = jnp.minimum(t + half, n_seq) - jnp.maximum(t - half, 0)
            centred = win / cnt.astype(F32) - u0
            pl_ref[c0:c0 + CONV_ROWS, :] = centred.astype(BF16)
            return corner(centred)
        return run

    units = [stage] + [copy(b) for b in range(1, V7X_SUBLANES)]
    for c in range(tm // CONV_ROWS):
        units += [conv(c * CONV_ROWS), pool(c * CONV_ROWS)]
    return units


def _mixffn_kernel(n_tiles, n_seq, x_ref, attn_ref, yf_ref,
                   amc_ref, apc_ref, anc_ref, umc_ref, upc_ref, unc_ref,
                   amn_ref, apn_ref, ann_ref, umn_ref, upn_ref, unn_ref,
                   mod_ref, g2_ref, wf_ref, dww_ref, cvec_ref, wpw_ref, wpl_ref, wo_ref,
                   wfi_ref, wfo_ref, o_ref, g_ref, gsh_ref, pu_ref, psh_ref, cv_ref, pl_ref):
    tm = x_ref.shape[1]
    s = pl.program_id(0)
    n_steps = pl.num_programs(0)
    scratch = (dww_ref, cvec_ref, g_ref, gsh_ref, pu_ref, psh_ref, cv_ref, pl_ref)

    @pl.when(s == 0)
    def _():
        for unit in _conv_pool_units(0, n_tiles, n_seq, tm, amc_ref, apc_ref, anc_ref, umc_ref,
                                     upc_ref, unc_ref, *scratch):
            unit(None)

    cv = cv_ref[...]
    pooled = pl_ref[...]
    nxt_tile = jnp.minimum(s + 1, n_steps - 1) % n_tiles
    units = _conv_pool_units(nxt_tile, n_tiles, n_seq, tm, amn_ref, apn_ref, ann_ref, umn_ref,
                             upn_ref, unn_ref, *scratch)
    n_pre = V7X_SUBLANES
    slots = [units[:n_pre // 2], units[n_pre // 2:n_pre]]
    slots += [units[i:i + 2] for i in range(n_pre, len(units), 2)]

    def run_slot(gate_src):
        if not slots:
            return []
        gate = None if gate_src is None else _exact_zero(
            gate_src[-V7X_SUBLANES:, -V7X_LANES:])
        return [unit(gate) for unit in slots.pop(0)]

    def tied(lhs_f32, tokens):
        lhs = lhs_f32.astype(BF16)
        if not tokens:
            return lhs
        zero = sum(_exact_zero(t) for t in tokens)
        head = lhs_f32[0:V7X_BF16_ROWS] + jnp.tile(
            zero, (V7X_BF16_ROWS // V7X_SUBLANES, lhs_f32.shape[1] // V7X_LANES))
        return jnp.concatenate([head.astype(BF16), lhs[V7X_BF16_ROWS:]], axis=0)

    tokens = run_slot(None)
    four = _dot(yf_ref[...].astype(BF16), wf_ref[...]).astype(BF16)
    conv = _dot(cv, wpw_ref[...]).astype(BF16)
    pool = (_dot(pooled, wpl_ref[...]) * cvec_ref[3:4, :]).astype(BF16)
    mix = (_dot(attn_ref[0], wo_ref[0:W_ATTN, :])
           + _dot(four, wo_ref[W_ATTN:2 * W_ATTN, :])
           + _dot(conv, wo_ref[2 * W_ATTN:3 * W_ATTN, :])
           + _dot(pool, wo_ref[3 * W_ATTN:, :]))
    x1 = x_ref[0] + mod_ref[MOD_GATE1:MOD_GATE1 + 1, :] * mix

    ms = jnp.mean(x1 * x1, axis=-1, keepdims=True)
    gain = g2_ref[...] * (1.0 + mod_ref[MOD_SCALE2:MOD_SCALE2 + 1, :])
    xn_f32 = (x1 * lax.rsqrt(ms + EPS)) * gain + mod_ref[MOD_SHIFT2:MOD_SHIFT2 + 1, :]
    prev = mix
    acc = None
    for lo, size in FFN_CHUNKS:
        lhs = tied(xn_f32, tokens)
        tokens = run_slot(prev)
        a = _dot(lhs, wfi_ref[:, lo:lo + size])
        lhs = tied(xn_f32, tokens)
        tokens = run_slot(a)
        g = _dot(lhs, wfi_ref[:, D_FF + lo:D_FF + lo + size])
        lhs = tied((a * _sigmoid(a)) * g, tokens)
        tokens = run_slot(g)
        part = _dot(lhs, wfo_ref[lo:lo + size, :])
        prev = part
        acc = part if acc is None else acc + part
    assert not slots
    out = x1 + mod_ref[MOD_GATE2:MOD_GATE2 + 1, :] * acc
    if tokens:
        zero = sum(_exact_zero(t) for t in tokens)
        out = out + jnp.tile(zero, (tm // V7X_SUBLANES, out.shape[1] // V7X_LANES))
    o_ref[0] = out


def _mix_ffn(layer, x, attn, yf, a_conv, u_pool, mod6, mod_row, g2, w_f, dw_w, cvec, w_pw,
             w_pool_bd, w_o, w_fi, w_fo, tm):
    bsz, n, d = x.shape
    nt = n // tm
    n_steps = bsz * nt
    hb = tm // HALO
    last_halo = n // HALO - 1

    def nxt(s):
        s1 = jnp.minimum(s + 1, n_steps - 1)
        return s1 // nt, s1 % nt

    tok = lambda width: pl.BlockSpec((1, tm, width), lambda s: (s // nt, s % nt, 0))
    cur_main = lambda w: pl.BlockSpec((1, tm, w), lambda s: (0, 0, 0))
    cur_prev = lambda w: pl.BlockSpec((1, HALO, w), lambda s: (0, 0, 0))
    cur_next = lambda w: pl.BlockSpec((1, HALO, w), lambda s: (0, min(hb, last_halo), 0))
    nxt_main = lambda w: pl.BlockSpec((1, tm, w), lambda s: (nxt(s)[0], nxt(s)[1], 0))
    nxt_prev = lambda w: pl.BlockSpec(
        (1, HALO, w), lambda s: (nxt(s)[0], jnp.maximum(nxt(s)[1] * hb - 1, 0), 0))
    nxt_next = lambda w: pl.BlockSpec(
        (1, HALO, w), lambda s: (nxt(s)[0], jnp.minimum((nxt(s)[1] + 1) * hb, last_halo), 0))
    rows_sh = tm + 2 * HALO - V7X_SUBLANES
    return pl.pallas_call(
        functools.partial(_mixffn_kernel, nt, n),
        grid=(n_steps,),
        in_specs=[
            tok(d), tok(W_ATTN),
            pl.BlockSpec((tm, W_FOURIER), lambda s: (s % nt, s // nt)),
            cur_main(2 * W_CONV), cur_prev(2 * W_CONV), cur_next(2 * W_CONV),
            cur_main(W_POOL), cur_prev(W_POOL), cur_next(W_POOL),
            nxt_main(2 * W_CONV), nxt_prev(2 * W_CONV), nxt_next(2 * W_CONV),
            nxt_main(W_POOL), nxt_prev(W_POOL), nxt_next(W_POOL),
            pl.BlockSpec((None, None, N_MOD, d), lambda s: (layer, mod_row(s // nt), 0, 0)),
            _layer_resident(layer, (1, d)),
            _layer_resident(layer, (W_FOURIER, W_FOURIER)),
            _layer_resident(layer, (CONV_TAP_ROWS, W_CONV)),
            _layer_resident(layer, (V7X_SUBLANES, W_CONV)),
            _layer_resident(layer, (W_CONV, W_CONV)),
            _layer_resident(layer, (W_POOL, W_POOL)),
            _layer_resident(layer, (4 * W_ATTN, d)),
            _layer_resident(layer, (d, 2 * D_FF)),
            _layer_resident(layer, (D_FF, d)),
        ],
        out_specs=tok(d),
        out_shape=jax.ShapeDtypeStruct((bsz, n, d), F32),
        scratch_shapes=[
            pltpu.VMEM((tm + 2 * HALO, W_CONV), F32),
            pltpu.VMEM((V7X_SUBLANES - 1, rows_sh, W_CONV), F32),
            pltpu.VMEM((tm + 2 * HALO, W_POOL), F32),
            pltpu.VMEM((V7X_SUBLANES - 1, rows_sh, W_POOL), F32),
            pltpu.VMEM((tm, W_CONV), BF16),
            pltpu.VMEM((tm, W_POOL), BF16),
        ],
        compiler_params=pltpu.CompilerParams(
            dimension_semantics=("arbitrary",), vmem_limit_bytes=56 * 1024 * 1024),
        name="mix_ffn_n%d" % n,
    )(x, attn, yf, a_conv, a_conv, a_conv, u_pool, u_pool, u_pool,
      a_conv, a_conv, a_conv, u_pool, u_pool, u_pool, mod6, g2,
      w_f, dw_w, cvec, w_pw, w_pool_bd, w_o, w_fi, w_fo)


def _pool_block_diag(w_pool):
    depth, groups = w_pool.shape[:2]
    eye = jnp.eye(groups, dtype=w_pool.dtype)
    blocks = w_pool[:, :, :, None, :] * eye[None, :, None, :, None]
    return blocks.reshape(depth, W_POOL, W_POOL)


def kernel(x, c, ctx, c_ctx, w_mod, b_mod, g_norm1, g_norm2, w_in, q_norm_g, k_norm_g,
           w_fourier, conv_dw_w, conv_dw_b, conv_ln_g, conv_ln_b, w_conv_pw, w_pool,
           pool_scale, w_out, w_ffn_in, w_ffn_out):
    bsz, n, d = x.shape
    n_ctx = ctx.shape[1]
    depth = w_in.shape[0]
    tm_lat, tm_ctx = 512, n_ctx
    tq_lat, tk_lat = 512, 512

    cc = jnp.concatenate([c, c_ctx[None, :],
                          jnp.zeros((V7X_SUBLANES - bsz - 1, d), F32)], axis=0)
    mod6 = _modulation(cc, w_mod, b_mod).reshape(depth, V7X_SUBLANES, N_MOD, d)
    lat_row = lambda b: b
    ctx_row = lambda b: bsz

    w_in_b = w_in.astype(BF16)
    w_f = w_fourier.astype(BF16)
    w_pw = w_conv_pw.astype(BF16)
    w_pl = _pool_block_diag(w_pool).astype(BF16)
    w_o = w_out.astype(BF16)
    w_fi = w_ffn_in.astype(BF16)
    w_fo = w_ffn_out.astype(BF16)
    qk_gain = jnp.concatenate([jnp.tile(q_norm_g, (1, N_Q_HEADS)),
                               jnp.tile(k_norm_g, (1, N_KV_HEADS))], axis=1)[:, None, :]
    g1 = g_norm1[:, None, :]
    g2 = g_norm2[:, None, :]
    dw_taps = jnp.concatenate(
        [conv_dw_w, jnp.zeros((depth, CONV_TAP_ROWS - CONV_WIDTH, W_CONV), F32)], axis=1)
    cvec = jnp.concatenate(
        [conv_dw_b[:, None], conv_ln_g[:, None], conv_ln_b[:, None], pool_scale[:, None],
         jnp.zeros((depth, V7X_SUBLANES - 4, W_CONV), F32)], axis=1)

    h = ctx
    for l in range(DEPTH):
        last = l == DEPTH - 1
        qc, kc, vc, gfc, ac, uc = _inproj(l, h, mod6, ctx_row, g1, w_in_b, qk_gain, False, tm_ctx)
        q, k, v, gf, a, u = _inproj(l, x, mod6, lat_row, g1, w_in_b, qk_gain, True, 2 * tm_lat)

        attn = _attention(q, [(kc, vc), (k, v)], tq_lat, tk_lat)
        yf = _fourier_latent(gf)
        x_new = _mix_ffn(l, x, attn, yf, a, u, mod6, lat_row, g2, w_f, dw_taps, cvec, w_pw, w_pl,
                         w_o, w_fi, w_fo, tm_lat)

        if not last:
            attn_c = _attention(qc, [(kc, vc)], n_ctx, n_ctx)
            yfc = _fourier_ctx(gfc)
            h = _mix_ffn(l, h, attn_c, yfc, ac, uc, mod6, ctx_row, g2, w_f, dw_taps, cvec, w_pw,
                         w_pl, w_o, w_fi, w_fo, tm_ctx)
        x = x_new
    return x
```

```python
import functools

import numpy as np
import jax
import jax.numpy as jnp
from jax import lax
from jax.experimental import pallas as pl
from jax.experimental.pallas import tpu as pltpu

F32 = jnp.float32
BF16 = jnp.bfloat16

D_MODEL = 1024
DEPTH = 2
GRID_W = 64
HEAD_DIM = 64
N_Q_HEADS = 4
N_KV_HEADS = 2
W_ATTN = 256
W_FOURIER = 256
W_CONV = 256
W_POOL = 256
CONV_WIDTH = 31
POOL_GROUP = 64
D_FF = 2816
ROPE_THETA = 10000.0
EPS = 1e-6
ATTN_SCALE = HEAD_DIM ** -0.5
Q_SCALE_LOG2 = ATTN_SCALE * float(np.log2(np.e))
W_QK = W_ATTN + N_KV_HEADS * HEAD_DIM
OFF_V = W_QK
OFF_F = OFF_V + N_KV_HEADS * HEAD_DIM
OFF_C = OFF_F + W_FOURIER
OFF_P = OFF_C + 2 * W_CONV
D_IN = OFF_P + W_POOL

V7X_LANES = 128
V7X_SUBLANES = 8
V7X_BF16_ROWS = 16
V7X_VMEM_BYTES = 64 * 1024 * 1024

MOD_SHIFT1, MOD_SCALE1, MOD_GATE1, MOD_SHIFT2, MOD_SCALE2, MOD_GATE2 = range(6)
N_MOD = 6

HALO = V7X_BF16_ROWS
DFT_N1 = 64
NEG_BIG = -1e30


def _dot(a, b):
    return jnp.dot(a, b, preferred_element_type=F32)


def _sigmoid(x):
    return 1.0 / (1.0 + jnp.exp(-x))


def _split_bf16(x):
    hi = x.astype(BF16)
    lo = (x - hi.astype(F32)).astype(BF16)
    return hi, lo


def _resident(shape):
    nd = len(shape)
    return pl.BlockSpec(shape, lambda *_: (0,) * nd, pipeline_mode=pl.Buffered(1))


def _layer_resident(layer, shape):
    nd = len(shape)
    return pl.BlockSpec((None,) + tuple(shape), lambda *_: (layer,) + (0,) * nd,
                        pipeline_mode=pl.Buffered(1))


def _params(n_grid, vmem_mb):
    return pltpu.CompilerParams(
        dimension_semantics=("parallel",) * n_grid,
        vmem_limit_bytes=vmem_mb * 1024 * 1024)


@functools.lru_cache(maxsize=None)
def _rope_tables(n):
    n_freq = HEAD_DIM // 4
    inv_freq = ROPE_THETA ** (-np.arange(n_freq, dtype=np.float64) / n_freq)
    sign = np.tile(np.array([-1.0, 1.0]), HEAD_DIM // 2)

    def lanes(pos, first_half):
        ang = pos[:, None] * inv_freq
        zero = np.zeros_like(ang)
        halves = [ang, zero] if first_half else [zero, ang]
        live = np.repeat(np.concatenate([np.ones_like(ang) * first_half,
                                         np.ones_like(ang) * (not first_half)], axis=1), 2, axis=1)
        full = np.repeat(np.concatenate(halves, axis=1), 2, axis=1)
        cos = np.cos(full) * live
        sin = np.sin(full) * live * sign
        return (np.tile(cos, (1, 2)).astype(np.float32), np.tile(sin, (1, 2)).astype(np.float32))

    row_cos, row_sin = lanes(np.arange(n // GRID_W, dtype=np.float64), True)
    col_cos, col_sin = lanes(np.arange(GRID_W, dtype=np.float64), False)
    return row_cos, row_sin, col_cos, col_sin


@functools.lru_cache(maxsize=None)
def _head_sum_matrix():
    idx = np.arange(W_QK) // HEAD_DIM
    return np.asarray((idx[:, None] == idx[None, :]).astype(np.float32), dtype=BF16)


@functools.lru_cache(maxsize=None)
def _channel_dft():
    c = np.arange(W_FOURIER)
    same = (c[:, None] // HEAD_DIM) == (c[None, :] // HEAD_DIM)
    ang = 2.0 * np.pi * ((c[:, None] % HEAD_DIM) * (c[None, :] % HEAD_DIM)) / HEAD_DIM
    cr = np.where(same, np.cos(ang), 0.0) / 8.0
    ci = np.where(same, -np.sin(ang), 0.0) / 8.0
    return np.concatenate([cr, ci], axis=1).astype(np.float32)


@functools.lru_cache(maxsize=None)
def _seq_dft_tables(n):
    n1, n2 = DFT_N1, n // DFT_N1
    t1 = np.arange(n1)[:, None, None]
    k2 = np.arange(n2)[None, :, None]
    t2 = np.arange(n2)[None, None, :]
    theta = 2.0 * np.pi * ((k2 * (t1 + n1 * t2)) % n) / n
    dr = np.cos(theta) / np.sqrt(n2)
    di = -np.sin(theta) / np.sqrt(n2)
    tab_a = np.concatenate([dr, di], axis=1)
    k1 = np.arange(n1)[:, None]
    tt = np.arange(n1)[None, :]
    phi = 2.0 * np.pi * ((k1 * tt) % n1) / n1
    tab_c = np.concatenate([np.cos(phi), np.sin(phi)], axis=1) / np.sqrt(n1)
    return tab_a.astype(np.float32), tab_c.astype(np.float32)


@functools.lru_cache(maxsize=None)
def _ctx_dft_tables(n):
    k = np.arange(n)[:, None]
    t = np.arange(n)[None, :]
    ang = 2.0 * np.pi * ((k * t) % n) / n
    s = 1.0 / np.sqrt(n)
    return (np.cos(ang) * s).astype(np.float32), (np.sin(ang) * s).astype(np.float32)


def _mod_kernel(c_ref, w_ref, b_ref, o_ref):
    c = c_ref[...]
    sc = c * _sigmoid(c)
    a_hi, a_lo = _split_bf16(sc)
    w_hi, w_lo = _split_bf16(w_ref[0])
    o_ref[0] = _dot(a_hi, w_hi) + _dot(a_lo, w_hi) + _dot(a_hi, w_lo) + b_ref[0]


def _modulation(cc, w_mod, b_mod):
    depth, d, d6 = w_mod.shape
    tn = 1536
    return pl.pallas_call(
        _mod_kernel,
        grid=(depth, d6 // tn),
        in_specs=[
            pl.BlockSpec((V7X_SUBLANES, d), lambda l, j: (0, 0)),
            pl.BlockSpec((1, d, tn), lambda l, j: (l, 0, j)),
            pl.BlockSpec((1, 1, tn), lambda l, j: (l, 0, j)),
        ],
        out_specs=pl.BlockSpec((1, V7X_SUBLANES, tn), lambda l, j: (l, 0, j)),
        out_shape=jax.ShapeDtypeStruct((depth, V7X_SUBLANES, d6), F32),
        compiler_params=_params(2, 40),
        name="modulation",
    )(cc, w_mod, b_mod.reshape(depth, 1, d6))


INPROJ_ROWS = 256


def _rope(x, cos, sin):
    lane = lax.broadcasted_iota(jnp.int32, x.shape, 1)
    swapped = jnp.where((lane & 1) == 0,
                        pltpu.roll(x, V7X_LANES - 1, axis=1),
                        pltpu.roll(x, 1, axis=1))
    return x * cos + swapped * sin


def _inproj_kernel(rope, x_ref, mod_ref, g1_ref, w_ref, qkg_ref, bd_ref, cd_ref, *rest):
    if rope:
        rcos_ref, rsin_ref, ccos_ref, csin_ref = rest[:4]
        rest = rest[4:]
    q_ref, k_ref, v_ref, gf_ref, a_ref, u_ref = rest
    tm = x_ref.shape[1]
    rg = min(tm, INPROJ_ROWS)
    gain = g1_ref[...] * (1.0 + mod_ref[MOD_SCALE1:MOD_SCALE1 + 1, :])
    lane = lax.broadcasted_iota(jnp.int32, (rg, V7X_LANES), 1)
    one_col = jnp.where(lane == HEAD_DIM, 1.0, 0.0)
    row = lax.broadcasted_iota(jnp.int32, (V_AUG - HEAD_DIM, rg), 0)
    ones_rows = jnp.where(row == 0, 1.0, 0.0).astype(BF16)
    for r0 in range(0, tm, rg):
        rows = slice(r0, r0 + rg)
        x = x_ref[0, rows, :]
        ms = jnp.mean(x * x, axis=-1, keepdims=True)
        xn = (x * lax.rsqrt(ms + EPS)) * gain + mod_ref[MOD_SHIFT1:MOD_SHIFT1 + 1, :]
        p = _dot(xn.astype(BF16), w_ref[...])

        qk = p[:, :W_QK]
        ss = _dot((qk * qk).astype(BF16), bd_ref[...])
        qkn = qk * lax.rsqrt(ss * (1.0 / HEAD_DIM) + EPS) * qkg_ref[...]
        if rope:
            grid_rows = range(r0 // GRID_W, (r0 + rg) // GRID_W)
            cos = jnp.concatenate([rcos_ref[g:g + 1, :] + ccos_ref[...] for g in grid_rows], axis=0)
            sin = jnp.concatenate([rsin_ref[g:g + 1, :] + csin_ref[...] for g in grid_rows], axis=0)
        slabs = []
        for j in range(W_QK // V7X_LANES):
            slab = qkn[:, j * V7X_LANES:(j + 1) * V7X_LANES]
            if rope:
                slab = _rope(slab, cos, sin)
            slabs.append(slab)
        for j in range(W_ATTN // V7X_LANES):
            qt = (slabs[j] * Q_SCALE_LOG2).T.astype(BF16)
            q_ref[0, 2 * j, :, rows] = qt[:HEAD_DIM]
            q_ref[0, 2 * j + 1, :, rows] = qt[HEAD_DIM:]
        k_ref[0, 0, rows, :] = jnp.where(lane < HEAD_DIM, slabs[2], one_col).astype(BF16)
        k_ref[0, 1, rows, :] = jnp.where(lane < HEAD_DIM, pltpu.roll(slabs[2], HEAD_DIM, axis=1),
                                         one_col).astype(BF16)
        vt = p[:, OFF_V:OFF_F].T.astype(BF16)
        for g in range(N_KV_HEADS):
            v_ref[0, g, 0:HEAD_DIM, rows] = vt[g * HEAD_DIM:(g + 1) * HEAD_DIM]
            v_ref[0, g, HEAD_DIM:, rows] = ones_rows
        gf_ref[0, rows, :] = _dot(p[:, OFF_F:OFF_C].astype(BF16), cd_ref[...]).astype(BF16)
        a_ref[0, rows, :] = p[:, OFF_C:OFF_P].astype(BF16)
        u_ref[0, rows, :] = p[:, OFF_P:].astype(BF16)


def _inproj(layer, x, mod6, mod_row, g1, w_in, qk_gain, rope, tm):
    bsz, n, d = x.shape
    nt = n // tm
    tok = lambda width: pl.BlockSpec((1, tm, width), lambda b, i: (b, i, 0))
    in_specs = [
        tok(d),
        pl.BlockSpec((None, None, N_MOD, d), lambda b, i: (layer, mod_row(b), 0, 0)),
        _layer_resident(layer, (1, d)),
        _layer_resident(layer, (d, D_IN)),
        _layer_resident(layer, (1, W_QK)),
        _resident((W_QK, W_QK)),
        _resident((W_FOURIER, 2 * W_FOURIER)),
    ]
    args = [x, mod6, g1, w_in, qk_gain, _head_sum_matrix(),
            jnp.asarray(_channel_dft()).astype(BF16)]
    if rope:
        row_cos, row_sin, col_cos, col_sin = _rope_tables(n)
        in_specs += [pl.BlockSpec((tm // GRID_W, V7X_LANES), lambda b, i: (i, 0))] * 2
        in_specs += [_resident((GRID_W, V7X_LANES))] * 2
        args += [row_cos, row_sin, col_cos, col_sin]
    widths = (2 * W_FOURIER, 2 * W_CONV, W_POOL)
    out_specs = [
        pl.BlockSpec((1, N_Q_HEADS, HEAD_DIM, tm), lambda b, i: (b, 0, 0, i)),
        pl.BlockSpec((1, N_KV_HEADS, tm, K_AUG), lambda b, i: (b, 0, i, 0)),
        pl.BlockSpec((1, N_KV_HEADS, V_AUG, tm), lambda b, i: (b, 0, 0, i)),
    ] + [tok(w) for w in widths]
    out_shape = [
        jax.ShapeDtypeStruct((bsz, N_Q_HEADS, HEAD_DIM, n), BF16),
        jax.ShapeDtypeStruct((bsz, N_KV_HEADS, n, K_AUG), BF16),
        jax.ShapeDtypeStruct((bsz, N_KV_HEADS, V_AUG, n), BF16),
    ] + [jax.ShapeDtypeStruct((bsz, n, w), BF16) for w in widths]
    return pl.pallas_call(
        functools.partial(_inproj_kernel, rope),
        grid=(bsz, nt),
        in_specs=in_specs,
        out_specs=out_specs,
        out_shape=out_shape,
        compiler_params=_params(2, 40),
        name="inproj_rope" if rope else "inproj_ctx",
    )(*args)


K_AUG = V7X_LANES
V_AUG = HEAD_DIM + V7X_BF16_ROWS
SAFE_SHIFT_LOG2 = 50.0


def _attn_kernel(chunks, q_ref, *refs):
    n_src = (len(refs) - 3) // 2
    k_refs, v_refs = refs[:n_src], refs[n_src:2 * n_src]
    o_ref, kmax_ref, acc_ref = refs[2 * n_src:]
    n_chunks = len(chunks)
    tq = q_ref.shape[-1]
    i = pl.program_id(2)

    def keys(c):
        src, start, size = chunks[c]
        return k_refs[src][0, 0, start:start + size, :]

    def values(c):
        src, start, size = chunks[c]
        return v_refs[src][0, 0, :, start:start + size]

    @pl.when(i == 0)
    def _():
        ones = jnp.ones((K_AUG, K_AUG), BF16)
        mx = jnp.zeros((1, K_AUG), F32)
        for c in range(n_chunks):
            kf = keys(c).astype(F32)
            norms = _dot((kf * kf).astype(BF16), ones)
            mx = jnp.maximum(mx, jnp.max(norms, axis=0, keepdims=True))
        kmax_ref[...] = jnp.sqrt(mx)

    q2 = jnp.concatenate([q_ref[0, 0], q_ref[0, 1]], axis=1)
    qf = q2.astype(F32)
    kmax = jnp.tile(kmax_ref[...], (1, 2 * tq // K_AUG))
    bound = jnp.sqrt(jnp.sum(qf * qf, axis=0, keepdims=True)) * kmax
    safe = jnp.max(bound) <= SAFE_SHIFT_LOG2
    row = lax.broadcasted_iota(jnp.int32, (K_AUG - HEAD_DIM, 2 * tq), 0)
    shift_rows = jnp.where(row == 0, -bound, 0.0).astype(BF16)
    q_aug = jnp.concatenate([q2, shift_rows], axis=0)

    def logits(c):
        return _dot(keys(c), q_aug)

    @pl.when(safe)
    def _():
        acc = None
        for c in range(n_chunks):
            p = jnp.exp2(logits(c)).astype(BF16)
            part = _dot(values(c), p)
            acc = part if acc is None else acc + part
        acc_ref[...] = acc

    @pl.when(jnp.logical_not(safe))
    def _():
        m = jnp.full((1, 2 * tq), NEG_BIG, F32)
        acc = jnp.zeros((V_AUG, 2 * tq), F32)
        for c in range(n_chunks):
            s = logits(c)
            m_new = jnp.maximum(m, jnp.max(s, axis=0, keepdims=True))
            p = jnp.exp2(s - m_new).astype(BF16)
            acc = jnp.exp2(m - m_new) * acc + _dot(values(c), p)
            m = m_new
        acc_ref[...] = acc

    o = acc_ref[0:HEAD_DIM, :] * (1.0 / acc_ref[HEAD_DIM:HEAD_DIM + 1, :])
    o2 = jnp.concatenate([o[:, :tq], o[:, tq:]], axis=0)
    o_ref[0] = o2.T.astype(BF16)


def _attention(q_t, kv_sources, tq, tk):
    bsz, _, _, nq = q_t.shape
    q_per_kv = N_Q_HEADS // N_KV_HEADS
    chunks = []
    for src, (k_aug, _) in enumerate(kv_sources):
        nk = k_aug.shape[2]
        step = min(tk, nk)
        chunks += [(src, start, step) for start in range(0, nk, step)]
    k_specs = [pl.BlockSpec((1, 1, k.shape[2], K_AUG), lambda b, g, i: (b, g, 0, 0))
               for k, _ in kv_sources]
    v_specs = [pl.BlockSpec((1, 1, V_AUG, v.shape[3]), lambda b, g, i: (b, g, 0, 0))
               for _, v in kv_sources]
    return pl.pallas_call(
        functools.partial(_attn_kernel, tuple(chunks)),
        grid=(bsz, N_KV_HEADS, nq // tq),
        in_specs=[pl.BlockSpec((1, q_per_kv, HEAD_DIM, tq), lambda b, g, i: (b, g, 0, i))]
        + k_specs + v_specs,
        out_specs=pl.BlockSpec((1, tq, q_per_kv * HEAD_DIM), lambda b, g, i: (b, i, g)),
        out_shape=jax.ShapeDtypeStruct((bsz, nq, W_ATTN), BF16),
        scratch_shapes=[pltpu.VMEM((1, K_AUG), F32),
                        pltpu.VMEM((V_AUG, q_per_kv * tq), F32)],
        compiler_params=pltpu.CompilerParams(
            dimension_semantics=("parallel", "parallel", "arbitrary"),
            vmem_limit_bytes=40 * 1024 * 1024),
        name="attention_%dsrc" % len(kv_sources),
    )(q_t, *[k for k, _ in kv_sources], *[v for _, v in kv_sources])


DFT_A_GROUP = 8


def _dft_a_kernel(bsz, g_ref, ta_ref, o_ref):
    n2 = g_ref.shape[2]
    for j in range(DFT_A_GROUP):
        tab = ta_ref[j]
        tab_r = tab.astype(BF16)
        tab_i = jnp.concatenate([-tab[n2:], tab[:n2]], axis=0).astype(BF16)
        for b in range(bsz):
            res = (_dot(tab_r, g_ref[j, b, :, :W_FOURIER])
                   + _dot(tab_i, g_ref[j, b, :, W_FOURIER:]))
            o_ref[0, j, :, b * W_FOURIER:(b + 1) * W_FOURIER] = res[:n2].astype(BF16)
            o_ref[1, j, :, b * W_FOURIER:(b + 1) * W_FOURIER] = res[n2:].astype(BF16)


def _dft_c_kernel(a_ref, tc_ref, o_ref):
    res = _dot(tc_ref[...], a_ref[...])
    k2s, lanes = o_ref.shape[1], o_ref.shape[2]
    for j in range(k2s):
        o_ref[:, j, :] = res[:, j * lanes:(j + 1) * lanes]


def _fourier_latent(gf):
    bsz, n, _ = gf.shape
    n1, n2 = DFT_N1, n // DFT_N1
    tab_a, tab_c = _seq_dft_tables(n)
    g_t = jnp.transpose(gf.reshape(bsz, n2, n1, 2 * W_FOURIER), (2, 0, 1, 3))
    lanes = bsz * W_FOURIER
    a = pl.pallas_call(
        functools.partial(_dft_a_kernel, bsz),
        grid=(n1 // DFT_A_GROUP,),
        in_specs=[
            pl.BlockSpec((DFT_A_GROUP, bsz, n2, 2 * W_FOURIER), lambda t: (t, 0, 0, 0)),
            pl.BlockSpec((DFT_A_GROUP, 2 * n2, n2), lambda t: (t, 0, 0)),
        ],
        out_specs=pl.BlockSpec((2, DFT_A_GROUP, n2, lanes), lambda t: (0, t, 0, 0)),
        out_shape=jax.ShapeDtypeStruct((2, n1, n2, lanes), BF16),
        compiler_params=_params(1, 40),
        name="dft_stage_a",
    )(g_t, tab_a)
    a2 = a.reshape(2 * n1, n2 * lanes)
    k2s = V7X_SUBLANES
    y = pl.pallas_call(
        _dft_c_kernel,
        grid=(n2 // k2s,),
        in_specs=[pl.BlockSpec((2 * n1, k2s * lanes), lambda j: (0, j)),
                  _resident((n1, 2 * n1))],
        out_specs=pl.BlockSpec((n1, k2s, lanes), lambda j: (0, j, 0)),
        out_shape=jax.ShapeDtypeStruct((n1, n2, lanes), F32),
        compiler_params=_params(1, 40),
        name="dft_stage_c",
    )(a2, jnp.asarray(tab_c).astype(BF16))
    return y.reshape(n, lanes)


def _dft_ctx_kernel(g_ref, c_ref, s_ref, o_ref):
    o_ref[...] = (_dot(c_ref[...], g_ref[0, :, :W_FOURIER])
                  + _dot(s_ref[...], g_ref[0, :, W_FOURIER:])).astype(BF16)


def _fourier_ctx(gf):
    bsz, n, _ = gf.shape
    tab_cos, tab_sin = _ctx_dft_tables(n)
    return pl.pallas_call(
        _dft_ctx_kernel,
        grid=(bsz,),
        in_specs=[pl.BlockSpec((1, n, 2 * W_FOURIER), lambda b: (b, 0, 0)),
                  _resident((n, n)), _resident((n, n))],
        out_specs=pl.BlockSpec((n, W_FOURIER), lambda b: (0, b)),
        out_shape=jax.ShapeDtypeStruct((n, bsz * W_FOURIER), BF16),
        compiler_params=_params(1, 40),
        name="dft_ctx",
    )(gf, jnp.asarray(tab_cos).astype(BF16), jnp.asarray(tab_sin).astype(BF16))


CONV_ROWS = 64
CONV_TAP_ROWS = 32
FFN_CHUNKS = ((0, 768), (768, 768), (1536, 768), (2304, 512))


def _exact_zero(v):
    return jnp.minimum(jnp.abs(v), 0.0)


def _conv_pool_units(tile, n_tiles, n_seq, tm, am_ref, ap_ref, an_ref, um_ref, up_ref, un_ref,
                     dww_ref, cvec_ref, g_ref, gsh_ref, pu_ref, psh_ref, cv_ref, pl_ref):
    first = tile == 0
    last = tile == n_tiles - 1
    rows_sh = gsh_ref.shape[1]
    corner = lambda v: v[0:V7X_SUBLANES, 0:V7X_LANES]

    def gated(v, gate):
        if gate is None:
            return v
        rows, cols = v.shape
        return v + jnp.tile(gate, (rows // V7X_SUBLANES, cols // V7X_LANES))

    def glu(a_blk):
        a = a_blk.astype(F32)
        return a[:, :W_CONV] * _sigmoid(a[:, W_CONV:])

    def stage(gate):
        main = gated(glu(am_ref[0]), gate)
        g_ref[0:HALO] = jnp.where(first, 0.0, glu(ap_ref[0]))
        g_ref[HALO:HALO + tm] = main
        g_ref[HALO + tm:] = jnp.where(last, 0.0, glu(an_ref[0]))
        pu_ref[0:HALO] = jnp.where(first, 0.0, up_ref[0].astype(F32))
        pu_ref[HALO:HALO + tm] = um_ref[0].astype(F32)
        pu_ref[HALO + tm:] = jnp.where(last, 0.0, un_ref[0].astype(F32))
        return corner(main)

    def copy(b):
        def run(gate):
            gv = gated(g_ref[pl.ds(b, rows_sh), :], gate)
            gsh_ref[b - 1] = gv
            psh_ref[b - 1] = gated(pu_ref[pl.ds(b, rows_sh), :], gate)
            return corner(gv)
        return run

    lane = lax.broadcasted_iota(jnp.int32, (CONV_ROWS, W_POOL), 1)
    grp = lane // POOL_GROUP
    half = jnp.left_shift(1, grp)
    row = lax.broadcasted_iota(jnp.int32, (CONV_ROWS, W_POOL), 0)

    def shifted(base_ref, sh_ref, c0, j):
        src = base_ref if j % V7X_SUBLANES == 0 else sh_ref.at[j % V7X_SUBLANES - 1]
        return src[pl.ds(c0 + (j // V7X_SUBLANES) * V7X_SUBLANES, CONV_ROWS), :]

    def conv(c0):
        def run(gate):
            wts = gated(dww_ref[...], gate)
            acc = jnp.broadcast_to(cvec_ref[0:1, :], (CONV_ROWS, W_CONV))
            for k in range(CONV_WIDTH):
                acc = acc + wts[k:k + 1, :] * shifted(g_ref, gsh_ref, c0, k + 1)
            mu = jnp.mean(acc, axis=-1, keepdims=True)
            yc = acc - mu
            var = jnp.mean(yc * yc, axis=-1, keepdims=True)
            yn = yc * lax.rsqrt(var + EPS) * cvec_ref[1:2, :] + cvec_ref[2:3, :]
            act = yn * _sigmoid(yn)
            cv_ref[c0:c0 + CONV_ROWS, :] = act.astype(BF16)
            return corner(act)
        return run

    def pool(c0):
        def run(gate):
            ld = lambda d: shifted(pu_ref, psh_ref, c0, HALO + d)
            u0 = ld(0)
            s2 = gated(ld(-1) + u0, gate)
            s4 = s2 + ld(-2) + ld(1)
            s8 = s4 + ld(-4) + ld(-3) + ld(2) + ld(3)
            s16 = s8 + ld(-8) + ld(-7) + ld(-6) + ld(-5) + ld(4) + ld(5) + ld(6) + ld(7)
            win = jnp.where(grp == 0, s2, jnp.where(grp == 1, s4, jnp.where(grp == 2, s8, s16)))
            t = tile * tm + c0 + row
            cnt = jnp.minimum(t + half, n_seq) - jnp.maximum(t - half, 0)
            centred = win / cnt.astype(F32) - u0
            pl_ref[c0:c0 + CONV_ROWS, :] = centred.astype(BF16)
            return corner(centred)
        return run

    units = [stage] + [copy(b) for b in range(1, V7X_SUBLANES)]
    for c in range(tm // CONV_ROWS):
        units += [conv(c * CONV_ROWS), pool(c * CONV_ROWS)]
    return units


def _mixffn_kernel(n_tiles, n_seq, x_ref, attn_ref, yf_ref,
                   amc_ref, apc_ref, anc_ref, umc_ref, upc_ref, unc_ref,
                   amn_ref, apn_ref, ann_ref, umn_ref, upn_ref, unn_ref,
                   mod_ref, g2_ref, wf_ref, dww_ref, cvec_ref, wpw_ref, wpl_ref, wo_ref,
                   wfi_ref, wfo_ref, o_ref, g_ref, gsh_ref, pu_ref, psh_ref, cv_ref, pl_ref):
    tm = x_ref.shape[1]
    s = pl.program_id(0)
    n_steps = pl.num_programs(0)
    scratch = (dww_ref, cvec_ref, g_ref, gsh_ref, pu_ref, psh_ref, cv_ref, pl_ref)

    @pl.when(s == 0)
    def _():
        for unit in _conv_pool_units(0, n_tiles, n_seq, tm, amc_ref, apc_ref, anc_ref, umc_ref,
                                     upc_ref, unc_ref, *scratch):
            unit(None)

    cv = cv_ref[...]
    pooled = pl_ref[...]
    nxt_tile = jnp.minimum(s + 1, n_steps - 1) % n_tiles
    units = _conv_pool_units(nxt_tile, n_tiles, n_seq, tm, amn_ref, apn_ref, ann_ref, umn_ref,
                             upn_ref, unn_ref, *scratch)
    n_pre = V7X_SUBLANES
    slots = [units[:n_pre // 2], units[n_pre // 2:n_pre]]
    slots += [units[i:i + 2] for i in range(n_pre, len(units), 2)]

    def run_slot(gate_src):
        if not slots:
            return []
        gate = None if gate_src is None else _exact_zero(
            gate_src[-V7X_SUBLANES:, -V7X_LANES:])
        return [unit(gate) for unit in slots.pop(0)]

    def tied(lhs_f32, tokens):
        lhs = lhs_f32.astype(BF16)
        if not tokens:
            return lhs
        zero = sum(_exact_zero(t) for t in tokens)
        head = lhs_f32[0:V7X_BF16_ROWS] + jnp.tile(
            zero, (V7X_BF16_ROWS // V7X_SUBLANES, lhs_f32.shape[1] // V7X_LANES))
        return jnp.concatenate([head.astype(BF16), lhs[V7X_BF16_ROWS:]], axis=0)

    tokens = run_slot(None)
    four = _dot(yf_ref[...].astype(BF16), wf_ref[...]).astype(BF16)
    conv = _dot(cv, wpw_ref[...]).astype(BF16)
    pool = (_dot(pooled, wpl_ref[...]) * cvec_ref[3:4, :]).astype(BF16)
    mix = (_dot(attn_ref[0], wo_ref[0:W_ATTN, :])
           + _dot(four, wo_ref[W_ATTN:2 * W_ATTN, :])
           + _dot(conv, wo_ref[2 * W_ATTN:3 * W_ATTN, :])
           + _dot(pool, wo_ref[3 * W_ATTN:, :]))
    x1 = x_ref[0] + mod_ref[MOD_GATE1:MOD_GATE1 + 1, :] * mix

    ms = jnp.mean(x1 * x1, axis=-1, keepdims=True)
    gain = g2_ref[...] * (1.0 + mod_ref[MOD_SCALE2:MOD_SCALE2 + 1, :])
    xn_f32 = (x1 * lax.rsqrt(ms + EPS)) * gain + mod_ref[MOD_SHIFT2:MOD_SHIFT2 + 1, :]
    prev = mix
    acc = None
    for lo, size in FFN_CHUNKS:
        lhs = tied(xn_f32, tokens)
        tokens = run_slot(prev)
        a = _dot(lhs, wfi_ref[:, lo:lo + size])
        lhs = tied(xn_f32, tokens)
        tokens = run_slot(a)
        g = _dot(lhs, wfi_ref[:, D_FF + lo:D_FF + lo + size])
        lhs = tied((a * _sigmoid(a)) * g, tokens)
        tokens = run_slot(g)
        part = _dot(lhs, wfo_ref[lo:lo + size, :])
        prev = part
        acc = part if acc is None else acc + part
    assert not slots
    out = x1 + mod_ref[MOD_GATE2:MOD_GATE2 + 1, :] * acc
    if tokens:
        zero = sum(_exact_zero(t) for t in tokens)
        out = out + jnp.tile(zero, (tm // V7X_SUBLANES, out.shape[1] // V7X_LANES))
    o_ref[0] = out


def _mix_ffn(layer, x, attn, yf, a_conv, u_pool, mod6, mod_row, g2, w_f, dw_w, cvec, w_pw,
             w_pool_bd, w_o, w_fi, w_fo, tm):
    bsz, n, d = x.shape
    nt = n // tm
    n_steps = bsz * nt
    hb = tm // HALO
    last_halo = n // HALO - 1

    def nxt(s):
        s1 = jnp.minimum(s + 1, n_steps - 1)
        return s1 // nt, s1 % nt

    tok = lambda width: pl.BlockSpec((1, tm, width), lambda s: (s // nt, s % nt, 0))
    cur_main = lambda w: pl.BlockSpec((1, tm, w), lambda s: (0, 0, 0))
    cur_prev = lambda w: pl.BlockSpec((1, HALO, w), lambda s: (0, 0, 0))
    cur_next = lambda w: pl.BlockSpec((1, HALO, w), lambda s: (0, min(hb, last_halo), 0))
    nxt_main = lambda w: pl.BlockSpec((1, tm, w), lambda s: (nxt(s)[0], nxt(s)[1], 0))
    nxt_prev = lambda w: pl.BlockSpec(
        (1, HALO, w), lambda s: (nxt(s)[0], jnp.maximum(nxt(s)[1] * hb - 1, 0), 0))
    nxt_next = lambda w: pl.BlockSpec(
        (1, HALO, w), lambda s: (nxt(s)[0], jnp.minimum((nxt(s)[1] + 1) * hb, last_halo), 0))
    rows_sh = tm + 2 * HALO - V7X_SUBLANES
    return pl.pallas_call(
        functools.partial(_mixffn_kernel, nt, n),
        grid=(n_steps,),
        in_specs=[
            tok(d), tok(W_ATTN),
            pl.BlockSpec((tm, W_FOURIER), lambda s: (s % nt, s // nt)),
            cur_main(2 * W_CONV), cur_prev(2 * W_CONV), cur_next(2 * W_CONV),
            cur_main(W_POOL), cur_prev(W_POOL), cur_next(W_POOL),
            nxt_main(2 * W_CONV), nxt_prev(2 * W_CONV), nxt_next(2 * W_CONV),
            nxt_main(W_POOL), nxt_prev(W_POOL), nxt_next(W_POOL),
            pl.BlockSpec((None, None, N_MOD, d), lambda s: (layer, mod_row(s // nt), 0, 0)),
            _layer_resident(layer, (1, d)),
            _layer_resident(layer, (W_FOURIER, W_FOURIER)),
            _layer_resident(layer, (CONV_TAP_ROWS, W_CONV)),
            _layer_resident(layer, (V7X_SUBLANES, W_CONV)),
            _layer_resident(layer, (W_CONV, W_CONV)),
            _layer_resident(layer, (W_POOL, W_POOL)),
            _layer_resident(layer, (4 * W_ATTN, d)),
            _layer_resident(layer, (d, 2 * D_FF)),
            _layer_resident(layer, (D_FF, d)),
        ],
        out_specs=tok(d),
        out_shape=jax.ShapeDtypeStruct((bsz, n, d), F32),
        scratch_shapes=[
            pltpu.VMEM((tm + 2 * HALO, W_CONV), F32),
            pltpu.VMEM((V7X_SUBLANES - 1, rows_sh, W_CONV), F32),
            pltpu.VMEM((tm + 2 * HALO, W_POOL), F32),
            pltpu.VMEM((V7X_SUBLANES - 1, rows_sh, W_POOL), F32),
            pltpu.VMEM((tm, W_CONV), BF16),
            pltpu.VMEM((tm, W_POOL), BF16),
        ],
        compiler_params=pltpu.CompilerParams(
            dimension_semantics=("arbitrary",), vmem_limit_bytes=56 * 1024 * 1024),
        name="mix_ffn_n%d" % n,
    )(x, attn, yf, a_conv, a_conv, a_conv, u_pool, u_pool, u_pool,
      a_conv, a_conv, a_conv, u_pool, u_pool, u_pool, mod6, g2,
      w_f, dw_w, cvec, w_pw, w_pool_bd, w_o, w_fi, w_fo)


def _pool_block_diag(w_pool):
    depth, groups = w_pool.shape[:2]
    eye = jnp.eye(groups, dtype=w_pool.dtype)
    blocks = w_pool[:, :, :, None, :] * eye[None, :, None, :, None]
    return blocks.reshape(depth, W_POOL, W_POOL)


def kernel(x, c, ctx, c_ctx, w_mod, b_mod, g_norm1, g_norm2, w_in, q_norm_g, k_norm_g,
           w_fourier, conv_dw_w, conv_dw_b, conv_ln_g, conv_ln_b, w_conv_pw, w_pool,
           pool_scale, w_out, w_ffn_in, w_ffn_out):
    bsz, n, d = x.shape
    n_ctx = ctx.shape[1]
    depth = w_in.shape[0]
    tm_lat, tm_ctx = 512, n_ctx
    tq_lat, tk_lat = 512, 1024

    cc = jnp.concatenate([c, c_ctx[None, :],
                          jnp.zeros((V7X_SUBLANES - bsz - 1, d), F32)], axis=0)
    mod6 = _modulation(cc, w_mod, b_mod).reshape(depth, V7X_SUBLANES, N_MOD, d)
    lat_row = lambda b: b
    ctx_row = lambda b: bsz

    w_in_b = w_in.astype(BF16)
    w_f = w_fourier.astype(BF16)
    w_pw = w_conv_pw.astype(BF16)
    w_pl = _pool_block_diag(w_pool).astype(BF16)
    w_o = w_out.astype(BF16)
    w_fi = w_ffn_in.astype(BF16)
    w_fo = w_ffn_out.astype(BF16)
    qk_gain = jnp.concatenate([jnp.tile(q_norm_g, (1, N_Q_HEADS)),
                               jnp.tile(k_norm_g, (1, N_KV_HEADS))], axis=1)[:, None, :]
    g1 = g_norm1[:, None, :]
    g2 = g_norm2[:, None, :]
    dw_taps = jnp.concatenate(
        [conv_dw_w, jnp.zeros((depth, CONV_TAP_ROWS - CONV_WIDTH, W_CONV), F32)], axis=1)
    cvec = jnp.concatenate(
        [conv_dw_b[:, None], conv_ln_g[:, None], conv_ln_b[:, None], pool_scale[:, None],
         jnp.zeros((depth, V7X_SUBLANES - 4, W_CONV), F32)], axis=1)

    h = ctx
    for l in range(DEPTH):
        last = l == DEPTH - 1
        qc, kc, vc, gfc, ac, uc = _inproj(l, h, mod6, ctx_row, g1, w_in_b, qk_gain, False, tm_ctx)
        q, k, v, gf, a, u = _inproj(l, x, mod6, lat_row, g1, w_in_b, qk_gain, True, 2 * tm_lat)

        attn = _attention(q, [(kc, vc), (k, v)], tq_lat, tk_lat)
        yf = _fourier_latent(gf)
        x_new = _mix_ffn(l, x, attn, yf, a, u, mod6, lat_row, g2, w_f, dw_taps, cvec, w_pw, w_pl,
                         w_o, w_fi, w_fo, tm_lat)

        if not last:
            attn_c = _attention(qc, [(kc, vc)], n_ctx, n_ctx)
            yfc = _fourier_ctx(gfc)
            h = _mix_ffn(l, h, attn_c, yfc, ac, uc, mod6, ctx_row, g2, w_f, dw_taps, cvec, w_pw,
                         w_pl, w_o, w_fi, w_fo, tm_ctx)
        x = x_new
    return x
```

```python
import functools

import numpy as np
import jax
import jax.numpy as jnp
from jax import lax
from jax.experimental import pallas as pl
from jax.experimental.pallas import tpu as pltpu

F32 = jnp.float32
BF16 = jnp.bfloat16

D_MODEL = 1024
DEPTH = 2
GRID_W = 64
HEAD_DIM = 64
N_Q_HEADS = 4
N_KV_HEADS = 2
W_ATTN = 256
W_FOURIER = 256
W_CONV = 256
W_POOL = 256
CONV_WIDTH = 31
POOL_GROUP = 64
D_FF = 2816
ROPE_THETA = 10000.0
EPS = 1e-6
ATTN_SCALE = HEAD_DIM ** -0.5
Q_SCALE_LOG2 = ATTN_SCALE * float(np.log2(np.e))
W_QK = W_ATTN + N_KV_HEADS * HEAD_DIM
OFF_V = W_QK
OFF_F = OFF_V + N_KV_HEADS * HEAD_DIM
OFF_C = OFF_F + W_FOURIER
OFF_P = OFF_C + 2 * W_CONV
D_IN = OFF_P + W_POOL

V7X_LANES = 128
V7X_SUBLANES = 8
V7X_BF16_ROWS = 16
V7X_VMEM_BYTES = 64 * 1024 * 1024

MOD_SHIFT1, MOD_SCALE1, MOD_GATE1, MOD_SHIFT2, MOD_SCALE2, MOD_GATE2 = range(6)
N_MOD = 6

HALO = V7X_BF16_ROWS
DFT_N1 = 64
NEG_BIG = -1e30


def _dot(a, b):
    return jnp.dot(a, b, preferred_element_type=F32)


def _sigmoid(x):
    return 1.0 / (1.0 + jnp.exp(-x))


def _split_bf16(x):
    hi = x.astype(BF16)
    lo = (x - hi.astype(F32)).astype(BF16)
    return hi, lo


def _resident(shape):
    nd = len(shape)
    return pl.BlockSpec(shape, lambda *_: (0,) * nd, pipeline_mode=pl.Buffered(1))


def _layer_resident(layer, shape):
    nd = len(shape)
    return pl.BlockSpec((None,) + tuple(shape), lambda *_: (layer,) + (0,) * nd,
                        pipeline_mode=pl.Buffered(1))


def _params(n_grid, vmem_mb):
    return pltpu.CompilerParams(
        dimension_semantics=("parallel",) * n_grid,
        vmem_limit_bytes=vmem_mb * 1024 * 1024)


@functools.lru_cache(maxsize=None)
def _rope_tables(n):
    rows = n // GRID_W
    row = np.repeat(np.arange(rows, dtype=np.float64), GRID_W)
    col = np.tile(np.arange(GRID_W, dtype=np.float64), rows)
    n_freq = HEAD_DIM // 4
    inv_freq = ROPE_THETA ** (-np.arange(n_freq, dtype=np.float64) / n_freq)
    ang = np.concatenate([row[:, None] * inv_freq, col[:, None] * inv_freq], axis=-1)
    cos = np.repeat(np.cos(ang), 2, axis=1)
    sin = np.repeat(np.sin(ang), 2, axis=1)
    sign = np.tile(np.array([-1.0, 1.0]), HEAD_DIM // 2)
    cos2 = np.tile(cos, (1, 2)).astype(np.float32)
    sin2 = np.tile(sin * sign, (1, 2)).astype(np.float32)
    return cos2, sin2


@functools.lru_cache(maxsize=None)
def _head_sum_matrix():
    idx = np.arange(W_QK) // HEAD_DIM
    return np.asarray((idx[:, None] == idx[None, :]).astype(np.float32), dtype=BF16)


@functools.lru_cache(maxsize=None)
def _channel_dft():
    c = np.arange(W_FOURIER)
    same = (c[:, None] // HEAD_DIM) == (c[None, :] // HEAD_DIM)
    ang = 2.0 * np.pi * ((c[:, None] % HEAD_DIM) * (c[None, :] % HEAD_DIM)) / HEAD_DIM
    cr = np.where(same, np.cos(ang), 0.0) / 8.0
    ci = np.where(same, -np.sin(ang), 0.0) / 8.0
    return np.concatenate([cr, ci], axis=1).astype(np.float32)


@functools.lru_cache(maxsize=None)
def _seq_dft_tables(n):
    n1, n2 = DFT_N1, n // DFT_N1
    t1 = np.arange(n1)[:, None, None]
    k2 = np.arange(n2)[None, :, None]
    t2 = np.arange(n2)[None, None, :]
    theta = 2.0 * np.pi * ((k2 * (t1 + n1 * t2)) % n) / n
    dr = np.cos(theta) / np.sqrt(n2)
    di = -np.sin(theta) / np.sqrt(n2)
    tab_r = np.concatenate([dr, di], axis=1)
    tab_i = np.concatenate([-di, dr], axis=1)
    k1 = np.arange(n1)[:, None]
    tt = np.arange(n1)[None, :]
    phi = 2.0 * np.pi * ((k1 * tt) % n1) / n1
    tab_c = np.concatenate([np.cos(phi), np.sin(phi)], axis=1) / np.sqrt(n1)
    return tab_r.astype(np.float32), tab_i.astype(np.float32), tab_c.astype(np.float32)


@functools.lru_cache(maxsize=None)
def _ctx_dft_tables(n):
    k = np.arange(n)[:, None]
    t = np.arange(n)[None, :]
    ang = 2.0 * np.pi * ((k * t) % n) / n
    s = 1.0 / np.sqrt(n)
    return (np.cos(ang) * s).astype(np.float32), (np.sin(ang) * s).astype(np.float32)


def _mod_kernel(c_ref, w_ref, b_ref, o_ref):
    c = c_ref[...]
    sc = c * _sigmoid(c)
    a_hi, a_lo = _split_bf16(sc)
    w_hi, w_lo = _split_bf16(w_ref[0])
    o_ref[0] = _dot(a_hi, w_hi) + _dot(a_lo, w_hi) + _dot(a_hi, w_lo) + b_ref[0]


def _modulation(cc, w_mod, b_mod):
    depth, d, d6 = w_mod.shape
    tn = 1536
    return pl.pallas_call(
        _mod_kernel,
        grid=(depth, d6 // tn),
        in_specs=[
            pl.BlockSpec((V7X_SUBLANES, d), lambda l, j: (0, 0)),
            pl.BlockSpec((1, d, tn), lambda l, j: (l, 0, j)),
            pl.BlockSpec((1, 1, tn), lambda l, j: (l, 0, j)),
        ],
        out_specs=pl.BlockSpec((1, V7X_SUBLANES, tn), lambda l, j: (l, 0, j)),
        out_shape=jax.ShapeDtypeStruct((depth, V7X_SUBLANES, d6), F32),
        compiler_params=_params(2, 40),
        name="modulation",
    )(cc, w_mod, b_mod.reshape(depth, 1, d6))


INPROJ_ROWS = 256


def _rope(x, cos, sin):
    lane = lax.broadcasted_iota(jnp.int32, x.shape, 1)
    swapped = jnp.where((lane & 1) == 0,
                        pltpu.roll(x, V7X_LANES - 1, axis=1),
                        pltpu.roll(x, 1, axis=1))
    return x * cos + swapped * sin


def _inproj_kernel(rope, x_ref, mod_ref, g1_ref, w_ref, qkg_ref, bd_ref, cd_ref, *rest):
    if rope:
        cos_ref, sin_ref = rest[:2]
        rest = rest[2:]
    q_ref, k_ref, v_ref, gf_ref, a_ref, u_ref = rest
    tm = x_ref.shape[1]
    rg = min(tm, INPROJ_ROWS)
    gain = g1_ref[...] * (1.0 + mod_ref[MOD_SCALE1:MOD_SCALE1 + 1, :])
    lane = lax.broadcasted_iota(jnp.int32, (rg, V7X_LANES), 1)
    one_col = jnp.where(lane == HEAD_DIM, 1.0, 0.0)
    row = lax.broadcasted_iota(jnp.int32, (V_AUG - HEAD_DIM, rg), 0)
    ones_rows = jnp.where(row == 0, 1.0, 0.0).astype(BF16)
    for r0 in range(0, tm, rg):
        rows = slice(r0, r0 + rg)
        x = x_ref[0, rows, :]
        ms = jnp.mean(x * x, axis=-1, keepdims=True)
        xn = (x * lax.rsqrt(ms + EPS)) * gain + mod_ref[MOD_SHIFT1:MOD_SHIFT1 + 1, :]
        p = _dot(xn.astype(BF16), w_ref[...])

        qk = p[:, :W_QK]
        ss = _dot((qk * qk).astype(BF16), bd_ref[...])
        qkn = qk * lax.rsqrt(ss * (1.0 / HEAD_DIM) + EPS) * qkg_ref[...]
        slabs = []
        for j in range(W_QK // V7X_LANES):
            slab = qkn[:, j * V7X_LANES:(j + 1) * V7X_LANES]
            if rope:
                slab = _rope(slab, cos_ref[rows, :], sin_ref[rows, :])
            slabs.append(slab)
        for j in range(W_ATTN // V7X_LANES):
            qt = (slabs[j] * Q_SCALE_LOG2).T.astype(BF16)
            q_ref[0, 2 * j, :, rows] = qt[:HEAD_DIM]
            q_ref[0, 2 * j + 1, :, rows] = qt[HEAD_DIM:]
        k_ref[0, 0, rows, :] = jnp.where(lane < HEAD_DIM, slabs[2], one_col).astype(BF16)
        k_ref[0, 1, rows, :] = jnp.where(lane < HEAD_DIM, pltpu.roll(slabs[2], HEAD_DIM, axis=1),
                                         one_col).astype(BF16)
        vt = p[:, OFF_V:OFF_F].T.astype(BF16)
        for g in range(N_KV_HEADS):
            v_ref[0, g, 0:HEAD_DIM, rows] = vt[g * HEAD_DIM:(g + 1) * HEAD_DIM]
            v_ref[0, g, HEAD_DIM:, rows] = ones_rows
        gf_ref[0, rows, :] = _dot(p[:, OFF_F:OFF_C].astype(BF16), cd_ref[...]).astype(BF16)
        a_ref[0, rows, :] = p[:, OFF_C:OFF_P].astype(BF16)
        u_ref[0, rows, :] = p[:, OFF_P:].astype(BF16)


def _inproj(layer, x, mod6, mod_row, g1, w_in, qk_gain, rope, tm):
    bsz, n, d = x.shape
    nt = n // tm
    tok = lambda width: pl.BlockSpec((1, tm, width), lambda b, i: (b, i, 0))
    in_specs = [
        tok(d),
        pl.BlockSpec((None, None, N_MOD, d), lambda b, i: (layer, mod_row(b), 0, 0)),
        _layer_resident(layer, (1, d)),
        _layer_resident(layer, (d, D_IN)),
        _layer_resident(layer, (1, W_QK)),
        _resident((W_QK, W_QK)),
        _resident((W_FOURIER, 2 * W_FOURIER)),
    ]
    args = [x, mod6, g1, w_in, qk_gain, _head_sum_matrix(),
            jnp.asarray(_channel_dft()).astype(BF16)]
    if rope:
        cos2, sin2 = _rope_tables(n)
        in_specs += [pl.BlockSpec((tm, V7X_LANES), lambda b, i: (i, 0))] * 2
        args += [cos2, sin2]
    widths = (2 * W_FOURIER, 2 * W_CONV, W_POOL)
    out_specs = [
        pl.BlockSpec((1, N_Q_HEADS, HEAD_DIM, tm), lambda b, i: (b, 0, 0, i)),
        pl.BlockSpec((1, N_KV_HEADS, tm, K_AUG), lambda b, i: (b, 0, i, 0)),
        pl.BlockSpec((1, N_KV_HEADS, V_AUG, tm), lambda b, i: (b, 0, 0, i)),
    ] + [tok(w) for w in widths]
    out_shape = [
        jax.ShapeDtypeStruct((bsz, N_Q_HEADS, HEAD_DIM, n), BF16),
        jax.ShapeDtypeStruct((bsz, N_KV_HEADS, n, K_AUG), BF16),
        jax.ShapeDtypeStruct((bsz, N_KV_HEADS, V_AUG, n), BF16),
    ] + [jax.ShapeDtypeStruct((bsz, n, w), BF16) for w in widths]
    return pl.pallas_call(
        functools.partial(_inproj_kernel, rope),
        grid=(bsz, nt),
        in_specs=in_specs,
        out_specs=out_specs,
        out_shape=out_shape,
        compiler_params=_params(2, 40),
        name="inproj_rope" if rope else "inproj_ctx",
    )(*args)


K_AUG = V7X_LANES
V_AUG = HEAD_DIM + V7X_BF16_ROWS
SAFE_SHIFT_LOG2 = 50.0


def _attn_kernel(chunks, q_ref, *refs):
    n_src = (len(refs) - 3) // 2
    k_refs, v_refs = refs[:n_src], refs[n_src:2 * n_src]
    o_ref, kmax_ref, acc_ref = refs[2 * n_src:]
    n_chunks = len(chunks)
    tq = q_ref.shape[-1]
    q_per_kv = N_Q_HEADS // N_KV_HEADS
    lanes = q_per_kv * tq
    groups = range(N_KV_HEADS)
    i = pl.program_id(1)

    def keys(g, c):
        src, start, size = chunks[c]
        return k_refs[src][0, g, start:start + size, :]

    def values(g, c):
        src, start, size = chunks[c]
        return v_refs[src][0, g, :, start:start + size]

    @pl.when(i == 0)
    def _():
        ones = jnp.ones((K_AUG, K_AUG), BF16)
        for g in groups:
            mx = jnp.zeros((1, K_AUG), F32)
            for c in range(n_chunks):
                kf = keys(g, c).astype(F32)
                norms = _dot((kf * kf).astype(BF16), ones)
                mx = jnp.maximum(mx, jnp.max(norms, axis=0, keepdims=True))
            kmax_ref[g] = jnp.sqrt(mx)

    row = lax.broadcasted_iota(jnp.int32, (K_AUG - HEAD_DIM, lanes), 0)
    q_aug, worst = [], None
    for g in groups:
        q2 = jnp.concatenate([q_ref[0, q_per_kv * g + h] for h in range(q_per_kv)], axis=1)
        qf = q2.astype(F32)
        kmax = jnp.tile(kmax_ref[g], (1, lanes // K_AUG))
        bound = jnp.sqrt(jnp.sum(qf * qf, axis=0, keepdims=True)) * kmax
        worst = bound if worst is None else jnp.maximum(worst, bound)
        shift_rows = jnp.where(row == 0, -bound, 0.0).astype(BF16)
        q_aug.append(jnp.concatenate([q2, shift_rows], axis=0))
    safe = jnp.max(worst) <= SAFE_SHIFT_LOG2

    def logits(g, c):
        return _dot(keys(g, c), q_aug[g])

    @pl.when(safe)
    def _():
        accs = [None] * N_KV_HEADS
        for c in range(n_chunks):
            for g in groups:
                p = jnp.exp2(logits(g, c)).astype(BF16)
                part = _dot(values(g, c), p)
                accs[g] = part if accs[g] is None else accs[g] + part
        for g in groups:
            acc_ref[g] = accs[g]

    @pl.when(jnp.logical_not(safe))
    def _():
        for g in groups:
            m = jnp.full((1, lanes), NEG_BIG, F32)
            acc = jnp.zeros((V_AUG, lanes), F32)
            for c in range(n_chunks):
                s = logits(g, c)
                m_new = jnp.maximum(m, jnp.max(s, axis=0, keepdims=True))
                p = jnp.exp2(s - m_new).astype(BF16)
                acc = jnp.exp2(m - m_new) * acc + _dot(values(g, c), p)
                m = m_new
            acc_ref[g] = acc

    outs = []
    for g in groups:
        o = acc_ref[g, 0:HEAD_DIM, :] * (1.0 / acc_ref[g, HEAD_DIM:HEAD_DIM + 1, :])
        o2 = jnp.concatenate([o[:, h * tq:(h + 1) * tq] for h in range(q_per_kv)], axis=0)
        outs.append(o2.T)
    o_ref[0] = jnp.concatenate(outs, axis=1).astype(BF16)


def _attention(q_t, kv_sources, tq, tk):
    bsz, _, _, nq = q_t.shape
    q_per_kv = N_Q_HEADS // N_KV_HEADS
    chunks = []
    for src, (k_aug, _) in enumerate(kv_sources):
        nk = k_aug.shape[2]
        step = min(tk, nk)
        chunks += [(src, start, step) for start in range(0, nk, step)]
    k_specs = [pl.BlockSpec((1, N_KV_HEADS, k.shape[2], K_AUG), lambda b, i: (b, 0, 0, 0))
               for k, _ in kv_sources]
    v_specs = [pl.BlockSpec((1, N_KV_HEADS, V_AUG, v.shape[3]), lambda b, i: (b, 0, 0, 0))
               for _, v in kv_sources]
    return pl.pallas_call(
        functools.partial(_attn_kernel, tuple(chunks)),
        grid=(bsz, nq // tq),
        in_specs=[pl.BlockSpec((1, N_Q_HEADS, HEAD_DIM, tq), lambda b, i: (b, 0, 0, i))]
        + k_specs + v_specs,
        out_specs=pl.BlockSpec((1, tq, W_ATTN), lambda b, i: (b, i, 0)),
        out_shape=jax.ShapeDtypeStruct((bsz, nq, W_ATTN), BF16),
        scratch_shapes=[pltpu.VMEM((N_KV_HEADS, 1, K_AUG), F32),
                        pltpu.VMEM((N_KV_HEADS, V_AUG, q_per_kv * tq), F32)],
        compiler_params=pltpu.CompilerParams(
            dimension_semantics=("parallel", "arbitrary"),
            vmem_limit_bytes=40 * 1024 * 1024),
        name="attention_%dsrc" % len(kv_sources),
    )(q_t, *[k for k, _ in kv_sources], *[v for _, v in kv_sources])


DFT_A_GROUP = 8


def _dft_a_kernel(bsz, g_ref, tr_ref, ti_ref, o_ref):
    n2 = g_ref.shape[2]
    for j in range(DFT_A_GROUP):
        tab_r = tr_ref[j].astype(BF16)
        tab_i = ti_ref[j].astype(BF16)
        for b in range(bsz):
            res = (_dot(tab_r, g_ref[j, b, :, :W_FOURIER])
                   + _dot(tab_i, g_ref[j, b, :, W_FOURIER:]))
            o_ref[0, j, :, b * W_FOURIER:(b + 1) * W_FOURIER] = res[:n2].astype(BF16)
            o_ref[1, j, :, b * W_FOURIER:(b + 1) * W_FOURIER] = res[n2:].astype(BF16)


def _dft_c_kernel(a_ref, tc_ref, o_ref):
    res = _dot(tc_ref[...], a_ref[...])
    k2s, lanes = o_ref.shape[1], o_ref.shape[2]
    for j in range(k2s):
        o_ref[:, j, :] = res[:, j * lanes:(j + 1) * lanes]


def _fourier_latent(gf):
    bsz, n, _ = gf.shape
    n1, n2 = DFT_N1, n // DFT_N1
    tab_r, tab_i, tab_c = _seq_dft_tables(n)
    g_t = jnp.transpose(gf.reshape(bsz, n2, n1, 2 * W_FOURIER), (2, 0, 1, 3))
    lanes = bsz * W_FOURIER
    a = pl.pallas_call(
        functools.partial(_dft_a_kernel, bsz),
        grid=(n1 // DFT_A_GROUP,),
        in_specs=[
            pl.BlockSpec((DFT_A_GROUP, bsz, n2, 2 * W_FOURIER), lambda t: (t, 0, 0, 0)),
            pl.BlockSpec((DFT_A_GROUP, 2 * n2, n2), lambda t: (t, 0, 0)),
            pl.BlockSpec((DFT_A_GROUP, 2 * n2, n2), lambda t: (t, 0, 0)),
        ],
        out_specs=pl.BlockSpec((2, DFT_A_GROUP, n2, lanes), lambda t: (0, t, 0, 0)),
        out_shape=jax.ShapeDtypeStruct((2, n1, n2, lanes), BF16),
        compiler_params=_params(1, 40),
        name="dft_stage_a",
    )(g_t, tab_r, tab_i)
    a2 = a.reshape(2 * n1, n2 * lanes)
    k2s = V7X_SUBLANES
    y = pl.pallas_call(
        _dft_c_kernel,
        grid=(n2 // k2s,),
        in_specs=[pl.BlockSpec((2 * n1, k2s * lanes), lambda j: (0, j)),
                  _resident((n1, 2 * n1))],
        out_specs=pl.BlockSpec((n1, k2s, lanes), lambda j: (0, j, 0)),
        out_shape=jax.ShapeDtypeStruct((n1, n2, lanes), F32),
        compiler_params=_params(1, 40),
        name="dft_stage_c",
    )(a2, jnp.asarray(tab_c).astype(BF16))
    return y.reshape(n, lanes)


def _dft_ctx_kernel(g_ref, c_ref, s_ref, o_ref):
    o_ref[...] = (_dot(c_ref[...], g_ref[0, :, :W_FOURIER])
                  + _dot(s_ref[...], g_ref[0, :, W_FOURIER:])).astype(BF16)


def _fourier_ctx(gf):
    bsz, n, _ = gf.shape
    tab_cos, tab_sin = _ctx_dft_tables(n)
    return pl.pallas_call(
        _dft_ctx_kernel,
        grid=(bsz,),
        in_specs=[pl.BlockSpec((1, n, 2 * W_FOURIER), lambda b: (b, 0, 0)),
                  _resident((n, n)), _resident((n, n))],
        out_specs=pl.BlockSpec((n, W_FOURIER), lambda b: (0, b)),
        out_shape=jax.ShapeDtypeStruct((n, bsz * W_FOURIER), BF16),
        compiler_params=_params(1, 40),
        name="dft_ctx",
    )(gf, jnp.asarray(tab_cos).astype(BF16), jnp.asarray(tab_sin).astype(BF16))


CONV_ROWS = 64
CONV_TAP_ROWS = 32
FFN_CHUNKS = ((0, 768), (768, 768), (1536, 768), (2304, 512))


def _exact_zero(v):
    return jnp.minimum(jnp.abs(v), 0.0)


def _conv_pool_units(tile, n_tiles, n_seq, tm, am_ref, ap_ref, an_ref, um_ref, up_ref, un_ref,
                     dww_ref, cvec_ref, g_ref, gsh_ref, pu_ref, psh_ref, cv_ref, pl_ref):
    first = tile == 0
    last = tile == n_tiles - 1
    rows_sh = gsh_ref.shape[1]
    corner = lambda v: v[0:V7X_SUBLANES, 0:V7X_LANES]

    def gated(v, gate):
        if gate is None:
            return v
        rows, cols = v.shape
        return v + jnp.tile(gate, (rows // V7X_SUBLANES, cols // V7X_LANES))

    def glu(a_blk):
        a = a_blk.astype(F32)
        return a[:, :W_CONV] * _sigmoid(a[:, W_CONV:])

    def stage(gate):
        main = gated(glu(am_ref[0]), gate)
        g_ref[0:HALO] = jnp.where(first, 0.0, glu(ap_ref[0]))
        g_ref[HALO:HALO + tm] = main
        g_ref[HALO + tm:] = jnp.where(last, 0.0, glu(an_ref[0]))
        pu_ref[0:HALO] = jnp.where(first, 0.0, up_ref[0].astype(F32))
        pu_ref[HALO:HALO + tm] = um_ref[0].astype(F32)
        pu_ref[HALO + tm:] = jnp.where(last, 0.0, un_ref[0].astype(F32))
        return corner(main)

    def copy(b):
        def run(gate):
            gv = gated(g_ref[pl.ds(b, rows_sh), :], gate)
            gsh_ref[b - 1] = gv
            psh_ref[b - 1] = gated(pu_ref[pl.ds(b, rows_sh), :], gate)
            return corner(gv)
        return run

    lane = lax.broadcasted_iota(jnp.int32, (CONV_ROWS, W_POOL), 1)
    grp = lane // POOL_GROUP
    half = jnp.left_shift(1, grp)
    row = lax.broadcasted_iota(jnp.int32, (CONV_ROWS, W_POOL), 0)

    def shifted(base_ref, sh_ref, c0, j):
        src = base_ref if j % V7X_SUBLANES == 0 else sh_ref.at[j % V7X_SUBLANES - 1]
        return src[pl.ds(c0 + (j // V7X_SUBLANES) * V7X_SUBLANES, CONV_ROWS), :]

    def conv(c0):
        def run(gate):
            wts = gated(dww_ref[...], gate)
            acc = jnp.broadcast_to(cvec_ref[0:1, :], (CONV_ROWS, W_CONV))
            for k in range(CONV_WIDTH):
                acc = acc + wts[k:k + 1, :] * shifted(g_ref, gsh_ref, c0, k + 1)
            mu = jnp.mean(acc, axis=-1, keepdims=True)
            yc = acc - mu
            var = jnp.mean(yc * yc, axis=-1, keepdims=True)
            yn = yc * lax.rsqrt(var + EPS) * cvec_ref[1:2, :] + cvec_ref[2:3, :]
            act = yn * _sigmoid(yn)
            cv_ref[c0:c0 + CONV_ROWS, :] = act.astype(BF16)
            return corner(act)
        return run

    def pool(c0):
        def run(gate):
            ld = lambda d: shifted(pu_ref, psh_ref, c0, HALO + d)
            u0 = ld(0)
            s2 = gated(ld(-1) + u0, gate)
            s4 = s2 + ld(-2) + ld(1)
            s8 = s4 + ld(-4) + ld(-3) + ld(2) + ld(3)
            s16 = s8 + ld(-8) + ld(-7) + ld(-6) + ld(-5) + ld(4) + ld(5) + ld(6) + ld(7)
            win = jnp.where(grp == 0, s2, jnp.where(grp == 1, s4, jnp.where(grp == 2, s8, s16)))
            t = tile * tm + c0 + row
            cnt = jnp.minimum(t + half, n_seq) - jnp.maximum(t - half, 0)
            centred = win / cnt.astype(F32) - u0
            pl_ref[c0:c0 + CONV_ROWS, :] = centred.astype(BF16)
            return corner(centred)
        return run

    units = [stage] + [copy(b) for b in range(1, V7X_SUBLANES)]
    for c in range(tm // CONV_ROWS):
        units += [conv(c * CONV_ROWS), pool(c * CONV_ROWS)]
    return units


def _mixffn_kernel(n_tiles, n_seq, x_ref, attn_ref, yf_ref,
                   amc_ref, apc_ref, anc_ref, umc_ref, upc_ref, unc_ref,
                   amn_ref, apn_ref, ann_ref, umn_ref, upn_ref, unn_ref,
                   mod_ref, g2_ref, wf_ref, dww_ref, cvec_ref, wpw_ref, wpl_ref, wo_ref,
                   wfi_ref, wfo_ref, o_ref, g_ref, gsh_ref, pu_ref, psh_ref, cv_ref, pl_ref):
    tm = x_ref.shape[1]
    s = pl.program_id(0)
    n_steps = pl.num_programs(0)
    scratch = (dww_ref, cvec_ref, g_ref, gsh_ref, pu_ref, psh_ref, cv_ref, pl_ref)

    @pl.when(s == 0)
    def _():
        for unit in _conv_pool_units(0, n_tiles, n_seq, tm, amc_ref, apc_ref, anc_ref, umc_ref,
                                     upc_ref, unc_ref, *scratch):
            unit(None)

    cv = cv_ref[...]
    pooled = pl_ref[...]
    nxt_tile = jnp.minimum(s + 1, n_steps - 1) % n_tiles
    units = _conv_pool_units(nxt_tile, n_tiles, n_seq, tm, amn_ref, apn_ref, ann_ref, umn_ref,
                             upn_ref, unn_ref, *scratch)
    n_pre = V7X_SUBLANES
    slots = [units[:n_pre // 2], units[n_pre // 2:n_pre]]
    slots += [units[i:i + 2] for i in range(n_pre, len(units), 2)]

    def run_slot(gate_src):
        if not slots:
            return []
        gate = None if gate_src is None else _exact_zero(
            gate_src[-V7X_SUBLANES:, -V7X_LANES:])
        return [unit(gate) for unit in slots.pop(0)]

    def tied(lhs_f32, tokens):
        lhs = lhs_f32.astype(BF16)
        if not tokens:
            return lhs
        zero = sum(_exact_zero(t) for t in tokens)
        head = lhs_f32[0:V7X_BF16_ROWS] + jnp.tile(
            zero, (V7X_BF16_ROWS // V7X_SUBLANES, lhs_f32.shape[1] // V7X_LANES))
        return jnp.concatenate([head.astype(BF16), lhs[V7X_BF16_ROWS:]], axis=0)

    tokens = run_slot(None)
    four = _dot(yf_ref[...].astype(BF16), wf_ref[...]).astype(BF16)
    conv = _dot(cv, wpw_ref[...]).astype(BF16)
    pool = (_dot(pooled, wpl_ref[...]) * cvec_ref[3:4, :]).astype(BF16)
    mix = (_dot(attn_ref[0], wo_ref[0:W_ATTN, :])
           + _dot(four, wo_ref[W_ATTN:2 * W_ATTN, :])
           + _dot(conv, wo_ref[2 * W_ATTN:3 * W_ATTN, :])
           + _dot(pool, wo_ref[3 * W_ATTN:, :]))
    x1 = x_ref[0] + mod_ref[MOD_GATE1:MOD_GATE1 + 1, :] * mix

    ms = jnp.mean(x1 * x1, axis=-1, keepdims=True)
    gain = g2_ref[...] * (1.0 + mod_ref[MOD_SCALE2:MOD_SCALE2 + 1, :])
    xn_f32 = (x1 * lax.rsqrt(ms + EPS)) * gain + mod_ref[MOD_SHIFT2:MOD_SHIFT2 + 1, :]
    prev = mix
    acc = None
    for lo, size in FFN_CHUNKS:
        lhs = tied(xn_f32, tokens)
        tokens = run_slot(prev)
        a = _dot(lhs, wfi_ref[:, lo:lo + size])
        lhs = tied(xn_f32, tokens)
        tokens = run_slot(a)
        g = _dot(lhs, wfi_ref[:, D_FF + lo:D_FF + lo + size])
        lhs = tied((a * _sigmoid(a)) * g, tokens)
        tokens = run_slot(g)
        part = _dot(lhs, wfo_ref[lo:lo + size, :])
        prev = part
        acc = part if acc is None else acc + part
    assert not slots
    out = x1 + mod_ref[MOD_GATE2:MOD_GATE2 + 1, :] * acc
    if tokens:
        zero = sum(_exact_zero(t) for t in tokens)
        out = out + jnp.tile(zero, (tm // V7X_SUBLANES, out.shape[1] // V7X_LANES))
    o_ref[0] = out


def _mix_ffn(layer, x, attn, yf, a_conv, u_pool, mod6, mod_row, g2, w_f, dw_w, cvec, w_pw,
             w_pool_bd, w_o, w_fi, w_fo, tm):
    bsz, n, d = x.shape
    nt = n // tm
    n_steps = bsz * nt
    hb = tm // HALO
    last_halo = n // HALO - 1

    def nxt(s):
        s1 = jnp.minimum(s + 1, n_steps - 1)
        return s1 // nt, s1 % nt

    tok = lambda width: pl.BlockSpec((1, tm, width), lambda s: (s // nt, s % nt, 0))
    cur_main = lambda w: pl.BlockSpec((1, tm, w), lambda s: (0, 0, 0))
    cur_prev = lambda w: pl.BlockSpec((1, HALO, w), lambda s: (0, 0, 0))
    cur_next = lambda w: pl.BlockSpec((1, HALO, w), lambda s: (0, min(hb, last_halo), 0))
    nxt_main = lambda w: pl.BlockSpec((1, tm, w), lambda s: (nxt(s)[0], nxt(s)[1], 0))
    nxt_prev = lambda w: pl.BlockSpec(
        (1, HALO, w), lambda s: (nxt(s)[0], jnp.maximum(nxt(s)[1] * hb - 1, 0), 0))
    nxt_next = lambda w: pl.BlockSpec(
        (1, HALO, w), lambda s: (nxt(s)[0], jnp.minimum((nxt(s)[1] + 1) * hb, last_halo), 0))
    rows_sh = tm + 2 * HALO - V7X_SUBLANES
    return pl.pallas_call(
        functools.partial(_mixffn_kernel, nt, n),
        grid=(n_steps,),
        in_specs=[
            tok(d), tok(W_ATTN),
            pl.BlockSpec((tm, W_FOURIER), lambda s: (s % nt, s // nt)),
            cur_main(2 * W_CONV), cur_prev(2 * W_CONV), cur_next(2 * W_CONV),
            cur_main(W_POOL), cur_prev(W_POOL), cur_next(W_POOL),
            nxt_main(2 * W_CONV), nxt_prev(2 * W_CONV), nxt_next(2 * W_CONV),
            nxt_main(W_POOL), nxt_prev(W_POOL), nxt_next(W_POOL),
            pl.BlockSpec((None, None, N_MOD, d), lambda s: (layer, mod_row(s // nt), 0, 0)),
            _layer_resident(layer, (1, d)),
            _layer_resident(layer, (W_FOURIER, W_FOURIER)),
            _layer_resident(layer, (CONV_TAP_ROWS, W_CONV)),
            _layer_resident(layer, (V7X_SUBLANES, W_CONV)),
            _layer_resident(layer, (W_CONV, W_CONV)),
            _layer_resident(layer, (W_POOL, W_POOL)),
            _layer_resident(layer, (4 * W_ATTN, d)),
            _layer_resident(layer, (d, 2 * D_FF)),
            _layer_resident(layer, (D_FF, d)),
        ],
        out_specs=tok(d),
        out_shape=jax.ShapeDtypeStruct((bsz, n, d), F32),
        scratch_shapes=[
            pltpu.VMEM((tm + 2 * HALO, W_CONV), F32),
            pltpu.VMEM((V7X_SUBLANES - 1, rows_sh, W_CONV), F32),
            pltpu.VMEM((tm + 2 * HALO, W_POOL), F32),
            pltpu.VMEM((V7X_SUBLANES - 1, rows_sh, W_POOL), F32),
            pltpu.VMEM((tm, W_CONV), BF16),
            pltpu.VMEM((tm, W_POOL), BF16),
        ],
        compiler_params=pltpu.CompilerParams(
            dimension_semantics=("arbitrary",), vmem_limit_bytes=56 * 1024 * 1024),
        name="mix_ffn_n%d" % n,
    )(x, attn, yf, a_conv, a_conv, a_conv, u_pool, u_pool, u_pool,
      a_conv, a_conv, a_conv, u_pool, u_pool, u_pool, mod6, g2,
      w_f, dw_w, cvec, w_pw, w_pool_bd, w_o, w_fi, w_fo)


def _pool_block_diag(w_pool):
    depth, groups = w_pool.shape[:2]
    eye = jnp.eye(groups, dtype=w_pool.dtype)
    blocks = w_pool[:, :, :, None, :] * eye[None, :, None, :, None]
    return blocks.reshape(depth, W_POOL, W_POOL)


def kernel(x, c, ctx, c_ctx, w_mod, b_mod, g_norm1, g_norm2, w_in, q_norm_g, k_norm_g,
           w_fourier, conv_dw_w, conv_dw_b, conv_ln_g, conv_ln_b, w_conv_pw, w_pool,
           pool_scale, w_out, w_ffn_in, w_ffn_out):
    bsz, n, d = x.shape
    n_ctx = ctx.shape[1]
    depth = w_in.shape[0]
    tm_lat, tm_ctx = 512, n_ctx
    tq_lat, tk_lat = 256, 1024

    cc = jnp.concatenate([c, c_ctx[None, :],
                          jnp.zeros((V7X_SUBLANES - bsz - 1, d), F32)], axis=0)
    mod6 = _modulation(cc, w_mod, b_mod).reshape(depth, V7X_SUBLANES, N_MOD, d)
    lat_row = lambda b: b
    ctx_row = lambda b: bsz

    w_in_b = w_in.astype(BF16)
    w_f = w_fourier.astype(BF16)
    w_pw = w_conv_pw.astype(BF16)
    w_pl = _pool_block_diag(w_pool).astype(BF16)
    w_o = w_out.astype(BF16)
    w_fi = w_ffn_in.astype(BF16)
    w_fo = w_ffn_out.astype(BF16)
    qk_gain = jnp.concatenate([jnp.tile(q_norm_g, (1, N_Q_HEADS)),
                               jnp.tile(k_norm_g, (1, N_KV_HEADS))], axis=1)[:, None, :]
    g1 = g_norm1[:, None, :]
    g2 = g_norm2[:, None, :]
    dw_taps = jnp.concatenate(
        [conv_dw_w, jnp.zeros((depth, CONV_TAP_ROWS - CONV_WIDTH, W_CONV), F32)], axis=1)
    cvec = jnp.concatenate(
        [conv_dw_b[:, None], conv_ln_g[:, None], conv_ln_b[:, None], pool_scale[:, None],
         jnp.zeros((depth, V7X_SUBLANES - 4, W_CONV), F32)], axis=1)

    h = ctx
    for l in range(DEPTH):
        last = l == DEPTH - 1
        qc, kc, vc, gfc, ac, uc = _inproj(l, h, mod6, ctx_row, g1, w_in_b, qk_gain, False, tm_ctx)
        q, k, v, gf, a, u = _inproj(l, x, mod6, lat_row, g1, w_in_b, qk_gain, True, 2 * tm_lat)

        attn = _attention(q, [(kc, vc), (k, v)], tq_lat, tk_lat)
        yf = _fourier_latent(gf)
        x_new = _mix_ffn(l, x, attn, yf, a, u, mod6, lat_row, g2, w_f, dw_taps, cvec, w_pw, w_pl,
                         w_o, w_fi, w_fo, tm_lat)

        if not last:
            attn_c = _attention(qc, [(kc, vc)], n_ctx, n_ctx)
            yfc = _fourier_ctx(gfc)
            h = _mix_ffn(l, h, attn_c, yfc, ac, uc, mod6, ctx_row, g2, w_f, dw_taps, cvec, w_pw,
                         w_pl, w_o, w_fi, w_fo, tm_ctx)
        x = x_new
    return x
```
